```python
import math
import jax, jax.numpy as jnp
from jax import lax
import numpy as np

D_MODEL = 1024
BATCH = 16
SEQ = 256
DEPTH = 4
DEC_BATCH = 2
DEC_SEQ = 4096
PAST_LEN = 512

GRID_W = 64
N_MIXERS = 3
N_MLA = (DEPTH + 2) // 3
N_S5 = (DEPTH + 1) // 3
N_DIFF = DEPTH // 3
N_DENSE = (DEPTH + 1) // 2
N_MOE = DEPTH // 2

MLA_HEADS = 8
MLA_Q_RANK = 384
MLA_KV_RANK = 256
MLA_D_NOPE = 128
MLA_D_ROPE = 64
MLA_D_V = 128
S5_GROUP = 16
S5_GROUPS = D_MODEL // S5_GROUP
S5_STATE = 64
DT_MIN = 0.001
DT_MAX = 0.1
DIFF_HEADS = 8
DIFF_DH = D_MODEL // (2 * DIFF_HEADS)
D_FF = 2816
N_EXPERTS = 8
TOP_K = 2

Q_BLOCK = 128
ROPE_THETA = 10000.0
EPS = 1e-6

kernel_name = "hybrid_mla_s5_diffattn_prefix_dit_step"


def rmsnorm(x, g):
    xf = x.astype(jnp.float32)
    y = xf * lax.rsqrt(jnp.mean(xf * xf, axis=-1, keepdims=True) + EPS)
    return (y * g.astype(jnp.float32)).astype(x.dtype)


def grid_positions(n_tokens):
    rows = n_tokens // GRID_W
    row = jnp.repeat(jnp.arange(rows, dtype=jnp.int32), GRID_W)
    col = jnp.tile(jnp.arange(GRID_W, dtype=jnp.int32), rows)
    return row, col


def rope_1d(x, pos):
    half = x.shape[-1] // 2
    freqs = ROPE_THETA ** (-jnp.arange(half, dtype=jnp.float32) / half)
    ang = pos.astype(jnp.float32)[:, None] * freqs[None, :]
    shape = (ang.shape[0],) + (1,) * (x.ndim - 3) + (half,)
    cos = jnp.cos(ang).reshape(shape).astype(x.dtype)
    sin = jnp.sin(ang).reshape(shape).astype(x.dtype)
    x1, x2 = x[..., :half], x[..., half:]
    return jnp.concatenate([x1 * cos - x2 * sin, x1 * sin + x2 * cos], axis=-1)


def rope_2d(x, row, col):
    h = x.shape[-1] // 2
    return jnp.concatenate([rope_1d(x[..., :h], row), rope_1d(x[..., h:], col)], axis=-1)


def ada_params(cond, w, b):
    m = jax.nn.silu(cond) @ w + b
    return jnp.split(m, 6, axis=-1)


def modulate(h, shift, scale):
    return h * (1 + scale[:, None]) + shift[:, None]


def map_query_blocks(fn, qs):
    B, T = qs[0].shape[:2]
    nb = T // Q_BLOCK
    split = lambda a: jnp.moveaxis(a.reshape((B, nb, Q_BLOCK) + a.shape[2:]), 1, 0)
    out = lax.map(fn, tuple(split(a) for a in qs))
    out = jnp.moveaxis(out, 0, 1)
    return out.reshape((B, T) + out.shape[3:])


def mla_queries(h, w_dq, q_norm, w_uq):
    B, T, _ = h.shape
    q = rmsnorm(h @ w_dq, q_norm) @ w_uq
    q = q.reshape(B, T, MLA_HEADS, MLA_D_NOPE + MLA_D_ROPE)
    return q[..., :MLA_D_NOPE], q[..., MLA_D_NOPE:]


def mla_compress(h, w_dkv, kv_norm):
    ckv = h @ w_dkv
    return rmsnorm(ckv[..., :MLA_KV_RANK], kv_norm), ckv[..., MLA_KV_RANK:]


def mla_attend(q_nope, q_rope, ckv, k_rope, w_ukv, w_o):
    B, S, _ = ckv.shape
    kv = (ckv @ w_ukv).reshape(B, S, MLA_HEADS, MLA_D_NOPE + MLA_D_V)
    k_nope, v = kv[..., :MLA_D_NOPE], kv[..., MLA_D_NOPE:]
    scale = (MLA_D_NOPE + MLA_D_ROPE) ** -0.5

    def block(qs):
        qn, qr = qs
        s = jnp.einsum('bqhd,bkhd->bhqk', qn, k_nope) + jnp.einsum('bqhr,bkr->bhqk', qr, k_rope)
        p = jax.nn.softmax(s.astype(jnp.float32) * scale, axis=-1).astype(v.dtype)
        return jnp.einsum('bhqk,bkhd->bqhd', p, v)

    o = map_query_blocks(block, (q_nope, q_rope))
    return o.reshape(B, -1, MLA_HEADS * MLA_D_V) @ w_o


def mla_context(h, w_dq, q_norm, w_uq, w_dkv, kv_norm, w_ukv, w_o):
    qn, qr = mla_queries(h, w_dq, q_norm, w_uq)
    ckv, kr = mla_compress(h, w_dkv, kv_norm)
    return mla_attend(qn, qr, ckv, kr, w_ukv, w_o), ckv, kr


def mla_latent(h, cache_ckv, cache_kr, w_dq, q_norm, w_uq, w_dkv, kv_norm, w_ukv, w_o):
    row, col = grid_positions(h.shape[1])
    qn, qr = mla_queries(h, w_dq, q_norm, w_uq)
    qr = rope_2d(qr, row, col)
    ckv, kr = mla_compress(h, w_dkv, kv_norm)
    kr = rope_2d(kr, row, col)
    ckv_all = jnp.concatenate([cache_ckv, ckv], axis=1)
    kr_all = jnp.concatenate([cache_kr, kr], axis=1)
    return mla_attend(qn, qr, ckv_all, kr_all, w_ukv, w_o)


def s5_discretise(a_re, a_im, log_dt, b_re, b_im):
    f = lambda a: a.astype(jnp.float32)
    lam = lax.complex(f(a_re), f(a_im))
    dt = jnp.exp(f(log_dt))[..., None]
    abar = jnp.exp(lam * dt)
    bbar = ((abar - 1) / lam)[..., None] * lax.complex(f(b_re), f(b_im))
    return abar, bbar


def s5_scan(ug, abar, bbar, cmat, h0, reverse):
    bu = jnp.einsum('btgc,gpc->btgp', ug.astype(jnp.complex64), bbar)
    if h0 is not None:
        edge = -1 if reverse else 0
        bu = bu.at[:, edge].add(abar * h0)
    a = jnp.broadcast_to(abar, bu.shape)

    def combine(e1, e2):
        a1, b1 = e1
        a2, b2 = e2
        return a1 * a2, a2 * b1 + b2

    _, hs = lax.associative_scan(combine, (a, bu), axis=1, reverse=reverse)
    y = jnp.einsum('btgp,gcp->btgc', hs, cmat).real
    return y, hs


def s5_mixer(u, h0_re, h0_im, a_re, a_im, log_dt, b_re, b_im, c_re, c_im, d, w_glu, want_state):
    B, T, _ = u.shape
    f = lambda a: a.astype(jnp.float32)
    abar, bbar = s5_discretise(a_re, a_im, log_dt, b_re, b_im)
    cmat = lax.complex(f(c_re), f(c_im))
    uf = f(u)
    ug = uf.reshape(B, T, S5_GROUPS, S5_GROUP)
    y = f(d) * uf
    finals = []
    for direction in range(2):
        h0 = None if h0_re is None else lax.complex(f(h0_re[:, direction]), f(h0_im[:, direction]))
        rev = direction == 1
        yd, hs = s5_scan(ug, abar[direction], bbar[direction], cmat[direction], h0, rev)
        y = y + yd.reshape(B, T, D_MODEL)
        if want_state:
            finals.append(hs[:, 0] if rev else hs[:, -1])
    g = jax.nn.gelu(y).astype(u.dtype)
    ga, gb = jnp.split(g @ w_glu, 2, axis=-1)
    out = ga * jax.nn.sigmoid(gb)
    if want_state:
        final = jnp.stack(finals, axis=1)
        return out, final.real.astype(u.dtype), final.imag.astype(u.dtype)
    return out


def diff_lambda(lq1, lk1, lq2, lk2, lam_init):
    f = lambda a: a.astype(jnp.float32)
    return jnp.exp(jnp.sum(f(lq1) * f(lk1))) - jnp.exp(jnp.sum(f(lq2) * f(lk2))) + lam_init


def diff_qkv(h, w_qkv):
    B, T, _ = h.shape
    q, k, v = jnp.split(h @ w_qkv, 3, axis=-1)
    return (q.reshape(B, T, 2 * DIFF_HEADS, DIFF_DH),
            k.reshape(B, T, 2 * DIFF_HEADS, DIFF_DH),
            v.reshape(B, T, DIFF_HEADS, 2 * DIFF_DH))


def diff_attend(q, k, v, lam, lam_init, subln, w_o):
    B, T = q.shape[:2]
    S = k.shape[1]
    scale = DIFF_DH ** -0.5

    def block(qs):
        (qb,) = qs
        s = jnp.einsum('bqhd,bkhd->bhqk', qb, k).astype(jnp.float32) * scale
        p = jax.nn.softmax(s, axis=-1).reshape(B, DIFF_HEADS, 2, Q_BLOCK, S)
        att = (p[:, :, 0] - lam * p[:, :, 1]).astype(v.dtype)
        return jnp.einsum('bhqk,bkhd->bqhd', att, v)

    o = map_query_blocks(block, (q,))
    o = rmsnorm(o, subln) * (1 - lam_init)
    return o.reshape(B, T, D_MODEL) @ w_o


def diff_context(h, w_qkv, lam, lam_init, subln, w_o):
    q, k, v = diff_qkv(h, w_qkv)
    return diff_attend(q, k, v, lam, lam_init, subln, w_o), k, v


def diff_latent(h, cache_k, cache_v, w_qkv, lam, lam_init, subln, w_o):
    row, col = grid_positions(h.shape[1])
    q, k, v = diff_qkv(h, w_qkv)
    q = rope_2d(q, row, col)
    k = rope_2d(k, row, col)
    k_all = jnp.concatenate([cache_k, k], axis=1)
    v_all = jnp.concatenate([cache_v, v], axis=1)
    return diff_attend(q, k_all, v_all, lam, lam_init, subln, w_o)


def swiglu(h, w_in, w_out):
    a, b = jnp.split(h @ w_in, 2, axis=-1)
    return (jax.nn.silu(a) * b) @ w_out


def moe_swiglu(h, w_router, b_router, w_in, w_out):
    shp = h.shape
    hf = h.reshape(-1, D_MODEL)
    logits = (hf @ w_router + b_router).astype(jnp.float32)
    top_v, top_i = lax.top_k(logits, TOP_K)
    gates = jax.nn.softmax(top_v, axis=-1)
    comb = jnp.sum(jax.nn.one_hot(top_i, N_EXPERTS, dtype=jnp.float32) * gates[..., None], axis=1)
    comb = comb.astype(hf.dtype)
    y = jnp.zeros_like(hf)
    for e in range(N_EXPERTS):
        y = y + comb[:, e:e + 1] * swiglu(hf, w_in[e], w_out[e])
    return y.reshape(shp)


def setup_inputs(seed: int = 0) -> dict:
    key = jax.random.key(seed)
    ks = iter(jax.random.split(key, 64))

    def nrm(shape, scale=1.0):
        return scale * jax.random.normal(next(ks), shape, jnp.float32)

    def gain(shape):
        return 1.0 + nrm(shape, 0.02)

    D = D_MODEL
    L = PAST_LEN
    inp = {}
    inp["x_prompt"] = nrm((BATCH, SEQ, D))
    inp["x_sample"] = nrm((DEC_BATCH, DEC_SEQ, D))
    inp["c"] = nrm((DEC_BATCH, D))
    inp["c_ctx"] = nrm((D,))
    inp["cache_mla_ckv"] = nrm((DEC_BATCH, N_MLA, L, MLA_KV_RANK))
    inp["cache_mla_krope"] = nrm((DEC_BATCH, N_MLA, L, MLA_D_ROPE))
    inp["state_s5_re"] = nrm((DEC_BATCH, N_S5, 2, S5_GROUPS, S5_STATE), 0.3)
    inp["state_s5_im"] = nrm((DEC_BATCH, N_S5, 2, S5_GROUPS, S5_STATE), 0.3)
    inp["cache_diff_k"] = nrm((DEC_BATCH, N_DIFF, L, 2 * DIFF_HEADS, DIFF_DH))
    inp["cache_diff_v"] = nrm((DEC_BATCH, N_DIFF, L, DIFF_HEADS, 2 * DIFF_DH))
    inp["ada_w"] = nrm((DEPTH, D, 6 * D), D ** -0.5)
    inp["ada_b"] = nrm((DEPTH, 6 * D), 0.02)
    inp["norm_mix"] = gain((DEPTH, D))
    inp["norm_ffn"] = gain((DEPTH, D))
    inp["norm_final"] = gain((D,))
    inp["mla_w_dq"] = nrm((N_MLA, D, MLA_Q_RANK), D ** -0.5)
    inp["mla_q_norm"] = gain((N_MLA, MLA_Q_RANK))
    inp["mla_w_uq"] = nrm((N_MLA, MLA_Q_RANK, MLA_HEADS * (MLA_D_NOPE + MLA_D_ROPE)), MLA_Q_RANK ** -0.5)
    inp["mla_w_dkv"] = nrm((N_MLA, D, MLA_KV_RANK + MLA_D_ROPE), D ** -0.5)
    inp["mla_kv_norm"] = gain((N_MLA, MLA_KV_RANK))
    inp["mla_w_ukv"] = nrm((N_MLA, MLA_KV_RANK, MLA_HEADS * (MLA_D_NOPE + MLA_D_V)), MLA_KV_RANK ** -0.5)
    inp["mla_w_o"] = nrm((N_MLA, MLA_HEADS * MLA_D_V, D), (MLA_HEADS * MLA_D_V) ** -0.5)
    sshape = (N_S5, 2, S5_GROUPS, S5_STATE)
    inp["s5_a_re"] = -0.5 + nrm(sshape, 0.01)
    inp["s5_a_im"] = jnp.broadcast_to(math.pi * jnp.arange(S5_STATE, dtype=jnp.float32), sshape) + nrm(sshape, 0.01)
    inp["s5_log_dt"] = jax.random.uniform(next(ks), (N_S5, 2, S5_GROUPS), jnp.float32,
                                          math.log(DT_MIN), math.log(DT_MAX))
    inp["s5_b_re"] = nrm(sshape + (S5_GROUP,), (2 * S5_GROUP) ** -0.5)
    inp["s5_b_im"] = nrm(sshape + (S5_GROUP,), (2 * S5_GROUP) ** -0.5)
    inp["s5_c_re"] = nrm((N_S5, 2, S5_GROUPS, S5_GROUP, S5_STATE), S5_STATE ** -0.5)
    inp["s5_c_im"] = nrm((N_S5, 2, S5_GROUPS, S5_GROUP, S5_STATE), S5_STATE ** -0.5)
    inp["s5_d"] = nrm((N_S5, D))
    inp["s5_w_glu"] = nrm((N_S5, D, 2 * D), D ** -0.5)
    inp["diff_w_qkv"] = nrm((N_DIFF, D, 3 * D), D ** -0.5)
    inp["diff_lq1"] = nrm((N_DIFF, DIFF_DH), 0.1)
    inp["diff_lk1"] = nrm((N_DIFF, DIFF_DH), 0.1)
    inp["diff_lq2"] = nrm((N_DIFF, DIFF_DH), 0.1)
    inp["diff_lk2"] = nrm((N_DIFF, DIFF_DH), 0.1)
    inp["diff_subln"] = gain((N_DIFF, 2 * DIFF_DH))
    inp["diff_w_o"] = nrm((N_DIFF, D, D), D ** -0.5)
    inp["ffn_w_in"] = nrm((N_DENSE, D, 2 * D_FF), D ** -0.5)
    inp["ffn_w_out"] = nrm((N_DENSE, D_FF, D), D_FF ** -0.5)
    inp["moe_w_router"] = nrm((N_MOE, D, N_EXPERTS), D ** -0.5)
    inp["moe_b_router"] = nrm((N_MOE, N_EXPERTS), 0.01)
    inp["moe_w_in"] = nrm((N_MOE, N_EXPERTS, D, 2 * D_FF), D ** -0.5)
    inp["moe_w_out"] = nrm((N_MOE, N_EXPERTS, D_FF, D), D_FF ** -0.5)
    return inp


def reference(x_prompt, x_sample, c, c_ctx,
              cache_mla_ckv, cache_mla_krope, state_s5_re, state_s5_im, cache_diff_k, cache_diff_v,
              ada_w, ada_b, norm_mix, norm_ffn, norm_final,
              mla_w_dq, mla_q_norm, mla_w_uq, mla_w_dkv, mla_kv_norm, mla_w_ukv, mla_w_o,
              s5_a_re, s5_a_im, s5_log_dt, s5_b_re, s5_b_im, s5_c_re, s5_c_im, s5_d, s5_w_glu,
              diff_w_qkv, diff_lq1, diff_lk1, diff_lq2, diff_lk2, diff_subln, diff_w_o,
              ffn_w_in, ffn_w_out, moe_w_router, moe_b_router, moe_w_in, moe_w_out):
    xp, xs = x_prompt, x_sample
    new_ckv, new_kr, new_s5_re, new_s5_im, new_dk, new_dv = [], [], [], [], [], []
    for i in range(DEPTH):
        mp = ada_params(c_ctx[None], ada_w[i], ada_b[i])
        ms = ada_params(c, ada_w[i], ada_b[i])
        hp = modulate(rmsnorm(xp, norm_mix[i]), mp[0], mp[1])
        hs = modulate(rmsnorm(xs, norm_mix[i]), ms[0], ms[1])
        j = i // N_MIXERS
        kind = i % N_MIXERS
        if kind == 0:
            w = (mla_w_dq[j], mla_q_norm[j], mla_w_uq[j], mla_w_dkv[j], mla_kv_norm[j], mla_w_ukv[j], mla_w_o[j])
            op, ckv, kr = mla_context(hp, *w)
            osm = mla_latent(hs, cache_mla_ckv[:, j], cache_mla_krope[:, j], *w)
            new_ckv.append(ckv)
            new_kr.append(kr)
        elif kind == 1:
            w = (s5_a_re[j], s5_a_im[j], s5_log_dt[j], s5_b_re[j], s5_b_im[j],
                 s5_c_re[j], s5_c_im[j], s5_d[j], s5_w_glu[j])
            op, st_re, st_im = s5_mixer(hp, None, None, *w, want_state=True)
            osm = s5_mixer(hs, state_s5_re[:, j], state_s5_im[:, j], *w, want_state=False)
            new_s5_re.append(st_re)
            new_s5_im.append(st_im)
        else:
            lam_init = 0.8 - 0.6 * math.exp(-0.3 * i)
            lam = diff_lambda(diff_lq1[j], diff_lk1[j], diff_lq2[j], diff_lk2[j], lam_init)
            w = (diff_w_qkv[j], lam, lam_init, diff_subln[j], diff_w_o[j])
            op, dk, dv = diff_context(hp, *w)
            osm = diff_latent(hs, cache_diff_k[:, j], cache_diff_v[:, j], *w)
            new_dk.append(dk)
            new_dv.append(dv)
        xp = xp + mp[2][:, None] * op
        xs = xs + ms[2][:, None] * osm
        hp = modulate(rmsnorm(xp, norm_ffn[i]), mp[3], mp[4])
        hs = modulate(rmsnorm(xs, norm_ffn[i]), ms[3], ms[4])
        f = i // 2
        if i % 2 == 0:
            fp = swiglu(hp, ffn_w_in[f], ffn_w_out[f])
            fs = swiglu(hs, ffn_w_in[f], ffn_w_out[f])
        else:
            fp = moe_swiglu(hp, moe_w_router[f], moe_b_router[f], moe_w_in[f], moe_w_out[f])
            fs = moe_swiglu(hs, moe_w_router[f], moe_b_router[f], moe_w_in[f], moe_w_out[f])
        xp = xp + mp[5][:, None] * fp
        xs = xs + ms[5][:, None] * fs
    y_prompt = rmsnorm(xp, norm_final)
    y_sample = rmsnorm(xs, norm_final)
    return (y_prompt, y_sample,
            jnp.stack(new_ckv, axis=1), jnp.stack(new_kr, axis=1),
            jnp.stack(new_s5_re, axis=1), jnp.stack(new_s5_im, axis=1),
            jnp.stack(new_dk, axis=1), jnp.stack(new_dv, axis=1))
```

```python
import functools
import math

import jax
import jax.numpy as jnp
from jax import lax
from jax.experimental import pallas as pl
from jax.experimental.pallas import tpu as pltpu

D = 1024
BATCH = 16
SEQ = 256
DEPTH = 4
DEC_BATCH = 2
DEC_SEQ = 4096
PAST = 512
GRID_W = 64
N_P = BATCH * SEQ
N_S = DEC_BATCH * DEC_SEQ
N_TOK = N_P + N_S
N_GROUPS = 1 + DEC_BATCH

MLA_HEADS = 8
MLA_Q_RANK = 384
MLA_KV_RANK = 256
MLA_D_NOPE = 128
MLA_D_ROPE = 64
MLA_D_V = 128
MLA_DK = MLA_D_NOPE + MLA_D_ROPE

S5_GROUP = 16
S5_GROUPS = D // S5_GROUP
S5_STATE = 64
S5_GB = 8
S5_NGB = S5_GROUPS // S5_GB
S5_HALF = S5_GB * S5_STATE
S5_NCH = S5_HALF // 128
S5_SUB = 8

DIFF_HEADS = 8
DIFF_DH = D // (2 * DIFF_HEADS)

D_FF = 2816
N_EXPERTS = 8
ROPE_THETA = 10000.0
EPS = 1e-6

TM = 512
FF_TILE = 1408
LANES = 128
VMEM_LIMIT = 56 * 1024 * 1024

F32 = jnp.float32
BF16 = jnp.bfloat16


def _cparams(*sem):
    return pltpu.CompilerParams(dimension_semantics=sem, vmem_limit_bytes=VMEM_LIMIT)


def _group_of_tile(i, tm):
    n_p = N_P // tm
    per = DEC_SEQ // tm
    return jnp.where(i < n_p, 0, 1 + (i - n_p) // per)


def _rope_tile(i, tm):
    n_p = N_P // tm
    per = DEC_SEQ // tm
    return jnp.where(i < n_p, 0, 1 + (i - n_p) % per)


def _rms(x, g):
    return x * lax.rsqrt(jnp.mean(x * x, axis=-1, keepdims=True) + EPS) * g


def _normmod(x, g, mod, k_shift, k_scale):
    return _rms(x, g) * (1.0 + mod[k_scale:k_scale + 1, :]) + mod[k_shift:k_shift + 1, :]


def _sigmoid(x):
    return 1.0 / (1.0 + jnp.exp(-x))


def _dot(a, b):
    return jnp.dot(a, b, preferred_element_type=F32)


def _dot_nt(a, b):
    return lax.dot_general(a, b, (((1,), (1,)), ((), ())), preferred_element_type=F32)


def _rope(x, cos, sin):
    lane = lax.broadcasted_iota(jnp.int32, x.shape, 1)
    nxt = pltpu.roll(x, LANES - 16, 1)
    prv = pltpu.roll(x, 16, 1)
    swapped = jnp.where((lane % 32) < 16, nxt, prv)
    return x * cos + swapped * sin


def _ada_kernel(c_ref, w_ref, b_ref, o_ref):
    c = c_ref[...]
    s = (c * _sigmoid(c)).astype(BF16)
    o_ref[...] = _dot(s, w_ref[...].astype(BF16)) + b_ref[...]


def ada_all(cond8, ada_w, ada_b):
    tn = 1536
    return pl.pallas_call(
        _ada_kernel,
        grid=(DEPTH, 6 * D // tn),
        in_specs=[pl.BlockSpec((8, D), lambda l, n: (0, 0)),
                  pl.BlockSpec((None, D, tn), lambda l, n: (l, 0, n)),
                  pl.BlockSpec((None, 1, tn), lambda l, n: (l, 0, n))],
        out_specs=pl.BlockSpec((None, 8, tn), lambda l, n: (l, 0, n)),
        out_shape=jax.ShapeDtypeStruct((DEPTH, 8, 6 * D), F32),
        compiler_params=_cparams("parallel", "parallel"),
        name="ada",
    )(cond8, ada_w, ada_b.reshape(DEPTH, 1, 6 * D))


def _mod_spec(tm):
    return pl.BlockSpec((None, 6, D), lambda i, *_: (_group_of_tile(i, tm), 0, 0))


def _mla_tok_kernel(x_ref, g_ref, mod_ref, w1_ref, qn_ref, wuq_ref, kvn_ref, cos_ref, sin_ref,
                    q_ref, ckv_ref, kr_ref):
    h = _normmod(x_ref[...], g_ref[...], mod_ref[...], 0, 1).astype(BF16)
    t1 = _dot(h, w1_ref[...])
    ql = _rms(t1[:, :MLA_Q_RANK], qn_ref[...]).astype(BF16)
    q = _dot(ql, wuq_ref[...])
    c0 = MLA_Q_RANK
    ckv_ref[...] = _rms(t1[:, c0:c0 + MLA_KV_RANK], kvn_ref[...])
    cos = cos_ref[...]
    sin = sin_ref[...]
    kr = _rope(t1[:, c0 + MLA_KV_RANK:c0 + MLA_KV_RANK + LANES], cos, sin)
    kr_ref[...] = kr[:, :MLA_D_ROPE]
    n_nope = MLA_HEADS * MLA_D_NOPE
    for pair in range(MLA_HEADS // 2):
        qr = _rope(q[:, n_nope + pair * LANES:n_nope + (pair + 1) * LANES], cos, sin).astype(BF16)
        for sub in range(2):
            hd = 2 * pair + sub
            q_ref[hd, :, 0:MLA_D_NOPE] = q[:, hd * MLA_D_NOPE:(hd + 1) * MLA_D_NOPE].astype(BF16)
            q_ref[hd, :, MLA_D_NOPE:MLA_DK] = qr[:, sub * MLA_D_ROPE:(sub + 1) * MLA_D_ROPE]


def mla_tokens(x, g, mod, w1, qn, wuq, kvn, cos_t, sin_t):
    nt = N_TOK // TM
    const = lambda shape: pl.BlockSpec(shape, lambda i: (0,) * len(shape))
    return pl.pallas_call(
        _mla_tok_kernel,
        grid=(nt,),
        in_specs=[pl.BlockSpec((TM, D), lambda i: (i, 0)), const((1, D)), _mod_spec(TM),
                  const(w1.shape), const((1, MLA_Q_RANK)), const(wuq.shape), const((1, MLA_KV_RANK)),
                  pl.BlockSpec((TM, LANES), lambda i: (_rope_tile(i, TM), 0)),
                  pl.BlockSpec((TM, LANES), lambda i: (_rope_tile(i, TM), 0))],
        out_specs=[pl.BlockSpec((MLA_HEADS, TM, MLA_DK), lambda i: (0, i, 0)),
                   pl.BlockSpec((TM, MLA_KV_RANK), lambda i: (i, 0)),
                   pl.BlockSpec((TM, MLA_D_ROPE), lambda i: (i, 0))],
        out_shape=[jax.ShapeDtypeStruct((MLA_HEADS, N_TOK, MLA_DK), BF16),
                   jax.ShapeDtypeStruct((N_TOK, MLA_KV_RANK), F32),
                   jax.ShapeDtypeStruct((N_TOK, MLA_D_ROPE), F32)],
        compiler_params=_cparams("parallel"),
        name="mla_tokens",
    )(x, g, mod, w1, qn, wuq, kvn, cos_t, sin_t)


def _mla_kv_kernel(ckv_ref, kr_ref, w_ref, k_ref, v_ref):
    kv = _dot(ckv_ref[...].astype(BF16), w_ref[...])
    kr = kr_ref[...].astype(BF16)
    n_nope = MLA_HEADS * MLA_D_NOPE
    for hd in range(MLA_HEADS):
        k_ref[hd, :, 0:MLA_D_NOPE] = kv[:, hd * MLA_D_NOPE:(hd + 1) * MLA_D_NOPE].astype(BF16)
        k_ref[hd, :, MLA_D_NOPE:MLA_DK] = kr
        v_ref[hd] = kv[:, n_nope + hd * MLA_D_V:n_nope + (hd + 1) * MLA_D_V].astype(BF16)


def mla_kv(ckv, kr, wukv, ts):
    nb, s, _ = ckv.shape
    return pl.pallas_call(
        _mla_kv_kernel,
        grid=(nb, s // ts),
        in_specs=[pl.BlockSpec((None, ts, MLA_KV_RANK), lambda b, t: (b, t, 0)),
                  pl.BlockSpec((None, ts, MLA_D_ROPE), lambda b, t: (b, t, 0)),
                  pl.BlockSpec(wukv.shape, lambda b, t: (0, 0))],
        out_specs=[pl.BlockSpec((None, MLA_HEADS, ts, MLA_DK), lambda b, t: (b, 0, t, 0)),
                   pl.BlockSpec((None, MLA_HEADS, ts, MLA_D_V), lambda b, t: (b, 0, t, 0))],
        out_shape=[jax.ShapeDtypeStruct((nb, MLA_HEADS, s, MLA_DK), BF16),
                   jax.ShapeDtypeStruct((nb, MLA_HEADS, s, MLA_D_V), BF16)],
        compiler_params=_cparams("parallel", "parallel"),
        name="mla_kv",
    )(ckv, kr, wukv)


def _mla_attn_kernel(q_ref, k_ref, v_ref, o_ref):
    s = _dot_nt(q_ref[...], k_ref[...]) * (MLA_DK ** -0.5)
    m = jnp.max(s, axis=-1, keepdims=True)
    p = jnp.exp(s - m)
    l = jnp.sum(p, axis=-1, keepdims=True)
    o_ref[...] = (_dot(p.astype(BF16), v_ref[...]) / l).astype(BF16)


def mla_attention(q3, k3, v3, row0, seq, tq):
    nb, _, s, _ = k3.shape
    nq = seq // tq
    base = row0 // tq
    return pl.pallas_call(
        _mla_attn_kernel,
        grid=(nb, MLA_HEADS, nq),
        in_specs=[pl.BlockSpec((None, tq, MLA_DK), lambda b, h, i: (h, base + b * nq + i, 0)),
                  pl.BlockSpec((None, None, s, MLA_DK), lambda b, h, i: (b, h, 0, 0)),
                  pl.BlockSpec((None, None, s, MLA_D_V), lambda b, h, i: (b, h, 0, 0))],
        out_specs=pl.BlockSpec((tq, MLA_D_V), lambda b, h, i: (b * nq + i, h)),
        out_shape=jax.ShapeDtypeStruct((nb * seq, MLA_HEADS * MLA_D_V), BF16),
        compiler_params=_cparams("parallel", "parallel", "parallel"),
        name="mla_attn_%d" % s,
    )(q3, k3, v3)


def _proj_res_kernel(o_ref, w_ref, x_ref, mod_ref, out_ref):
    out_ref[...] = x_ref[...] + mod_ref[2:3, :] * _dot(o_ref[...], w_ref[...])


def proj_residual(o, w, x, mod):
    nt = N_TOK // TM
    return pl.pallas_call(
        _proj_res_kernel,
        grid=(nt,),
        in_specs=[pl.BlockSpec((TM, D), lambda i: (i, 0)), pl.BlockSpec((D, D), lambda i: (0, 0)),
                  pl.BlockSpec((TM, D), lambda i: (i, 0)), _mod_spec(TM)],
        out_specs=pl.BlockSpec((TM, D), lambda i: (i, 0)),
        out_shape=jax.ShapeDtypeStruct((N_TOK, D), F32),
        compiler_params=_cparams("parallel"),
        name="proj_residual",
    )(o, w, x, mod)


def _diff_tok_kernel(x_ref, g_ref, mod_ref, w_ref, cos_ref, sin_ref, q_ref, k_ref, v_ref, kc_ref):
    h = _normmod(x_ref[...], g_ref[...], mod_ref[...], 0, 1).astype(BF16)
    cos = cos_ref[...]
    sin = sin_ref[...]
    for c in range(D // LANES):
        sl = slice(c * LANES, (c + 1) * LANES)
        q = _dot(h, w_ref[:, c * LANES:(c + 1) * LANES])
        k = _dot(h, w_ref[:, D + c * LANES:D + (c + 1) * LANES])
        v = _dot(h, w_ref[:, 2 * D + c * LANES:2 * D + (c + 1) * LANES])
        q_ref[:, sl] = _rope(q, cos, sin).astype(BF16)
        k_ref[:, sl] = k
        kc_ref[:, sl] = _rope(k, cos, sin).astype(BF16)
        v_ref[:, sl] = v


def diff_tokens(x, g, mod, wqkv, cos_t, sin_t):
    nt = N_TOK // TM
    row = lambda i: (i, 0)
    return pl.pallas_call(
        _diff_tok_kernel,
        grid=(nt,),
        in_specs=[pl.BlockSpec((TM, D), row), pl.BlockSpec((1, D), lambda i: (0, 0)), _mod_spec(TM),
                  pl.BlockSpec((D, 3 * D), lambda i: (0, 0)),
                  pl.BlockSpec((TM, LANES), lambda i: (_rope_tile(i, TM), 0)),
                  pl.BlockSpec((TM, LANES), lambda i: (_rope_tile(i, TM), 0))],
        out_specs=[pl.BlockSpec((TM, D), row)] * 4,
        out_shape=[jax.ShapeDtypeStruct((N_TOK, D), BF16), jax.ShapeDtypeStruct((N_TOK, D), F32),
                   jax.ShapeDtypeStruct((N_TOK, D), F32), jax.ShapeDtypeStruct((N_TOK, D), BF16)],
        compiler_params=_cparams("parallel"),
        name="diff_tokens",
    )(x, g, mod, wqkv, cos_t, sin_t)


def _diff_attn_kernel(lam_init, lq1_ref, lk1_ref, lq2_ref, lk2_ref, sub_ref, q_ref, k_ref, v_ref, o_ref):
    lam = (jnp.exp(jnp.sum(lq1_ref[...] * lk1_ref[...], axis=-1, keepdims=True))
           - jnp.exp(jnp.sum(lq2_ref[...] * lk2_ref[...], axis=-1, keepdims=True)) + lam_init)
    q = q_ref[...]
    k = k_ref[...]
    lane = lax.broadcasted_iota(jnp.int32, q.shape, 1)
    zero = jnp.zeros_like(q)
    scale = DIFF_DH ** -0.5

    def probs(qh):
        s = _dot_nt(qh, k) * scale
        p = jnp.exp(s - jnp.max(s, axis=-1, keepdims=True))
        return p / jnp.sum(p, axis=-1, keepdims=True)

    p1 = probs(jnp.where(lane < DIFF_DH, q, zero))
    p2 = probs(jnp.where(lane >= DIFF_DH, q, zero))
    att = (p1 - lam * p2).astype(BF16)
    o = _dot(att, v_ref[...])
    o_ref[...] = (_rms(o, sub_ref[...]) * (1.0 - lam_init)).astype(BF16)


def diff_attention(lam_init, lvecs, subln, q, k, v, row0, seq, tq):
    nb, s, _ = k.shape
    nq = seq // tq
    base = row0 // tq
    vec = pl.BlockSpec((1, DIFF_DH), lambda b, h, i: (0, 0))
    return pl.pallas_call(
        functools.partial(_diff_attn_kernel, lam_init),
        grid=(nb, DIFF_HEADS, nq),
        in_specs=[vec, vec, vec, vec, pl.BlockSpec((1, 2 * DIFF_DH), lambda b, h, i: (0, 0)),
                  pl.BlockSpec((tq, LANES), lambda b, h, i: (base + b * nq + i, h)),
                  pl.BlockSpec((None, s, LANES), lambda b, h, i: (b, 0, h)),
                  pl.BlockSpec((None, s, LANES), lambda b, h, i: (b, 0, h))],
        out_specs=pl.BlockSpec((tq, LANES), lambda b, h, i: (b * nq + i, h)),
        out_shape=jax.ShapeDtypeStruct((nb * seq, D), BF16),
        compiler_params=_cparams("parallel", "parallel", "parallel"),
        name="diff_attn_%d" % s,
    )(*lvecs, subln, q, k, v)


def _normmod_kernel(x_ref, g_ref, mod_ref, h_ref):
    h_ref[...] = _normmod(x_ref[...], g_ref[...], mod_ref[...], 0, 1)


def normmod_tokens(x, g, mod):
    nt = N_TOK // TM
    return pl.pallas_call(
        _normmod_kernel,
        grid=(nt,),
        in_specs=[pl.BlockSpec((TM, D), lambda i: (i, 0)), pl.BlockSpec((1, D), lambda i: (0, 0)), _mod_spec(TM)],
        out_specs=pl.BlockSpec((TM, D), lambda i: (i, 0)),
        out_shape=jax.ShapeDtypeStruct((N_TOK, D), F32),
        compiler_params=_cparams("parallel"),
        name="normmod",
    )(x, g, mod)


def _s5_kernel(chained, n, u_ref, wb_ref, wc_ref, a_ref, an_ref, dsk_ref, h0_ref, y_ref, fin_ref,
               bu_ref, ini_ref):
    d = pl.program_id(2)
    rows = S5_SUB * n
    chunk = 512
    nch = S5_NCH
    for r in range(rows // chunk):
        rs = slice(r * chunk, (r + 1) * chunk)
        bu = _dot(u_ref[rs, :].astype(BF16), wb_ref[...])
        for c in range(2 * nch):
            bu_ref[c, rs, :] = bu[:, c * LANES:(c + 1) * LANES]
    ar = [jnp.broadcast_to(a_ref[0:1, c * LANES:(c + 1) * LANES], (S5_SUB, LANES)) for c in range(nch)]
    ai = [jnp.broadcast_to(a_ref[1:2, c * LANES:(c + 1) * LANES], (S5_SUB, LANES)) for c in range(nch)]

    def step_index(s):
        return jnp.where(d == 0, s, n - 1 - s)

    def advance(h, t):
        out = [None] * (2 * nch)
        for c in range(nch):
            br = bu_ref[c, pl.ds(t, S5_SUB, stride=n), :]
            bi = bu_ref[nch + c, pl.ds(t, S5_SUB, stride=n), :]
            out[c] = ar[c] * h[c] - ai[c] * h[nch + c] + br
            out[nch + c] = ar[c] * h[nch + c] + ai[c] * h[c] + bi
        return out

    unroll = 4
    zeros = [jnp.zeros((S5_SUB, LANES), F32) for _ in range(2 * nch)]

    if chained:
        def local_body(s, h):
            h = list(h)
            for k in range(unroll):
                h = advance(h, step_index(s * unroll + k))
            return tuple(h)

        ends = lax.fori_loop(0, n // unroll, local_body, tuple(zeros))
        sub_row = lax.broadcasted_iota(jnp.int32, (S5_SUB, LANES), 0)
        cur = [h0_ref[:, c * LANES:(c + 1) * LANES] for c in range(2 * nch)]
        anr = [an_ref[0:1, c * LANES:(c + 1) * LANES] for c in range(nch)]
        ani = [an_ref[1:2, c * LANES:(c + 1) * LANES] for c in range(nch)]
        for kk in range(S5_SUB):
            j = jnp.where(d == 0, kk, S5_SUB - 1 - kk)
            nxt = [None] * (2 * nch)
            for c in range(2 * nch):
                ini_ref[c, pl.ds(j, 1), :] = cur[c]
            for c in range(nch):
                er = jnp.sum(jnp.where(sub_row == j, ends[c], 0.0), axis=0, keepdims=True)
                ei = jnp.sum(jnp.where(sub_row == j, ends[nch + c], 0.0), axis=0, keepdims=True)
                nxt[c] = er + anr[c] * cur[c] - ani[c] * cur[nch + c]
                nxt[nch + c] = ei + anr[c] * cur[nch + c] + ani[c] * cur[c]
            cur = nxt
        h_init = [ini_ref[c] for c in range(2 * nch)]
    else:
        h_init = zeros

    def body(s, h):
        h = list(h)
        for k in range(unroll):
            t = step_index(s * unroll + k)
            h = advance(h, t)
            for c in range(2 * nch):
                bu_ref[c, pl.ds(t, S5_SUB, stride=n), :] = h[c]
        return tuple(h)

    fin = lax.fori_loop(0, n // unroll, body, tuple(h_init))
    for c in range(2 * nch):
        fin_ref[:, c * LANES:(c + 1) * LANES] = fin[c]

    @pl.when(d == 0)
    def _():
        y_ref[...] = dsk_ref[...] * u_ref[...]

    for r in range(rows // chunk):
        rs = slice(r * chunk, (r + 1) * chunk)
        hs = jnp.concatenate([bu_ref[c, rs, :].astype(BF16) for c in range(2 * nch)], axis=1)
        y_ref[rs, :] += _dot(hs, wc_ref[...])


def s5_scan(h, wb, wc, a, an, dskip, h0, row0, n_blocks, n, chained):
    rows = S5_SUB * n
    base = row0 // rows
    kern = functools.partial(_s5_kernel, chained, n)
    return pl.pallas_call(
        kern,
        grid=(n_blocks, S5_NGB, 2),
        in_specs=[pl.BlockSpec((rows, LANES), lambda r, c, d: (base + r, c)),
                  pl.BlockSpec((None, None, LANES, 2 * S5_HALF), lambda r, c, d: (d, c, 0, 0)),
                  pl.BlockSpec((None, None, 2 * S5_HALF, LANES), lambda r, c, d: (d, c, 0, 0)),
                  pl.BlockSpec((None, None, 2, S5_HALF), lambda r, c, d: (d, c, 0, 0)),
                  pl.BlockSpec((None, None, 2, S5_HALF), lambda r, c, d: (d, c, 0, 0)),
                  pl.BlockSpec((1, LANES), lambda r, c, d: (0, c)),
                  pl.BlockSpec((None, None, None, 1, 2 * S5_HALF), lambda r, c, d: (r, d, c, 0, 0))],
        out_specs=[pl.BlockSpec((rows, LANES), lambda r, c, d: (r, c)),
                   pl.BlockSpec((None, None, None, S5_SUB, 2 * S5_HALF), lambda r, c, d: (r, d, c, 0, 0))],
        out_shape=[jax.ShapeDtypeStruct((n_blocks * rows, D), F32),
                   jax.ShapeDtypeStruct((n_blocks, 2, S5_NGB, S5_SUB, 2 * S5_HALF), F32)],
        scratch_shapes=[pltpu.VMEM((2 * S5_NCH, rows, LANES), F32),
                        pltpu.VMEM((2 * S5_NCH, S5_SUB, LANES), F32)],
        compiler_params=_cparams("parallel", "parallel", "arbitrary"),
        name="s5_scan_%d" % n,
    )(h, wb, wc, a, an, dskip, h0)


def _glu_res_kernel(y_ref, w_ref, x_ref, mod_ref, out_ref):
    y = y_ref[...]
    g = 0.5 * y * (1.0 + jnp.tanh(math.sqrt(2.0 / math.pi) * (y + 0.044715 * (y * y * y))))
    t = _dot(g.astype(BF16), w_ref[...])
    out_ref[...] = x_ref[...] + mod_ref[2:3, :] * (t[:, :D] * _sigmoid(t[:, D:]))


def glu_residual(y, w, x, mod):
    nt = N_TOK // TM
    return pl.pallas_call(
        _glu_res_kernel,
        grid=(nt,),
        in_specs=[pl.BlockSpec((TM, D), lambda i: (i, 0)), pl.BlockSpec((D, 2 * D), lambda i: (0, 0)),
                  pl.BlockSpec((TM, D), lambda i: (i, 0)), _mod_spec(TM)],
        out_specs=pl.BlockSpec((TM, D), lambda i: (i, 0)),
        out_shape=jax.ShapeDtypeStruct((N_TOK, D), F32),
        compiler_params=_cparams("parallel"),
        name="glu_residual",
    )(y, w, x, mod)


def _ffn_kernel(x_ref, g_ref, mod_ref, wa_ref, wb_ref, wo_ref, out_ref, h_scr, acc_scr):
    f = pl.program_id(1)

    @pl.when(f == 0)
    def _():
        h_scr[...] = _normmod(x_ref[...], g_ref[...], mod_ref[...], 3, 4).astype(BF16)
        acc_scr[...] = jnp.zeros_like(acc_scr)

    h = h_scr[...]
    a = _dot(h, wa_ref[...])
    b = _dot(h, wb_ref[...])
    act = (a * _sigmoid(a) * b).astype(BF16)
    acc_scr[...] += _dot(act, wo_ref[...])

    @pl.when(f == pl.num_programs(1) - 1)
    def _():
        out_ref[...] = x_ref[...] + mod_ref[5:6, :] * acc_scr[...]


def ffn_residual(x, g, mod, w_in, w_out):
    nt = N_TOK // TM
    nf = D_FF // FF_TILE
    return pl.pallas_call(
        _ffn_kernel,
        grid=(nt, nf),
        in_specs=[pl.BlockSpec((TM, D), lambda i, f: (i, 0)), pl.BlockSpec((1, D), lambda i, f: (0, 0)),
                  _mod_spec(TM),
                  pl.BlockSpec((D, FF_TILE), lambda i, f: (0, f)),
                  pl.BlockSpec((D, FF_TILE), lambda i, f: (0, f + nf)),
                  pl.BlockSpec((FF_TILE, D), lambda i, f: (f, 0))],
        out_specs=pl.BlockSpec((TM, D), lambda i, f: (i, 0)),
        out_shape=jax.ShapeDtypeStruct((N_TOK, D), F32),
        scratch_shapes=[pltpu.VMEM((TM, D), BF16), pltpu.VMEM((TM, D), F32)],
        compiler_params=_cparams("parallel", "arbitrary"),
        name="ffn",
    )(x, g, mod, w_in, w_in, w_out)


def _moe_kernel(x_ref, g_ref, mod_ref, wr_ref, br_ref, wa_ref, wb_ref, wo_ref, out_ref,
                h_scr, acc_scr, comb_scr):
    e = pl.program_id(1)
    f = pl.program_id(2)

    @pl.when((e == 0) & (f == 0))
    def _():
        h = _normmod(x_ref[...], g_ref[...], mod_ref[...], 3, 4).astype(BF16)
        h_scr[...] = h
        acc_scr[...] = jnp.zeros_like(acc_scr)
        logits = _dot(h, wr_ref[...]) + br_ref[...]
        lane = lax.broadcasted_iota(jnp.int32, logits.shape, 1)
        neg = jnp.float32(-jnp.inf)
        lg = jnp.where(lane < N_EXPERTS, logits, neg)
        v1 = jnp.max(lg, axis=-1, keepdims=True)
        i1 = jnp.min(jnp.where(lg == v1, lane, LANES), axis=-1, keepdims=True)
        lg2 = jnp.where(lane == i1, neg, lg)
        v2 = jnp.max(lg2, axis=-1, keepdims=True)
        i2 = jnp.min(jnp.where(lg2 == v2, lane, LANES), axis=-1, keepdims=True)
        e2 = jnp.exp(v2 - v1)
        g1 = 1.0 / (1.0 + e2)
        g2 = e2 / (1.0 + e2)
        comb_scr[...] = jnp.where(lane == i1, g1, 0.0) + jnp.where(lane == i2, g2, 0.0)

    h = h_scr[...]
    a = _dot(h, wa_ref[...])
    b = _dot(h, wb_ref[...])
    act = (a * _sigmoid(a) * b).astype(BF16)
    comb = comb_scr[...]
    lane = lax.broadcasted_iota(jnp.int32, comb.shape, 1)
    ce = jnp.sum(jnp.where(lane == e, comb, 0.0), axis=-1, keepdims=True)
    acc_scr[...] += ce * _dot(act, wo_ref[...])

    @pl.when((e == pl.num_programs(1) - 1) & (f == pl.num_programs(2) - 1))
    def _():
        out_ref[...] = x_ref[...] + mod_ref[5:6, :] * acc_scr[...]


def moe_residual(x, g, mod, w_router, b_router, w_in, w_out):
    nt = N_TOK // TM
    nf = D_FF // FF_TILE
    return pl.pallas_call(
        _moe_kernel,
        grid=(nt, N_EXPERTS, nf),
        in_specs=[pl.BlockSpec((TM, D), lambda i, e, f: (i, 0)), pl.BlockSpec((1, D), lambda i, e, f: (0, 0)),
                  _mod_spec(TM),
                  pl.BlockSpec((D, LANES), lambda i, e, f: (0, 0)),
                  pl.BlockSpec((1, LANES), lambda i, e, f: (0, 0)),
                  pl.BlockSpec((None, D, FF_TILE), lambda i, e, f: (e, 0, f)),
                  pl.BlockSpec((None, D, FF_TILE), lambda i, e, f: (e, 0, f + nf)),
                  pl.BlockSpec((None, FF_TILE, D), lambda i, e, f: (e, f, 0))],
        out_specs=pl.BlockSpec((TM, D), lambda i, e, f: (i, 0)),
        out_shape=jax.ShapeDtypeStruct((N_TOK, D), F32),
        scratch_shapes=[pltpu.VMEM((TM, D), BF16), pltpu.VMEM((TM, D), F32), pltpu.VMEM((TM, LANES), F32)],
        compiler_params=_cparams("parallel", "arbitrary", "arbitrary"),
        name="moe",
    )(x, g, mod, w_router, b_router, w_in, w_in, w_out)


def _final_norm_kernel(x_ref, g_ref, o_ref):
    o_ref[...] = _rms(x_ref[...], g_ref[...])


def final_norm(x, g):
    nt = N_TOK // TM
    return pl.pallas_call(
        _final_norm_kernel,
        grid=(nt,),
        in_specs=[pl.BlockSpec((TM, D), lambda i: (i, 0)), pl.BlockSpec((1, D), lambda i: (0, 0))],
        out_specs=pl.BlockSpec((TM, D), lambda i: (i, 0)),
        out_shape=jax.ShapeDtypeStruct((N_TOK, D), F32),
        compiler_params=_cparams("parallel"),
        name="final_norm",
    )(x, g)


def _rope_tables():
    half = 16
    freqs = ROPE_THETA ** (-jnp.arange(half, dtype=F32) / half)
    t = jnp.arange(DEC_SEQ, dtype=jnp.int32)
    row = (t // GRID_W).astype(F32)[:, None] * freqs[None, :]
    col = (t % GRID_W).astype(F32)[:, None] * freqs[None, :]
    cos = jnp.concatenate([jnp.cos(row), jnp.cos(row), jnp.cos(col), jnp.cos(col)], axis=1)
    sin = jnp.concatenate([-jnp.sin(row), jnp.sin(row), -jnp.sin(col), jnp.sin(col)], axis=1)
    cos = jnp.concatenate([jnp.ones((TM, 64), F32), cos], axis=0)
    sin = jnp.concatenate([jnp.zeros((TM, 64), F32), sin], axis=0)
    return jnp.tile(cos, (1, 2)), jnp.tile(sin, (1, 2))


def _mla_weights(w_dq, w_uq, w_dkv, w_ukv, w_o):
    w1 = jnp.concatenate([w_dq, w_dkv, jnp.zeros((D, LANES - MLA_D_ROPE), F32)], axis=1).astype(BF16)
    uq = w_uq.reshape(MLA_Q_RANK, MLA_HEADS, MLA_DK)
    wuq = jnp.concatenate([uq[:, :, :MLA_D_NOPE].reshape(MLA_Q_RANK, -1),
                           uq[:, :, MLA_D_NOPE:].reshape(MLA_Q_RANK, -1)], axis=1).astype(BF16)
    ukv = w_ukv.reshape(MLA_KV_RANK, MLA_HEADS, MLA_D_NOPE + MLA_D_V)
    wukv = jnp.concatenate([ukv[:, :, :MLA_D_NOPE].reshape(MLA_KV_RANK, -1),
                            ukv[:, :, MLA_D_NOPE:].reshape(MLA_KV_RANK, -1)], axis=1).astype(BF16)
    return w1, wuq, wukv, w_o.astype(BF16)


def _s5_weights(a_re, a_im, log_dt, b_re, b_im, c_re, c_im, seg_len):
    lam = lax.complex(a_re, a_im)
    dt = jnp.exp(log_dt)[..., None]
    abar = jnp.exp(lam * dt)
    apow = jnp.exp(lam * (dt * seg_len))
    bbar = ((abar - 1) / lam)[..., None] * lax.complex(b_re, b_im)
    eye = jnp.eye(S5_GB, dtype=F32)

    def in_block(m):
        m = m.reshape(2, S5_NGB, S5_GB, S5_STATE, S5_GROUP)
        return jnp.einsum('dbgpc,gh->dbgchp', m, eye).reshape(2, S5_NGB, LANES, S5_HALF)

    def out_block(m):
        m = m.reshape(2, S5_NGB, S5_GB, S5_GROUP, S5_STATE)
        return jnp.einsum('dbgcp,gh->dbgphc', m, eye).reshape(2, S5_NGB, S5_HALF, LANES)

    wb = jnp.concatenate([in_block(bbar.real), in_block(bbar.imag)], axis=3).astype(BF16)
    wc = jnp.concatenate([out_block(c_re), out_block(-c_im)], axis=2).astype(BF16)
    lanes = lambda m: m.reshape(2, S5_NGB, 1, S5_HALF)
    a = jnp.concatenate([lanes(abar.real), lanes(abar.imag)], axis=2)
    an = jnp.concatenate([lanes(apow.real), lanes(apow.imag)], axis=2)
    return wb, wc, a, an


def kernel(x_prompt, x_sample, c, c_ctx, cache_mla_ckv, cache_mla_krope, state_s5_re, state_s5_im, cache_diff_k, cache_diff_v, ada_w, ada_b, norm_mix, norm_ffn, norm_final, mla_w_dq, mla_q_norm, mla_w_uq, mla_w_dkv, mla_kv_norm, mla_w_ukv, mla_w_o, s5_a_re, s5_a_im, s5_log_dt, s5_b_re, s5_b_im, s5_c_re, s5_c_im, s5_d, s5_w_glu, diff_w_qkv, diff_lq1, diff_lk1, diff_lq2, diff_lk2, diff_subln, diff_w_o, ffn_w_in, ffn_w_out, moe_w_router, moe_b_router, moe_w_in, moe_w_out):
    x = jnp.concatenate([x_prompt.reshape(N_P, D), x_sample.reshape(N_S, D)], axis=0)
    cond8 = jnp.concatenate([c_ctx[None], c, jnp.zeros((8 - N_GROUPS, D), F32)], axis=0)
    mods = ada_all(cond8, ada_w, ada_b).reshape(DEPTH, 8, 6, D)[:, :N_GROUPS]
    cos_t, sin_t = _rope_tables()

    new_ckv, new_kr, new_s5_re, new_s5_im, new_dk, new_dv = [], [], [], [], [], []
    for i in range(DEPTH):
        mod = mods[i]
        gmix = norm_mix[i].reshape(1, D)
        gffn = norm_ffn[i].reshape(1, D)
        j = i // 3
        kind = i % 3
        if kind == 0:
            w1, wuq, wukv, wo = _mla_weights(mla_w_dq[j], mla_w_uq[j], mla_w_dkv[j], mla_w_ukv[j], mla_w_o[j])
            q3, ckv, kr = mla_tokens(x, gmix, mod, w1, mla_q_norm[j].reshape(1, -1), wuq,
                                     mla_kv_norm[j].reshape(1, -1), cos_t, sin_t)
            ckv_p = ckv[:N_P].reshape(BATCH, SEQ, MLA_KV_RANK)
            kr_p = kr[:N_P].reshape(BATCH, SEQ, MLA_D_ROPE)
            new_ckv.append(ckv_p)
            new_kr.append(kr_p)
            ckv_s = jnp.concatenate([cache_mla_ckv[:, j], ckv[N_P:].reshape(DEC_BATCH, DEC_SEQ, -1)], axis=1)
            kr_s = jnp.concatenate([cache_mla_krope[:, j], kr[N_P:].reshape(DEC_BATCH, DEC_SEQ, -1)], axis=1)
            k3p, v3p = mla_kv(ckv_p, kr_p, wukv, SEQ)
            k3s, v3s = mla_kv(ckv_s, kr_s, wukv, 512)
            o_p = mla_attention(q3, k3p, v3p, 0, SEQ, SEQ)
            o_s = mla_attention(q3, k3s, v3s, N_P, DEC_SEQ, 512)
            x = proj_residual(jnp.concatenate([o_p, o_s], axis=0), wo, x, mod)
        elif kind == 1:
            h = normmod_tokens(x, gmix, mod)
            dsk = s5_d[j].reshape(1, D)
            seg = DEC_SEQ // S5_SUB
            wb, wc, a, an = _s5_weights(s5_a_re[j], s5_a_im[j], s5_log_dt[j], s5_b_re[j], s5_b_im[j],
                                        s5_c_re[j], s5_c_im[j], seg)
            zero_h0 = jnp.zeros((BATCH // S5_SUB, 2, S5_NGB, 1, 2 * S5_HALF), F32)
            y_p, fin = s5_scan(h, wb, wc, a, an, dsk, zero_h0, 0, BATCH // S5_SUB, SEQ, False)
            fin = fin.reshape(BATCH // S5_SUB, 2, S5_NGB, S5_SUB, 2, S5_GB, S5_STATE)
            fin = jnp.transpose(fin, (0, 3, 1, 4, 2, 5, 6)).reshape(BATCH, 2, 2, S5_GROUPS, S5_STATE)
            new_s5_re.append(fin[:, :, 0])
            new_s5_im.append(fin[:, :, 1])
            h0 = jnp.stack([state_s5_re[:, j], state_s5_im[:, j]], axis=2)
            h0 = h0.reshape(DEC_BATCH, 2, 2, S5_NGB, S5_HALF)
            h0 = jnp.transpose(h0, (0, 1, 3, 2, 4)).reshape(DEC_BATCH, 2, S5_NGB, 1, 2 * S5_HALF)
            y_s, _ = s5_scan(h, wb, wc, a, an, dsk, h0, N_P, DEC_BATCH, seg, True)
            x = glu_residual(jnp.concatenate([y_p, y_s], axis=0), s5_w_glu[j].astype(BF16), x, mod)
        else:
            lam_init = 0.8 - 0.6 * math.exp(-0.3 * i)
            q, k, v, kc = diff_tokens(x, gmix, mod, diff_w_qkv[j].astype(BF16), cos_t, sin_t)
            new_dk.append(k[:N_P].reshape(BATCH, SEQ, 2 * DIFF_HEADS, DIFF_DH))
            new_dv.append(v[:N_P].reshape(BATCH, SEQ, DIFF_HEADS, 2 * DIFF_DH))
            lvecs = [a_.reshape(1, DIFF_DH) for a_ in (diff_lq1[j], diff_lk1[j], diff_lq2[j], diff_lk2[j])]
            subln = diff_subln[j].reshape(1, 2 * DIFF_DH)
            vb = v.astype(BF16)
            k_p = kc[:N_P].reshape(BATCH, SEQ, D)
            v_p = vb[:N_P].reshape(BATCH, SEQ, D)
            k_s = jnp.concatenate([cache_diff_k[:, j].reshape(DEC_BATCH, PAST, D).astype(BF16),
                                   kc[N_P:].reshape(DEC_BATCH, DEC_SEQ, D)], axis=1)
            v_s = jnp.concatenate([cache_diff_v[:, j].reshape(DEC_BATCH, PAST, D).astype(BF16),
                                   vb[N_P:].reshape(DEC_BATCH, DEC_SEQ, D)], axis=1)
            o_p = diff_attention(lam_init, lvecs, subln, q, k_p, v_p, 0, SEQ, SEQ)
            o_s = diff_attention(lam_init, lvecs, subln, q, k_s, v_s, N_P, DEC_SEQ, 256)
            x = proj_residual(jnp.concatenate([o_p, o_s], axis=0), diff_w_o[j].astype(BF16), x, mod)
        f = i // 2
        if i % 2 == 0:
            x = ffn_residual(x, gffn, mod, ffn_w_in[f].astype(BF16), ffn_w_out[f].astype(BF16))
        else:
            wr = jnp.concatenate([moe_w_router[f], jnp.zeros((D, LANES - N_EXPERTS), F32)], axis=1).astype(BF16)
            br = jnp.concatenate([moe_b_router[f], jnp.zeros((LANES - N_EXPERTS,), F32)]).reshape(1, LANES)
            x = moe_residual(x, gffn, mod, wr, br, moe_w_in[f].astype(BF16), moe_w_out[f].astype(BF16))
    y = final_norm(x, norm_final.reshape(1, D))
    return (y[:N_P].reshape(BATCH, SEQ, D), y[N_P:].reshape(DEC_BATCH, DEC_SEQ, D),
            jnp.stack(new_ckv, axis=1), jnp.stack(new_kr, axis=1),
            jnp.stack(new_s5_re, axis=1), jnp.stack(new_s5_im, axis=1),
            jnp.stack(new_dk, axis=1), jnp.stack(new_dv, axis=1))
```

```python
import functools
import math

import jax
import jax.numpy as jnp
from jax import lax
from jax.experimental import pallas as pl
from jax.experimental.pallas import tpu as pltpu

D = 1024
BATCH = 16
SEQ = 256
DEPTH = 4
DEC_BATCH = 2
DEC_SEQ = 4096
PAST = 512
GRID_W = 64
N_P = BATCH * SEQ
N_S = DEC_BATCH * DEC_SEQ
N_TOK = N_P + N_S
N_GROUPS = 1 + DEC_BATCH

MLA_HEADS = 8
MLA_Q_RANK = 384
MLA_KV_RANK = 256
MLA_D_NOPE = 128
MLA_D_ROPE = 64
MLA_D_V = 128
MLA_DK = MLA_D_NOPE + MLA_D_ROPE

S5_GROUP = 16
S5_GROUPS = D // S5_GROUP
S5_STATE = 64
S5_GB = 8
S5_NGB = S5_GROUPS // S5_GB
S5_HALF = S5_GB * S5_STATE
S5_NCH = S5_HALF // 128
S5_SUB = 8

DIFF_HEADS = 8
DIFF_DH = D // (2 * DIFF_HEADS)

D_FF = 2816
N_EXPERTS = 8
ROPE_THETA = 10000.0
EPS = 1e-6

TM = 512
FF_TILE = 1408
LANES = 128
VMEM_LIMIT = 56 * 1024 * 1024

F32 = jnp.float32
BF16 = jnp.bfloat16


def _cparams(*sem):
    return pltpu.CompilerParams(dimension_semantics=sem, vmem_limit_bytes=VMEM_LIMIT)


def _group_of_tile(i, tm):
    n_p = N_P // tm
    per = DEC_SEQ // tm
    return jnp.where(i < n_p, 0, 1 + (i - n_p) // per)


def _rope_tile(i, tm):
    n_p = N_P // tm
    per = DEC_SEQ // tm
    return jnp.where(i < n_p, 0, 1 + (i - n_p) % per)


def _rms(x, g):
    return x * lax.rsqrt(jnp.mean(x * x, axis=-1, keepdims=True) + EPS) * g


def _normmod(x, g, mod, k_shift, k_scale):
    return _rms(x, g) * (1.0 + mod[k_scale:k_scale + 1, :]) + mod[k_shift:k_shift + 1, :]


def _sigmoid(x):
    return 1.0 / (1.0 + jnp.exp(-x))


def _dot(a, b):
    return jnp.dot(a, b, preferred_element_type=F32)


def _dot_nt(a, b):
    return lax.dot_general(a, b, (((1,), (1,)), ((), ())), preferred_element_type=F32)


def _rope(x, cos, sin):
    lane = lax.broadcasted_iota(jnp.int32, x.shape, 1)
    nxt = pltpu.roll(x, LANES - 16, 1)
    prv = pltpu.roll(x, 16, 1)
    swapped = jnp.where((lane % 32) < 16, nxt, prv)
    return x * cos + swapped * sin


def _ada_kernel(c_ref, w_ref, b_ref, o_ref):
    c = c_ref[...]
    s = (c * _sigmoid(c)).astype(BF16)
    o_ref[...] = _dot(s, w_ref[...].astype(BF16)) + b_ref[...]


def ada_all(cond8, ada_w, ada_b):
    tn = 1536
    return pl.pallas_call(
        _ada_kernel,
        grid=(DEPTH, 6 * D // tn),
        in_specs=[pl.BlockSpec((8, D), lambda l, n: (0, 0)),
                  pl.BlockSpec((None, D, tn), lambda l, n: (l, 0, n)),
                  pl.BlockSpec((None, 1, tn), lambda l, n: (l, 0, n))],
        out_specs=pl.BlockSpec((None, 8, tn), lambda l, n: (l, 0, n)),
        out_shape=jax.ShapeDtypeStruct((DEPTH, 8, 6 * D), F32),
        compiler_params=_cparams("parallel", "parallel"),
        name="ada",
    )(cond8, ada_w, ada_b.reshape(DEPTH, 1, 6 * D))


def _mod_spec(tm):
    return pl.BlockSpec((None, 6, D), lambda i, *_: (_group_of_tile(i, tm), 0, 0))


def _mla_tok_kernel(x_ref, g_ref, mod_ref, w1_ref, qn_ref, wuq_ref, kvn_ref, cos_ref, sin_ref,
                    q_ref, ckv_ref, kr_ref):
    h = _normmod(x_ref[...], g_ref[...], mod_ref[...], 0, 1).astype(BF16)
    t1 = _dot(h, w1_ref[...])
    ql = _rms(t1[:, :MLA_Q_RANK], qn_ref[...]).astype(BF16)
    q = _dot(ql, wuq_ref[...])
    c0 = MLA_Q_RANK
    ckv_ref[...] = _rms(t1[:, c0:c0 + MLA_KV_RANK], kvn_ref[...])
    cos = cos_ref[...]
    sin = sin_ref[...]
    kr = _rope(t1[:, c0 + MLA_KV_RANK:c0 + MLA_KV_RANK + LANES], cos, sin)
    kr_ref[...] = kr[:, :MLA_D_ROPE]
    n_nope = MLA_HEADS * MLA_D_NOPE
    for pair in range(MLA_HEADS // 2):
        qr = _rope(q[:, n_nope + pair * LANES:n_nope + (pair + 1) * LANES], cos, sin).astype(BF16)
        for sub in range(2):
            hd = 2 * pair + sub
            q_ref[hd, :, 0:MLA_D_NOPE] = q[:, hd * MLA_D_NOPE:(hd + 1) * MLA_D_NOPE].astype(BF16)
            q_ref[hd, :, MLA_D_NOPE:MLA_DK] = qr[:, sub * MLA_D_ROPE:(sub + 1) * MLA_D_ROPE]


def mla_tokens(x, g, mod, w1, qn, wuq, kvn, cos_t, sin_t):
    nt = N_TOK // TM
    const = lambda shape: pl.BlockSpec(shape, lambda i: (0,) * len(shape))
    return pl.pallas_call(
        _mla_tok_kernel,
        grid=(nt,),
        in_specs=[pl.BlockSpec((TM, D), lambda i: (i, 0)), const((1, D)), _mod_spec(TM),
                  const(w1.shape), const((1, MLA_Q_RANK)), const(wuq.shape), const((1, MLA_KV_RANK)),
                  pl.BlockSpec((TM, LANES), lambda i: (_rope_tile(i, TM), 0)),
                  pl.BlockSpec((TM, LANES), lambda i: (_rope_tile(i, TM), 0))],
        out_specs=[pl.BlockSpec((MLA_HEADS, TM, MLA_DK), lambda i: (0, i, 0)),
                   pl.BlockSpec((TM, MLA_KV_RANK), lambda i: (i, 0)),
                   pl.BlockSpec((TM, MLA_D_ROPE), lambda i: (i, 0))],
        out_shape=[jax.ShapeDtypeStruct((MLA_HEADS, N_TOK, MLA_DK), BF16),
                   jax.ShapeDtypeStruct((N_TOK, MLA_KV_RANK), F32),
                   jax.ShapeDtypeStruct((N_TOK, MLA_D_ROPE), F32)],
        compiler_params=_cparams("parallel"),
        name="mla_tokens",
    )(x, g, mod, w1, qn, wuq, kvn, cos_t, sin_t)


def _mla_kv_kernel(ckv_ref, kr_ref, w_ref, k_ref, v_ref):
    kv = _dot(ckv_ref[...].astype(BF16), w_ref[...])
    kr = kr_ref[...].astype(BF16)
    n_nope = MLA_HEADS * MLA_D_NOPE
    for hd in range(MLA_HEADS):
        k_ref[hd, :, 0:MLA_D_NOPE] = kv[:, hd * MLA_D_NOPE:(hd + 1) * MLA_D_NOPE].astype(BF16)
        k_ref[hd, :, MLA_D_NOPE:MLA_DK] = kr
        v_ref[hd] = kv[:, n_nope + hd * MLA_D_V:n_nope + (hd + 1) * MLA_D_V].astype(BF16)


def mla_kv(ckv, kr, wukv, ts):
    nb, s, _ = ckv.shape
    return pl.pallas_call(
        _mla_kv_kernel,
        grid=(nb, s // ts),
        in_specs=[pl.BlockSpec((None, ts, MLA_KV_RANK), lambda b, t: (b, t, 0)),
                  pl.BlockSpec((None, ts, MLA_D_ROPE), lambda b, t: (b, t, 0)),
                  pl.BlockSpec(wukv.shape, lambda b, t: (0, 0))],
        out_specs=[pl.BlockSpec((None, MLA_HEADS, ts, MLA_DK), lambda b, t: (b, 0, t, 0)),
                   pl.BlockSpec((None, MLA_HEADS, ts, MLA_D_V), lambda b, t: (b, 0, t, 0))],
        out_shape=[jax.ShapeDtypeStruct((nb, MLA_HEADS, s, MLA_DK), BF16),
                   jax.ShapeDtypeStruct((nb, MLA_HEADS, s, MLA_D_V), BF16)],
        compiler_params=_cparams("parallel", "parallel"),
        name="mla_kv",
    )(ckv, kr, wukv)


def _mla_attn_kernel(q_ref, k_ref, v_ref, o_ref):
    s = _dot_nt(q_ref[...], k_ref[...]) * (MLA_DK ** -0.5)
    m = jnp.max(s, axis=-1, keepdims=True)
    p = jnp.exp(s - m)
    l = jnp.sum(p, axis=-1, keepdims=True)
    o_ref[...] = (_dot(p.astype(BF16), v_ref[...]) / l).astype(BF16)


def mla_attention(q3, k3, v3, row0, seq, tq):
    nb, _, s, _ = k3.shape
    nq = seq // tq
    base = row0 // tq
    return pl.pallas_call(
        _mla_attn_kernel,
        grid=(nb, MLA_HEADS, nq),
        in_specs=[pl.BlockSpec((None, tq, MLA_DK), lambda b, h, i: (h, base + b * nq + i, 0)),
                  pl.BlockSpec((None, None, s, MLA_DK), lambda b, h, i: (b, h, 0, 0)),
                  pl.BlockSpec((None, None, s, MLA_D_V), lambda b, h, i: (b, h, 0, 0))],
        out_specs=pl.BlockSpec((tq, MLA_D_V), lambda b, h, i: (b * nq + i, h)),
        out_shape=jax.ShapeDtypeStruct((nb * seq, MLA_HEADS * MLA_D_V), BF16),
        compiler_params=_cparams("parallel", "parallel", "parallel"),
        name="mla_attn_%d" % s,
    )(q3, k3, v3)


def _proj_res_kernel(o_ref, w_ref, x_ref, mod_ref, out_ref):
    out_ref[...] = x_ref[...] + mod_ref[2:3, :] * _dot(o_ref[...], w_ref[...])


def proj_residual(o, w, x, mod):
    nt = N_TOK // TM
    return pl.pallas_call(
        _proj_res_kernel,
        grid=(nt,),
        in_specs=[pl.BlockSpec((TM, D), lambda i: (i, 0)), pl.BlockSpec((D, D), lambda i: (0, 0)),
                  pl.BlockSpec((TM, D), lambda i: (i, 0)), _mod_spec(TM)],
        out_specs=pl.BlockSpec((TM, D), lambda i: (i, 0)),
        out_shape=jax.ShapeDtypeStruct((N_TOK, D), F32),
        compiler_params=_cparams("parallel"),
        name="proj_residual",
    )(o, w, x, mod)


def _diff_tok_kernel(x_ref, g_ref, mod_ref, w_ref, cos_ref, sin_ref, q_ref, k_ref, v_ref, kc_ref):
    h = _normmod(x_ref[...], g_ref[...], mod_ref[...], 0, 1).astype(BF16)
    cos = cos_ref[...]
    sin = sin_ref[...]
    for c in range(D // LANES):
        sl = slice(c * LANES, (c + 1) * LANES)
        q = _dot(h, w_ref[:, c * LANES:(c + 1) * LANES])
        k = _dot(h, w_ref[:, D + c * LANES:D + (c + 1) * LANES])
        v = _dot(h, w_ref[:, 2 * D + c * LANES:2 * D + (c + 1) * LANES])
        q_ref[:, sl] = _rope(q, cos, sin).astype(BF16)
        k_ref[:, sl] = k
        kc_ref[:, sl] = _rope(k, cos, sin).astype(BF16)
        v_ref[:, sl] = v


def diff_tokens(x, g, mod, wqkv, cos_t, sin_t):
    nt = N_TOK // TM
    row = lambda i: (i, 0)
    return pl.pallas_call(
        _diff_tok_kernel,
        grid=(nt,),
        in_specs=[pl.BlockSpec((TM, D), row), pl.BlockSpec((1, D), lambda i: (0, 0)), _mod_spec(TM),
                  pl.BlockSpec((D, 3 * D), lambda i: (0, 0)),
                  pl.BlockSpec((TM, LANES), lambda i: (_rope_tile(i, TM), 0)),
                  pl.BlockSpec((TM, LANES), lambda i: (_rope_tile(i, TM), 0))],
        out_specs=[pl.BlockSpec((TM, D), row)] * 4,
        out_shape=[jax.ShapeDtypeStruct((N_TOK, D), BF16), jax.ShapeDtypeStruct((N_TOK, D), F32),
                   jax.ShapeDtypeStruct((N_TOK, D), F32), jax.ShapeDtypeStruct((N_TOK, D), BF16)],
        compiler_params=_cparams("parallel"),
        name="diff_tokens",
    )(x, g, mod, wqkv, cos_t, sin_t)


def _diff_attn_kernel(lam_init, lq1_ref, lk1_ref, lq2_ref, lk2_ref, sub_ref, q_ref, k_ref, v_ref, o_ref):
    lam = (jnp.exp(jnp.sum(lq1_ref[...] * lk1_ref[...], axis=-1, keepdims=True))
           - jnp.exp(jnp.sum(lq2_ref[...] * lk2_ref[...], axis=-1, keepdims=True)) + lam_init)
    q = q_ref[...]
    k = k_ref[...]
    lane = lax.broadcasted_iota(jnp.int32, q.shape, 1)
    zero = jnp.zeros_like(q)
    scale = DIFF_DH ** -0.5

    def probs(qh):
        s = _dot_nt(qh, k) * scale
        p = jnp.exp(s - jnp.max(s, axis=-1, keepdims=True))
        return p / jnp.sum(p, axis=-1, keepdims=True)

    p1 = probs(jnp.where(lane < DIFF_DH, q, zero))
    p2 = probs(jnp.where(lane >= DIFF_DH, q, zero))
    att = (p1 - lam * p2).astype(BF16)
    o = _dot(att, v_ref[...])
    o_ref[...] = (_rms(o, sub_ref[...]) * (1.0 - lam_init)).astype(BF16)


def diff_attention(lam_init, lvecs, subln, q, k, v, row0, seq, tq):
    nb, s, _ = k.shape
    nq = seq // tq
    base = row0 // tq
    vec = pl.BlockSpec((1, DIFF_DH), lambda b, h, i: (0, 0))
    return pl.pallas_call(
        functools.partial(_diff_attn_kernel, lam_init),
        grid=(nb, DIFF_HEADS, nq),
        in_specs=[vec, vec, vec, vec, pl.BlockSpec((1, 2 * DIFF_DH), lambda b, h, i: (0, 0)),
                  pl.BlockSpec((tq, LANES), lambda b, h, i: (base + b * nq + i, h)),
                  pl.BlockSpec((None, s, LANES), lambda b, h, i: (b, 0, h)),
                  pl.BlockSpec((None, s, LANES), lambda b, h, i: (b, 0, h))],
        out_specs=pl.BlockSpec((tq, LANES), lambda b, h, i: (b * nq + i, h)),
        out_shape=jax.ShapeDtypeStruct((nb * seq, D), BF16),
        compiler_params=_cparams("parallel", "parallel", "parallel"),
        name="diff_attn_%d" % s,
    )(*lvecs, subln, q, k, v)


def _normmod_kernel(x_ref, g_ref, mod_ref, h_ref):
    h_ref[...] = _normmod(x_ref[...], g_ref[...], mod_ref[...], 0, 1)


S5_T = SEQ
S5_SEG = DEC_SEQ // S5_SUB
S5_TILES_P = N_P // S5_T
S5_PER_SEG = S5_SEG // S5_T


def _s5_tile_pos(i):
    k = i - S5_TILES_P
    per_batch = S5_SUB * S5_PER_SEG
    lat_row = S5_TILES_P // S5_SUB + (k // per_batch) * S5_PER_SEG + k % S5_PER_SEG
    lat_col = (k % per_batch) // S5_PER_SEG
    is_p = i < S5_TILES_P
    return jnp.where(is_p, i // S5_SUB, lat_row), jnp.where(is_p, i % S5_SUB, lat_col)


def normmod_time_major(x, g, mod):
    nt = N_TOK // S5_T
    return pl.pallas_call(
        _normmod_kernel,
        grid=(nt,),
        in_specs=[pl.BlockSpec((S5_T, D), lambda i: (i, 0)), pl.BlockSpec((1, D), lambda i: (0, 0)),
                  _mod_spec(S5_T)],
        out_specs=pl.BlockSpec((S5_T, D), lambda i: _s5_tile_pos(i)),
        out_shape=jax.ShapeDtypeStruct((N_TOK // S5_SUB, S5_SUB * D), F32),
        compiler_params=_cparams("parallel"),
        name="normmod",
    )(x, g, mod)


def _s5_kernel(chained, n, u_ref, wb_ref, wc_ref, a_ref, an_ref, dsk_ref, h0_ref, y_ref, fin_ref,
               bu_ref, ini_ref):
    d = pl.program_id(2)
    rows = S5_SUB * n
    chunk = 512
    nch = S5_NCH
    for r in range(rows // chunk):
        rs = slice(r * chunk, (r + 1) * chunk)
        bu = _dot(u_ref[rs, :].astype(BF16), wb_ref[...])
        for c in range(2 * nch):
            bu_ref[c, rs, :] = bu[:, c * LANES:(c + 1) * LANES]
    ar = [jnp.broadcast_to(a_ref[0:1, c * LANES:(c + 1) * LANES], (S5_SUB, LANES)) for c in range(nch)]
    ai = [jnp.broadcast_to(a_ref[1:2, c * LANES:(c + 1) * LANES], (S5_SUB, LANES)) for c in range(nch)]

    def step_index(s):
        return jnp.where(d == 0, s, n - 1 - s)

    def step_rows(t):
        return pl.ds(pl.multiple_of(t * S5_SUB, S5_SUB), S5_SUB)

    def advance(h, t):
        out = [None] * (2 * nch)
        for c in range(nch):
            br = bu_ref[c, step_rows(t), :]
            bi = bu_ref[nch + c, step_rows(t), :]
            out[c] = ar[c] * h[c] - ai[c] * h[nch + c] + br
            out[nch + c] = ar[c] * h[nch + c] + ai[c] * h[c] + bi
        return out

    unroll = 4
    zeros = [jnp.zeros((S5_SUB, LANES), F32) for _ in range(2 * nch)]

    if chained:
        def local_body(s, h):
            h = list(h)
            for k in range(unroll):
                h = advance(h, step_index(s * unroll + k))
            return tuple(h)

        ends = lax.fori_loop(0, n // unroll, local_body, tuple(zeros))
        sub_row = lax.broadcasted_iota(jnp.int32, (S5_SUB, LANES), 0)
        cur = [h0_ref[:, c * LANES:(c + 1) * LANES] for c in range(2 * nch)]
        anr = [an_ref[0:1, c * LANES:(c + 1) * LANES] for c in range(nch)]
        ani = [an_ref[1:2, c * LANES:(c + 1) * LANES] for c in range(nch)]
        for kk in range(S5_SUB):
            j = jnp.where(d == 0, kk, S5_SUB - 1 - kk)
            nxt = [None] * (2 * nch)
            for c in range(2 * nch):
                ini_ref[c, pl.ds(j, 1), :] = cur[c]
            for c in range(nch):
                er = jnp.sum(jnp.where(sub_row == j, ends[c], 0.0), axis=0, keepdims=True)
                ei = jnp.sum(jnp.where(sub_row == j, ends[nch + c], 0.0), axis=0, keepdims=True)
                nxt[c] = er + anr[c] * cur[c] - ani[c] * cur[nch + c]
                nxt[nch + c] = ei + anr[c] * cur[nch + c] + ani[c] * cur[c]
            cur = nxt
        h_init = [ini_ref[c] for c in range(2 * nch)]
    else:
        h_init = zeros

    def body(s, h):
        h = list(h)
        for k in range(unroll):
            t = step_index(s * unroll + k)
            h = advance(h, t)
            for c in range(2 * nch):
                bu_ref[c, step_rows(t), :] = h[c]
        return tuple(h)

    fin = lax.fori_loop(0, n // unroll, body, tuple(h_init))
    for c in range(2 * nch):
        fin_ref[:, c * LANES:(c + 1) * LANES] = fin[c]

    @pl.when(d == 0)
    def _():
        y_ref[...] = dsk_ref[...] * u_ref[...]

    for r in range(rows // chunk):
        rs = slice(r * chunk, (r + 1) * chunk)
        hs = jnp.concatenate([bu_ref[c, rs, :].astype(BF16) for c in range(2 * nch)], axis=1)
        y_ref[rs, :] += _dot(hs, wc_ref[...])


def s5_scan(h, wb, wc, a, an, dskip, h0, row0, n_blocks, n, chained):
    rows = S5_SUB * n
    base = row0 // rows
    kern = functools.partial(_s5_kernel, chained, n)
    return pl.pallas_call(
        kern,
        grid=(n_blocks, S5_NGB, 2),
        in_specs=[pl.BlockSpec((rows, LANES), lambda r, c, d: (base + r, c)),
                  pl.BlockSpec((None, None, LANES, 2 * S5_HALF), lambda r, c, d: (d, c, 0, 0)),
                  pl.BlockSpec((None, None, 2 * S5_HALF, LANES), lambda r, c, d: (d, c, 0, 0)),
                  pl.BlockSpec((None, None, 2, S5_HALF), lambda r, c, d: (d, c, 0, 0)),
                  pl.BlockSpec((None, None, 2, S5_HALF), lambda r, c, d: (d, c, 0, 0)),
                  pl.BlockSpec((1, LANES), lambda r, c, d: (0, c)),
                  pl.BlockSpec((None, None, None, 1, 2 * S5_HALF), lambda r, c, d: (r, d, c, 0, 0))],
        out_specs=[pl.BlockSpec((rows, LANES), lambda r, c, d: (r, c)),
                   pl.BlockSpec((None, None, None, S5_SUB, 2 * S5_HALF), lambda r, c, d: (r, d, c, 0, 0))],
        out_shape=[jax.ShapeDtypeStruct((n_blocks * rows, D), F32),
                   jax.ShapeDtypeStruct((n_blocks, 2, S5_NGB, S5_SUB, 2 * S5_HALF), F32)],
        scratch_shapes=[pltpu.VMEM((2 * S5_NCH, rows, LANES), F32),
                        pltpu.VMEM((2 * S5_NCH, S5_SUB, LANES), F32)],
        compiler_params=_cparams("parallel", "parallel", "arbitrary"),
        name="s5_scan_%d" % n,
    )(h, wb, wc, a, an, dskip, h0)


def _glu_res_kernel(yp_ref, ys_ref, w_ref, x_ref, mod_ref, out_ref):
    y = jnp.where(pl.program_id(0) < S5_TILES_P, yp_ref[...], ys_ref[...])
    g = 0.5 * y * (1.0 + jnp.tanh(math.sqrt(2.0 / math.pi) * (y + 0.044715 * (y * y * y))))
    t = _dot(g.astype(BF16), w_ref[...])
    out_ref[...] = x_ref[...] + mod_ref[2:3, :] * (t[:, :D] * _sigmoid(t[:, D:]))


def glu_residual(y_p, y_s, w, x, mod):
    nt = N_TOK // S5_T
    n_blk_p = S5_TILES_P // S5_SUB

    def yp_map(i):
        r, c = _s5_tile_pos(jnp.minimum(i, S5_TILES_P - 1))
        return r, c

    def ys_map(i):
        r, c = _s5_tile_pos(jnp.maximum(i, S5_TILES_P))
        return r - n_blk_p, c

    return pl.pallas_call(
        _glu_res_kernel,
        grid=(nt,),
        in_specs=[pl.BlockSpec((S5_T, D), yp_map), pl.BlockSpec((S5_T, D), ys_map),
                  pl.BlockSpec((D, 2 * D), lambda i: (0, 0)),
                  pl.BlockSpec((S5_T, D), lambda i: (i, 0)), _mod_spec(S5_T)],
        out_specs=pl.BlockSpec((S5_T, D), lambda i: (i, 0)),
        out_shape=jax.ShapeDtypeStruct((N_TOK, D), F32),
        compiler_params=_cparams("parallel"),
        name="glu_residual",
    )(y_p.reshape(N_P // S5_SUB, S5_SUB * D), y_s.reshape(N_S // S5_SUB, S5_SUB * D), w, x, mod)


def _ffn_kernel(x_ref, g_ref, mod_ref, wa_ref, wb_ref, wo_ref, out_ref, h_scr, acc_scr):
    f = pl.program_id(1)

    @pl.when(f == 0)
    def _():
        h_scr[...] = _normmod(x_ref[...], g_ref[...], mod_ref[...], 3, 4).astype(BF16)
        acc_scr[...] = jnp.zeros_like(acc_scr)

    h = h_scr[...]
    a = _dot(h, wa_ref[...])
    b = _dot(h, wb_ref[...])
    act = (a * _sigmoid(a) * b).astype(BF16)
    acc_scr[...] += _dot(act, wo_ref[...])

    @pl.when(f == pl.num_programs(1) - 1)
    def _():
        out_ref[...] = x_ref[...] + mod_ref[5:6, :] * acc_scr[...]


def ffn_residual(x, g, mod, w_in, w_out):
    nt = N_TOK // TM
    nf = D_FF // FF_TILE
    return pl.pallas_call(
        _ffn_kernel,
        grid=(nt, nf),
        in_specs=[pl.BlockSpec((TM, D), lambda i, f: (i, 0)), pl.BlockSpec((1, D), lambda i, f: (0, 0)),
                  _mod_spec(TM),
                  pl.BlockSpec((D, FF_TILE), lambda i, f: (0, f)),
                  pl.BlockSpec((D, FF_TILE), lambda i, f: (0, f + nf)),
                  pl.BlockSpec((FF_TILE, D), lambda i, f: (f, 0))],
        out_specs=pl.BlockSpec((TM, D), lambda i, f: (i, 0)),
        out_shape=jax.ShapeDtypeStruct((N_TOK, D), F32),
        scratch_shapes=[pltpu.VMEM((TM, D), BF16), pltpu.VMEM((TM, D), F32)],
        compiler_params=_cparams("parallel", "arbitrary"),
        name="ffn",
    )(x, g, mod, w_in, w_in, w_out)


TS = 512
N_SLOTS = 2 * N_TOK + N_EXPERTS * TS
NT_S = N_SLOTS // TS
PIECE = 256
MAX_PIECES = (TM + 16 + PIECE - 1) // PIECE


def _moe_route_kernel(x_ref, g_ref, mod_ref, wr_ref, br_ref, h_ref, route_ref, run_ref, carry_scr):
    @pl.when(pl.program_id(0) == 0)
    def _():
        carry_scr[...] = jnp.zeros_like(carry_scr)

    h = _normmod(x_ref[...], g_ref[...], mod_ref[...], 3, 4).astype(BF16)
    h_ref[...] = h
    logits = _dot(h, wr_ref[...]) + br_ref[...]
    lane = lax.broadcasted_iota(jnp.int32, logits.shape, 1)
    neg = jnp.float32(-jnp.inf)
    lg = jnp.where(lane < N_EXPERTS, logits, neg)
    v1 = jnp.max(lg, axis=-1, keepdims=True)
    i1 = jnp.min(jnp.where(lg == v1, lane, LANES), axis=-1, keepdims=True)
    lg2 = jnp.where(lane == i1, neg, lg)
    v2 = jnp.max(lg2, axis=-1, keepdims=True)
    i2 = jnp.min(jnp.where(lg2 == v2, lane, LANES), axis=-1, keepdims=True)
    e2 = jnp.exp(v2 - v1)
    g1 = 1.0 / (1.0 + e2)
    g2 = e2 / (1.0 + e2)
    oh1 = lane == i1
    oh2 = lane == i2
    sel = jnp.where(oh1 | oh2, 1.0, 0.0)
    r = lax.broadcasted_iota(jnp.int32, (TM, TM), 0)
    c = lax.broadcasted_iota(jnp.int32, (TM, TM), 1)
    tri = jnp.where(c < r, 1.0, 0.0).astype(BF16)
    rank = _dot(tri, sel.astype(BF16)) + carry_scr[0:1, :]
    r1 = jnp.sum(jnp.where(oh1, rank, 0.0), axis=-1, keepdims=True)
    r2 = jnp.sum(jnp.where(oh2, rank, 0.0), axis=-1, keepdims=True)
    cols = (i1.astype(F32), i2.astype(F32), g1, g2, r1, r2)
    route = jnp.zeros(logits.shape, F32)
    for k, v in enumerate(cols):
        route = jnp.where(lane == k, v, route)
    route_ref[...] = route
    carry_scr[...] = carry_scr[...] + jnp.sum(sel, axis=0, keepdims=True)
    run_ref[...] = carry_scr[...]


def moe_route(x, g, mod, w_router, b_router):
    nt = N_TOK // TM
    return pl.pallas_call(
        _moe_route_kernel,
        grid=(nt,),
        in_specs=[pl.BlockSpec((TM, D), lambda i: (i, 0)), pl.BlockSpec((1, D), lambda i: (0, 0)), _mod_spec(TM),
                  pl.BlockSpec((D, LANES), lambda i: (0, 0)), pl.BlockSpec((1, LANES), lambda i: (0, 0))],
        out_specs=[pl.BlockSpec((TM, D), lambda i: (i, 0)), pl.BlockSpec((TM, LANES), lambda i: (i, 0)),
                   pl.BlockSpec((None, 8, LANES), lambda i: (i, 0, 0))],
        out_shape=[jax.ShapeDtypeStruct((N_TOK, D), BF16), jax.ShapeDtypeStruct((N_TOK, LANES), F32),
                   jax.ShapeDtypeStruct((nt, 8, LANES), F32)],
        scratch_shapes=[pltpu.VMEM((8, LANES), F32)],
        compiler_params=_cparams("arbitrary"),
        name="moe_route",
    )(x, g, mod, w_router, b_router)


def _moe_gather_kernel(used_ref, clo_ref, chi_ref, tok_ref, h_ref, xs_ref, gat_scr):
    i = pl.program_id(0)
    gat_scr[...] = jnp.zeros_like(gat_scr)

    @pl.when(used_ref[i] > 0)
    def _():
        tok = tok_ref[...]
        lane = lax.broadcasted_iota(jnp.int32, (TS, TM), 1)

        def body(c, carry):
            base = pl.multiple_of(c * TM, TM)
            pick = jnp.where(tok - base == lane, 1.0, 0.0).astype(BF16)
            gat_scr[...] += _dot(pick, h_ref[pl.ds(base, TM), :])
            return carry

        lax.fori_loop(clo_ref[i], chi_ref[i] + 1, body, 0)

    xs_ref[...] = gat_scr[...].astype(BF16)


def moe_gather(used, clo, chi, tok_col, h_b):
    grid_spec = pltpu.PrefetchScalarGridSpec(
        num_scalar_prefetch=3,
        grid=(NT_S,),
        in_specs=[pl.BlockSpec((TS, 1), lambda i, *_: (i, 0)),
                  pl.BlockSpec((N_TOK, D), lambda i, *_: (0, 0), pipeline_mode=pl.Buffered(1))],
        out_specs=pl.BlockSpec((TS, D), lambda i, *_: (i, 0)),
        scratch_shapes=[pltpu.VMEM((TS, D), F32)],
    )
    return pl.pallas_call(
        _moe_gather_kernel,
        grid_spec=grid_spec,
        out_shape=jax.ShapeDtypeStruct((N_SLOTS, D), BF16),
        compiler_params=_cparams("arbitrary"),
        name="moe_gather",
    )(used, clo, chi, tok_col, h_b)


def _moe_ffn_kernel(texp_ref, used_ref, xs_ref, gate_ref, wa_ref, wb_ref, wo_ref, ys_ref, acc_scr):
    i = pl.program_id(0)
    f = pl.program_id(1)
    live = used_ref[i] > 0

    @pl.when(f == 0)
    def _():
        acc_scr[...] = jnp.zeros_like(acc_scr)

    @pl.when(live)
    def _():
        xs = xs_ref[...]
        a = _dot(xs, wa_ref[...])
        b = _dot(xs, wb_ref[...])
        act = (a * _sigmoid(a) * b).astype(BF16)
        acc_scr[...] += _dot(act, wo_ref[...])

    @pl.when(f == pl.num_programs(1) - 1)
    def _():
        ys_ref[...] = (gate_ref[...] * acc_scr[...]).astype(BF16)


def moe_ffn(texp, used, xs, gate_col, w_in, w_out):
    nf = D_FF // FF_TILE
    grid_spec = pltpu.PrefetchScalarGridSpec(
        num_scalar_prefetch=2,
        grid=(NT_S, nf),
        in_specs=[pl.BlockSpec((TS, D), lambda i, f, *_: (i, 0)),
                  pl.BlockSpec((TS, 1), lambda i, f, *_: (i, 0)),
                  pl.BlockSpec((None, D, FF_TILE), lambda i, f, texp, *_: (texp[i], 0, f)),
                  pl.BlockSpec((None, D, FF_TILE), lambda i, f, texp, *_: (texp[i], 0, f + nf)),
                  pl.BlockSpec((None, FF_TILE, D), lambda i, f, texp, *_: (texp[i], f, 0))],
        out_specs=pl.BlockSpec((TS, D), lambda i, f, *_: (i, 0)),
        scratch_shapes=[pltpu.VMEM((TS, D), F32)],
    )
    return pl.pallas_call(
        _moe_ffn_kernel,
        grid_spec=grid_spec,
        out_shape=jax.ShapeDtypeStruct((N_SLOTS, D), BF16),
        compiler_params=_cparams("parallel", "arbitrary"),
        name="moe_ffn",
    )(texp, used, xs, gate_col, w_in, w_in, w_out)


def _moe_combine_kernel(start_ref, npc_ref, lo_ref, hi_ref, x_ref, mod_ref, pos_ref, ys_hbm, out_ref,
                        buf, acc_scr, sem):
    i = pl.program_id(0)

    def piece_copy(e, k):
        s = pl.multiple_of(start_ref[i * N_EXPERTS + e] + k * PIECE, 16)
        return pltpu.make_async_copy(ys_hbm.at[pl.ds(s, PIECE), :], buf.at[e, pl.ds(k * PIECE, PIECE), :],
                                     sem.at[e, k])

    for e in range(N_EXPERTS):
        for k in range(MAX_PIECES):
            @pl.when(k < npc_ref[i * N_EXPERTS + e])
            def _():
                piece_copy(e, k).start()

    acc_scr[...] = jnp.zeros_like(acc_scr)
    pos = pos_ref[...]
    lane = lax.broadcasted_iota(jnp.int32, (TM, PIECE), 1)
    for e in range(N_EXPERTS):
        lo = lo_ref[i * N_EXPERTS + e]
        hi = hi_ref[i * N_EXPERTS + e]
        p1 = jnp.where((pos[:, 0:1] >= lo) & (pos[:, 0:1] < hi), pos[:, 0:1], -1)
        p2 = jnp.where((pos[:, 1:2] >= lo) & (pos[:, 1:2] < hi), pos[:, 1:2], -1)
        for k in range(MAX_PIECES):
            @pl.when(k < npc_ref[i * N_EXPERTS + e])
            def _():
                piece_copy(e, k).wait()
                base = start_ref[i * N_EXPERTS + e] + k * PIECE
                pick = jnp.where((p1 - base == lane) | (p2 - base == lane), 1.0, 0.0).astype(BF16)
                acc_scr[...] += _dot(pick, buf[e, k * PIECE:(k + 1) * PIECE, :])

    out_ref[...] = x_ref[...] + mod_ref[5:6, :] * acc_scr[...]


def moe_combine(start, npc, lo, hi, x, mod, pos, ys):
    nt = N_TOK // TM
    grid_spec = pltpu.PrefetchScalarGridSpec(
        num_scalar_prefetch=4,
        grid=(nt,),
        in_specs=[pl.BlockSpec((TM, D), lambda i, *_: (i, 0)),
                  pl.BlockSpec((None, 6, D), lambda i, *_: (_group_of_tile(i, TM), 0, 0)),
                  pl.BlockSpec((TM, LANES), lambda i, *_: (i, 0)),
                  pl.BlockSpec(memory_space=pl.ANY)],
        out_specs=pl.BlockSpec((TM, D), lambda i, *_: (i, 0)),
        scratch_shapes=[pltpu.VMEM((N_EXPERTS, MAX_PIECES * PIECE, D), BF16), pltpu.VMEM((TM, D), F32),
                        pltpu.SemaphoreType.DMA((N_EXPERTS, MAX_PIECES))],
    )
    return pl.pallas_call(
        _moe_combine_kernel,
        grid_spec=grid_spec,
        out_shape=jax.ShapeDtypeStruct((N_TOK, D), F32),
        compiler_params=_cparams("arbitrary"),
        name="moe_combine",
    )(start, npc, lo, hi, x, mod, pos, ys)


def moe_residual(x, g, mod, w_router, b_router, w_in, w_out):
    nt = N_TOK // TM
    h_b, route, run = moe_route(x, g, mod, w_router, b_router)
    i32 = jnp.int32
    e1, e2 = route[:, 0].astype(i32), route[:, 1].astype(i32)
    g1, g2 = route[:, 2], route[:, 3]
    r1, r2 = route[:, 4].astype(i32), route[:, 5].astype(i32)
    run = run[:, 0, :N_EXPERTS].astype(i32)
    total = run[-1]
    padded = (total + TS - 1) // TS * TS
    off_end = jnp.cumsum(padded)
    off = off_end - padded
    pos1 = off[e1] + r1
    pos2 = off[e2] + r2
    tok_ids = jnp.arange(N_TOK, dtype=i32)
    tok_sorted = jnp.full((N_SLOTS,), -1, i32).at[pos1].set(tok_ids).at[pos2].set(tok_ids)
    gate_sorted = jnp.zeros((N_SLOTS,), F32).at[pos1].set(g1).at[pos2].set(g2)
    tile_start = jnp.arange(NT_S, dtype=i32) * TS
    used = (tile_start < off_end[-1]).astype(i32)
    texp = jnp.minimum(jnp.searchsorted(off_end, tile_start, side='right').astype(i32), N_EXPERTS - 1)
    tok_tiles = tok_sorted.reshape(NT_S, TS)
    clo = jnp.where(used > 0, tok_tiles[:, 0] // TM, 0).astype(i32)
    chi = jnp.where(used > 0, jnp.max(tok_tiles, axis=1) // TM, -1).astype(i32)
    xs = moe_gather(used, clo, chi, tok_sorted.reshape(N_SLOTS, 1), h_b)
    ys = moe_ffn(texp, used, xs, gate_sorted.reshape(N_SLOTS, 1), w_in, w_out)
    run_prev = jnp.concatenate([jnp.zeros((1, N_EXPERTS), i32), run[:-1]], axis=0)
    lo = off[None, :] + run_prev
    hi = off[None, :] + run
    start = jnp.minimum(lo // 16 * 16, N_SLOTS - MAX_PIECES * PIECE)
    npc = jnp.where(hi > lo, (hi - start + PIECE - 1) // PIECE, 0)
    pos = jnp.zeros((N_TOK, LANES), i32).at[:, 0].set(pos1).at[:, 1].set(pos2)
    flat = lambda a: a.reshape(nt * N_EXPERTS).astype(i32)
    return moe_combine(flat(start), flat(npc), flat(lo), flat(hi), x, mod, pos, ys)


def _final_norm_kernel(x_ref, g_ref, o_ref):
    o_ref[...] = _rms(x_ref[...], g_ref[...])


def final_norm(x, g):
    nt = N_TOK // TM
    return pl.pallas_call(
        _final_norm_kernel,
        grid=(nt,),
        in_specs=[pl.BlockSpec((TM, D), lambda i: (i, 0)), pl.BlockSpec((1, D), lambda i: (0, 0))],
        out_specs=pl.BlockSpec((TM, D), lambda i: (i, 0)),
        out_shape=jax.ShapeDtypeStruct((N_TOK, D), F32),
        compiler_params=_cparams("parallel"),
        name="final_norm",
    )(x, g)


def _rope_tables():
    half = 16
    freqs = ROPE_THETA ** (-jnp.arange(half, dtype=F32) / half)
    t = jnp.arange(DEC_SEQ, dtype=jnp.int32)
    row = (t // GRID_W).astype(F32)[:, None] * freqs[None, :]
    col = (t % GRID_W).astype(F32)[:, None] * freqs[None, :]
    cos = jnp.concatenate([jnp.cos(row), jnp.cos(row), jnp.cos(col), jnp.cos(col)], axis=1)
    sin = jnp.concatenate([-jnp.sin(row), jnp.sin(row), -jnp.sin(col), jnp.sin(col)], axis=1)
    cos = jnp.concatenate([jnp.ones((TM, 64), F32), cos], axis=0)
    sin = jnp.concatenate([jnp.zeros((TM, 64), F32), sin], axis=0)
    return jnp.tile(cos, (1, 2)), jnp.tile(sin, (1, 2))


def _mla_weights(w_dq, w_uq, w_dkv, w_ukv, w_o):
    w1 = jnp.concatenate([w_dq, w_dkv, jnp.zeros((D, LANES - MLA_D_ROPE), F32)], axis=1).astype(BF16)
    uq = w_uq.reshape(MLA_Q_RANK, MLA_HEADS, MLA_DK)
    wuq = jnp.concatenate([uq[:, :, :MLA_D_NOPE].reshape(MLA_Q_RANK, -1),
                           uq[:, :, MLA_D_NOPE:].reshape(MLA_Q_RANK, -1)], axis=1).astype(BF16)
    ukv = w_ukv.reshape(MLA_KV_RANK, MLA_HEADS, MLA_D_NOPE + MLA_D_V)
    wukv = jnp.concatenate([ukv[:, :, :MLA_D_NOPE].reshape(MLA_KV_RANK, -1),
                            ukv[:, :, MLA_D_NOPE:].reshape(MLA_KV_RANK, -1)], axis=1).astype(BF16)
    return w1, wuq, wukv, w_o.astype(BF16)


def _s5_weights(a_re, a_im, log_dt, b_re, b_im, c_re, c_im, seg_len):
    dt = jnp.exp(log_dt)[..., None]
    mag = jnp.exp(a_re * dt)
    abar_re, abar_im = mag * jnp.cos(a_im * dt), mag * jnp.sin(a_im * dt)
    mag_n = jnp.exp(a_re * dt * seg_len)
    apow_re, apow_im = mag_n * jnp.cos(a_im * dt * seg_len), mag_n * jnp.sin(a_im * dt * seg_len)
    den = a_re * a_re + a_im * a_im
    coef_re = ((abar_re - 1.0) * a_re + abar_im * a_im) / den
    coef_im = (abar_im * a_re - (abar_re - 1.0) * a_im) / den
    bbar_re = coef_re[..., None] * b_re - coef_im[..., None] * b_im
    bbar_im = coef_re[..., None] * b_im + coef_im[..., None] * b_re
    eye = jnp.eye(S5_GB, dtype=F32)

    def in_block(m):
        m = m.reshape(2, S5_NGB, S5_GB, S5_STATE, S5_GROUP)
        return jnp.einsum('dbgpc,gh->dbgchp', m, eye).reshape(2, S5_NGB, LANES, S5_HALF)

    def out_block(m):
        m = m.reshape(2, S5_NGB, S5_GB, S5_GROUP, S5_STATE)
        return jnp.einsum('dbgcp,gh->dbgphc', m, eye).reshape(2, S5_NGB, S5_HALF, LANES)

    wb = jnp.concatenate([in_block(bbar_re), in_block(bbar_im)], axis=3).astype(BF16)
    wc = jnp.concatenate([out_block(c_re), out_block(-c_im)], axis=2).astype(BF16)
    lanes = lambda m: m.reshape(2, S5_NGB, 1, S5_HALF)
    a = jnp.concatenate([lanes(abar_re), lanes(abar_im)], axis=2)
    an = jnp.concatenate([lanes(apow_re), lanes(apow_im)], axis=2)
    return wb, wc, a, an


def kernel(x_prompt, x_sample, c, c_ctx, cache_mla_ckv, cache_mla_krope, state_s5_re, state_s5_im, cache_diff_k, cache_diff_v, ada_w, ada_b, norm_mix, norm_ffn, norm_final, mla_w_dq, mla_q_norm, mla_w_uq, mla_w_dkv, mla_kv_norm, mla_w_ukv, mla_w_o, s5_a_re, s5_a_im, s5_log_dt, s5_b_re, s5_b_im, s5_c_re, s5_c_im, s5_d, s5_w_glu, diff_w_qkv, diff_lq1, diff_lk1, diff_lq2, diff_lk2, diff_subln, diff_w_o, ffn_w_in, ffn_w_out, moe_w_router, moe_b_router, moe_w_in, moe_w_out):
    x = jnp.concatenate([x_prompt.reshape(N_P, D), x_sample.reshape(N_S, D)], axis=0)
    cond8 = jnp.concatenate([c_ctx[None], c, jnp.zeros((8 - N_GROUPS, D), F32)], axis=0)
    mods = ada_all(cond8, ada_w, ada_b).reshape(DEPTH, 8, 6, D)[:, :N_GROUPS]
    cos_t, sin_t = _rope_tables()

    new_ckv, new_kr, new_s5_re, new_s5_im, new_dk, new_dv = [], [], [], [], [], []
    for i in range(DEPTH):
        mod = mods[i]
        gmix = norm_mix[i].reshape(1, D)
        gffn = norm_ffn[i].reshape(1, D)
        j = i // 3
        kind = i % 3
        if kind == 0:
            w1, wuq, wukv, wo = _mla_weights(mla_w_dq[j], mla_w_uq[j], mla_w_dkv[j], mla_w_ukv[j], mla_w_o[j])
            q3, ckv, kr = mla_tokens(x, gmix, mod, w1, mla_q_norm[j].reshape(1, -1), wuq,
                                     mla_kv_norm[j].reshape(1, -1), cos_t, sin_t)
            ckv_p = ckv[:N_P].reshape(BATCH, SEQ, MLA_KV_RANK)
            kr_p = kr[:N_P].reshape(BATCH, SEQ, MLA_D_ROPE)
            new_ckv.append(ckv_p)
            new_kr.append(kr_p)
            ckv_s = jnp.concatenate([cache_mla_ckv[:, j], ckv[N_P:].reshape(DEC_BATCH, DEC_SEQ, -1)], axis=1)
            kr_s = jnp.concatenate([cache_mla_krope[:, j], kr[N_P:].reshape(DEC_BATCH, DEC_SEQ, -1)], axis=1)
            k3p, v3p = mla_kv(ckv_p, kr_p, wukv, SEQ)
            k3s, v3s = mla_kv(ckv_s, kr_s, wukv, 512)
            o_p = mla_attention(q3, k3p, v3p, 0, SEQ, SEQ)
            o_s = mla_attention(q3, k3s, v3s, N_P, DEC_SEQ, 512)
            x = proj_residual(jnp.concatenate([o_p, o_s], axis=0), wo, x, mod)
        elif kind == 1:
            h = normmod_time_major(x, gmix, mod).reshape(N_TOK, D)
            dsk = s5_d[j].reshape(1, D)
            seg = S5_SEG
            wb, wc, a, an = _s5_weights(s5_a_re[j], s5_a_im[j], s5_log_dt[j], s5_b_re[j], s5_b_im[j],
                                        s5_c_re[j], s5_c_im[j], seg)
            zero_h0 = jnp.zeros((BATCH // S5_SUB, 2, S5_NGB, 1, 2 * S5_HALF), F32)
            y_p, fin = s5_scan(h, wb, wc, a, an, dsk, zero_h0, 0, BATCH // S5_SUB, SEQ, False)
            fin = fin.reshape(BATCH // S5_SUB, 2, S5_NGB, S5_SUB, 2, S5_GB, S5_STATE)
            fin = jnp.transpose(fin, (0, 3, 1, 4, 2, 5, 6)).reshape(BATCH, 2, 2, S5_GROUPS, S5_STATE)
            new_s5_re.append(fin[:, :, 0])
            new_s5_im.append(fin[:, :, 1])
            h0 = jnp.stack([state_s5_re[:, j], state_s5_im[:, j]], axis=2)
            h0 = h0.reshape(DEC_BATCH, 2, 2, S5_NGB, S5_HALF)
            h0 = jnp.transpose(h0, (0, 1, 3, 2, 4)).reshape(DEC_BATCH, 2, S5_NGB, 1, 2 * S5_HALF)
            y_s, _ = s5_scan(h, wb, wc, a, an, dsk, h0, N_P, DEC_BATCH, seg, True)
            x = glu_residual(y_p, y_s, s5_w_glu[j].astype(BF16), x, mod)
        else:
            lam_init = 0.8 - 0.6 * math.exp(-0.3 * i)
            q, k, v, kc = diff_tokens(x, gmix, mod, diff_w_qkv[j].astype(BF16), cos_t, sin_t)
            new_dk.append(k[:N_P].reshape(BATCH, SEQ, 2 * DIFF_HEADS, DIFF_DH))
            new_dv.append(v[:N_P].reshape(BATCH, SEQ, DIFF_HEADS, 2 * DIFF_DH))
            lvecs = [a_.reshape(1, DIFF_DH) for a_ in (diff_lq1[j], diff_lk1[j], diff_lq2[j], diff_lk2[j])]
            subln = diff_subln[j].reshape(1, 2 * DIFF_DH)
            vb = v.astype(BF16)
            k_p = kc[:N_P].reshape(BATCH, SEQ, D)
            v_p = vb[:N_P].reshape(BATCH, SEQ, D)
            k_s = jnp.concatenate([cache_diff_k[:, j].reshape(DEC_BATCH, PAST, D).astype(BF16),
                                   kc[N_P:].reshape(DEC_BATCH, DEC_SEQ, D)], axis=1)
            v_s = jnp.concatenate([cache_diff_v[:, j].reshape(DEC_BATCH, PAST, D).astype(BF16),
                                   vb[N_P:].reshape(DEC_BATCH, DEC_SEQ, D)], axis=1)
            o_p = diff_attention(lam_init, lvecs, subln, q, k_p, v_p, 0, SEQ, SEQ)
            o_s = diff_attention(lam_init, lvecs, subln, q, k_s, v_s, N_P, DEC_SEQ, 256)
            x = proj_residual(jnp.concatenate([o_p, o_s], axis=0), diff_w_o[j].astype(BF16), x, mod)
        f = i // 2
        if i % 2 == 0:
            x = ffn_residual(x, gffn, mod, ffn_w_in[f].astype(BF16), ffn_w_out[f].astype(BF16))
        else:
            wr = jnp.concatenate([moe_w_router[f], jnp.zeros((D, LANES - N_EXPERTS), F32)], axis=1).astype(BF16)
            br = jnp.concatenate([moe_b_router[f], jnp.zeros((LANES - N_EXPERTS,), F32)]).reshape(1, LANES)
            x = moe_residual(x, gffn, mod, wr, br, moe_w_in[f].astype(BF16), moe_w_out[f].astype(BF16))
    y = final_norm(x, norm_final.reshape(1, D))
    return (y[:N_P].reshape(BATCH, SEQ, D), y[N_P:].reshape(DEC_BATCH, DEC_SEQ, D),
            jnp.stack(new_ckv, axis=1), jnp.stack(new_kr, axis=1),
            jnp.stack(new_s5_re, axis=1), jnp.stack(new_s5_im, axis=1),
            jnp.stack(new_dk, axis=1), jnp.stack(new_dv, axis=1))
```

```python
import functools
import math

import jax
import jax.numpy as jnp
from jax import lax
from jax.experimental import pallas as pl
from jax.experimental.pallas import tpu as pltpu

D = 1024
BATCH = 16
SEQ = 256
DEPTH = 4
DEC_BATCH = 2
DEC_SEQ = 4096
PAST = 512
GRID_W = 64
N_P = BATCH * SEQ
N_S = DEC_BATCH * DEC_SEQ
N_TOK = N_P + N_S
N_GROUPS = 1 + DEC_BATCH

MLA_HEADS = 8
MLA_Q_RANK = 384
MLA_KV_RANK = 256
MLA_D_NOPE = 128
MLA_D_ROPE = 64
MLA_D_V = 128
MLA_DK = MLA_D_NOPE + MLA_D_ROPE

S5_GROUP = 16
S5_GROUPS = D // S5_GROUP
S5_STATE = 64
S5_GB = 8
S5_NGB = S5_GROUPS // S5_GB
S5_HALF = S5_GB * S5_STATE
S5_NCH = S5_HALF // 128
S5_SUB = 8

DIFF_HEADS = 8
DIFF_DH = D // (2 * DIFF_HEADS)

D_FF = 2816
N_EXPERTS = 8
ROPE_THETA = 10000.0
EPS = 1e-6

TM = 512
FF_TILE = 1408
LANES = 128
VMEM_LIMIT = 56 * 1024 * 1024

F32 = jnp.float32
BF16 = jnp.bfloat16


def _cparams(*sem):
    return pltpu.CompilerParams(dimension_semantics=sem, vmem_limit_bytes=VMEM_LIMIT)


def _group_of_tile(i, tm):
    n_p = N_P // tm
    per = DEC_SEQ // tm
    return jnp.where(i < n_p, 0, 1 + (i - n_p) // per)


def _rope_tile(i, tm):
    n_p = N_P // tm
    per = DEC_SEQ // tm
    return jnp.where(i < n_p, 0, 1 + (i - n_p) % per)


def _rms(x, g):
    return x * lax.rsqrt(jnp.mean(x * x, axis=-1, keepdims=True) + EPS) * g


def _normmod(x, g, mod, k_shift, k_scale):
    return _rms(x, g) * (1.0 + mod[k_scale:k_scale + 1, :]) + mod[k_shift:k_shift + 1, :]


def _sigmoid(x):
    return 1.0 / (1.0 + jnp.exp(-x))


def _dot(a, b):
    return jnp.dot(a, b, preferred_element_type=F32)


def _dot_nt(a, b):
    return lax.dot_general(a, b, (((1,), (1,)), ((), ())), preferred_element_type=F32)


def _rope(x, cos, sin):
    lane = lax.broadcasted_iota(jnp.int32, x.shape, 1)
    nxt = pltpu.roll(x, LANES - 16, 1)
    prv = pltpu.roll(x, 16, 1)
    swapped = jnp.where((lane % 32) < 16, nxt, prv)
    return x * cos + swapped * sin


def _ada_kernel(c_ref, w_ref, b_ref, o_ref):
    c = c_ref[...]
    s = (c * _sigmoid(c)).astype(BF16)
    o_ref[...] = _dot(s, w_ref[...].astype(BF16)) + b_ref[...]


def ada_all(cond8, ada_w, ada_b):
    tn = 1536
    return pl.pallas_call(
        _ada_kernel,
        grid=(DEPTH, 6 * D // tn),
        in_specs=[pl.BlockSpec((8, D), lambda l, n: (0, 0)),
                  pl.BlockSpec((None, D, tn), lambda l, n: (l, 0, n)),
                  pl.BlockSpec((None, 1, tn), lambda l, n: (l, 0, n))],
        out_specs=pl.BlockSpec((None, 8, tn), lambda l, n: (l, 0, n)),
        out_shape=jax.ShapeDtypeStruct((DEPTH, 8, 6 * D), F32),
        compiler_params=_cparams("parallel", "parallel"),
        name="ada",
    )(cond8, ada_w, ada_b.reshape(DEPTH, 1, 6 * D))


def _mod_spec(tm):
    return pl.BlockSpec((None, 6, D), lambda i, *_: (_group_of_tile(i, tm), 0, 0))


def _mla_tok_kernel(x_ref, g_ref, mod_ref, w1_ref, qn_ref, wuq_ref, kvn_ref, cos_ref, sin_ref,
                    q_ref, ckv_ref, kr_ref):
    h = _normmod(x_ref[...], g_ref[...], mod_ref[...], 0, 1).astype(BF16)
    t1 = _dot(h, w1_ref[...])
    ql = _rms(t1[:, :MLA_Q_RANK], qn_ref[...]).astype(BF16)
    q = _dot(ql, wuq_ref[...])
    c0 = MLA_Q_RANK
    ckv_ref[...] = _rms(t1[:, c0:c0 + MLA_KV_RANK], kvn_ref[...])
    cos = cos_ref[...]
    sin = sin_ref[...]
    kr = _rope(t1[:, c0 + MLA_KV_RANK:c0 + MLA_KV_RANK + LANES], cos, sin)
    kr_ref[...] = kr[:, :MLA_D_ROPE]
    n_nope = MLA_HEADS * MLA_D_NOPE
    for pair in range(MLA_HEADS // 2):
        qr = _rope(q[:, n_nope + pair * LANES:n_nope + (pair + 1) * LANES], cos, sin).astype(BF16)
        for sub in range(2):
            hd = 2 * pair + sub
            q_ref[hd, :, 0:MLA_D_NOPE] = q[:, hd * MLA_D_NOPE:(hd + 1) * MLA_D_NOPE].astype(BF16)
            q_ref[hd, :, MLA_D_NOPE:MLA_DK] = qr[:, sub * MLA_D_ROPE:(sub + 1) * MLA_D_ROPE]


def mla_tokens(x, g, mod, w1, qn, wuq, kvn, cos_t, sin_t):
    nt = N_TOK // TM
    const = lambda shape: pl.BlockSpec(shape, lambda i: (0,) * len(shape))
    return pl.pallas_call(
        _mla_tok_kernel,
        grid=(nt,),
        in_specs=[pl.BlockSpec((TM, D), lambda i: (i, 0)), const((1, D)), _mod_spec(TM),
                  const(w1.shape), const((1, MLA_Q_RANK)), const(wuq.shape), const((1, MLA_KV_RANK)),
                  pl.BlockSpec((TM, LANES), lambda i: (_rope_tile(i, TM), 0)),
                  pl.BlockSpec((TM, LANES), lambda i: (_rope_tile(i, TM), 0))],
        out_specs=[pl.BlockSpec((MLA_HEADS, TM, MLA_DK), lambda i: (0, i, 0)),
                   pl.BlockSpec((TM, MLA_KV_RANK), lambda i: (i, 0)),
                   pl.BlockSpec((TM, MLA_D_ROPE), lambda i: (i, 0))],
        out_shape=[jax.ShapeDtypeStruct((MLA_HEADS, N_TOK, MLA_DK), BF16),
                   jax.ShapeDtypeStruct((N_TOK, MLA_KV_RANK), F32),
                   jax.ShapeDtypeStruct((N_TOK, MLA_D_ROPE), F32)],
        compiler_params=_cparams("parallel"),
        name="mla_tokens",
    )(x, g, mod, w1, qn, wuq, kvn, cos_t, sin_t)


def _mla_kv_kernel(ckv_ref, kr_ref, w_ref, k_ref, v_ref):
    kv = _dot(ckv_ref[...].astype(BF16), w_ref[...])
    kr = kr_ref[...].astype(BF16)
    n_nope = MLA_HEADS * MLA_D_NOPE
    for hd in range(MLA_HEADS):
        k_ref[hd, :, 0:MLA_D_NOPE] = kv[:, hd * MLA_D_NOPE:(hd + 1) * MLA_D_NOPE].astype(BF16)
        k_ref[hd, :, MLA_D_NOPE:MLA_DK] = kr
        v_ref[hd] = kv[:, n_nope + hd * MLA_D_V:n_nope + (hd + 1) * MLA_D_V].astype(BF16)


def mla_kv(ckv, kr, wukv, ts):
    nb, s, _ = ckv.shape
    return pl.pallas_call(
        _mla_kv_kernel,
        grid=(nb, s // ts),
        in_specs=[pl.BlockSpec((None, ts, MLA_KV_RANK), lambda b, t: (b, t, 0)),
                  pl.BlockSpec((None, ts, MLA_D_ROPE), lambda b, t: (b, t, 0)),
                  pl.BlockSpec(wukv.shape, lambda b, t: (0, 0))],
        out_specs=[pl.BlockSpec((None, MLA_HEADS, ts, MLA_DK), lambda b, t: (b, 0, t, 0)),
                   pl.BlockSpec((None, MLA_HEADS, ts, MLA_D_V), lambda b, t: (b, 0, t, 0))],
        out_shape=[jax.ShapeDtypeStruct((nb, MLA_HEADS, s, MLA_DK), BF16),
                   jax.ShapeDtypeStruct((nb, MLA_HEADS, s, MLA_D_V), BF16)],
        compiler_params=_cparams("parallel", "parallel"),
        name="mla_kv",
    )(ckv, kr, wukv)


def _mla_attn_kernel(q_ref, k_ref, v_ref, o_ref):
    s = _dot_nt(q_ref[...], k_ref[...]) * (MLA_DK ** -0.5)
    m = jnp.max(s, axis=-1, keepdims=True)
    p = jnp.exp(s - m)
    l = jnp.sum(p, axis=-1, keepdims=True)
    o_ref[...] = (_dot(p.astype(BF16), v_ref[...]) / l).astype(BF16)


def mla_attention(q3, k3, v3, row0, seq, tq):
    nb, _, s, _ = k3.shape
    nq = seq // tq
    base = row0 // tq
    return pl.pallas_call(
        _mla_attn_kernel,
        grid=(nb, MLA_HEADS, nq),
        in_specs=[pl.BlockSpec((None, tq, MLA_DK), lambda b, h, i: (h, base + b * nq + i, 0)),
                  pl.BlockSpec((None, None, s, MLA_DK), lambda b, h, i: (b, h, 0, 0)),
                  pl.BlockSpec((None, None, s, MLA_D_V), lambda b, h, i: (b, h, 0, 0))],
        out_specs=pl.BlockSpec((tq, MLA_D_V), lambda b, h, i: (b * nq + i, h)),
        out_shape=jax.ShapeDtypeStruct((nb * seq, MLA_HEADS * MLA_D_V), BF16),
        compiler_params=_cparams("parallel", "parallel", "parallel"),
        name="mla_attn_%d" % s,
    )(q3, k3, v3)


def _proj_res_kernel(op_ref, os_ref, w_ref, x_ref, mod_ref, out_ref):
    o = jnp.where(pl.program_id(0) < N_P // TM, op_ref[...], os_ref[...])
    out_ref[...] = x_ref[...] + mod_ref[2:3, :] * _dot(o, w_ref[...])


def proj_residual(o_p, o_s, w, x, mod):
    nt = N_TOK // TM
    n_p = N_P // TM
    return pl.pallas_call(
        _proj_res_kernel,
        grid=(nt,),
        in_specs=[pl.BlockSpec((TM, D), lambda i: (jnp.minimum(i, n_p - 1), 0)),
                  pl.BlockSpec((TM, D), lambda i: (jnp.maximum(i - n_p, 0), 0)),
                  pl.BlockSpec((D, D), lambda i: (0, 0)),
                  pl.BlockSpec((TM, D), lambda i: (i, 0)), _mod_spec(TM)],
        out_specs=pl.BlockSpec((TM, D), lambda i: (i, 0)),
        out_shape=jax.ShapeDtypeStruct((N_TOK, D), F32),
        compiler_params=_cparams("parallel"),
        name="proj_residual",
    )(o_p, o_s, w, x, mod)


def _diff_tok_kernel(x_ref, g_ref, mod_ref, w_ref, cos_ref, sin_ref, q_ref, kc_ref, vc_ref, k_ref, v_ref):
    h = _normmod(x_ref[...], g_ref[...], mod_ref[...], 0, 1).astype(BF16)
    cos = cos_ref[...]
    sin = sin_ref[...]
    is_ctx = pl.program_id(0) < N_P // TM
    for c in range(D // LANES):
        sl = slice(c * LANES, (c + 1) * LANES)
        q = _dot(h, w_ref[:, c * LANES:(c + 1) * LANES])
        k = _dot(h, w_ref[:, D + c * LANES:D + (c + 1) * LANES])
        v = _dot(h, w_ref[:, 2 * D + c * LANES:2 * D + (c + 1) * LANES])
        q_ref[:, sl] = _rope(q, cos, sin).astype(BF16)
        kc_ref[:, sl] = _rope(k, cos, sin).astype(BF16)
        vc_ref[:, sl] = v.astype(BF16)

        @pl.when(is_ctx)
        def _():
            k_ref[:, sl] = k
            v_ref[:, sl] = v


def diff_tokens(x, g, mod, wqkv, cos_t, sin_t):
    nt = N_TOK // TM
    n_p = N_P // TM
    row = lambda i: (i, 0)
    ctx_row = lambda i: (jnp.minimum(i, n_p - 1), 0)
    return pl.pallas_call(
        _diff_tok_kernel,
        grid=(nt,),
        in_specs=[pl.BlockSpec((TM, D), row), pl.BlockSpec((1, D), lambda i: (0, 0)), _mod_spec(TM),
                  pl.BlockSpec((D, 3 * D), lambda i: (0, 0)),
                  pl.BlockSpec((TM, LANES), lambda i: (_rope_tile(i, TM), 0)),
                  pl.BlockSpec((TM, LANES), lambda i: (_rope_tile(i, TM), 0))],
        out_specs=[pl.BlockSpec((TM, D), row)] * 3 + [pl.BlockSpec((TM, D), ctx_row)] * 2,
        out_shape=[jax.ShapeDtypeStruct((N_TOK, D), BF16)] * 3 + [jax.ShapeDtypeStruct((N_P, D), F32)] * 2,
        compiler_params=_cparams("arbitrary"),
        name="diff_tokens",
    )(x, g, mod, wqkv, cos_t, sin_t)


def _diff_attn_kernel(lam_init, lq1_ref, lk1_ref, lq2_ref, lk2_ref, sub_ref, q_ref, k_ref, v_ref, o_ref):
    lam = (jnp.exp(jnp.sum(lq1_ref[...] * lk1_ref[...], axis=-1, keepdims=True))
           - jnp.exp(jnp.sum(lq2_ref[...] * lk2_ref[...], axis=-1, keepdims=True)) + lam_init)
    q = q_ref[...]
    k = k_ref[...]
    lane = lax.broadcasted_iota(jnp.int32, q.shape, 1)
    zero = jnp.zeros_like(q)
    scale = DIFF_DH ** -0.5

    def probs(qh):
        s = _dot_nt(qh, k) * scale
        p = jnp.exp(s - jnp.max(s, axis=-1, keepdims=True))
        return p / jnp.sum(p, axis=-1, keepdims=True)

    p1 = probs(jnp.where(lane < DIFF_DH, q, zero))
    p2 = probs(jnp.where(lane >= DIFF_DH, q, zero))
    att = (p1 - lam * p2).astype(BF16)
    o = _dot(att, v_ref[...])
    o_ref[...] = (_rms(o, sub_ref[...]) * (1.0 - lam_init)).astype(BF16)


def diff_attention(lam_init, lvecs, subln, q, k, v, row0, seq, tq):
    nb, s, _ = k.shape
    nq = seq // tq
    base = row0 // tq
    vec = pl.BlockSpec((1, DIFF_DH), lambda b, h, i: (0, 0))
    return pl.pallas_call(
        functools.partial(_diff_attn_kernel, lam_init),
        grid=(nb, DIFF_HEADS, nq),
        in_specs=[vec, vec, vec, vec, pl.BlockSpec((1, 2 * DIFF_DH), lambda b, h, i: (0, 0)),
                  pl.BlockSpec((tq, LANES), lambda b, h, i: (base + b * nq + i, h)),
                  pl.BlockSpec((None, s, LANES), lambda b, h, i: (b, 0, h)),
                  pl.BlockSpec((None, s, LANES), lambda b, h, i: (b, 0, h))],
        out_specs=pl.BlockSpec((tq, LANES), lambda b, h, i: (b * nq + i, h)),
        out_shape=jax.ShapeDtypeStruct((nb * seq, D), BF16),
        compiler_params=_cparams("parallel", "parallel", "parallel"),
        name="diff_attn_%d" % s,
    )(*lvecs, subln, q, k, v)


def _normmod_kernel(x_ref, g_ref, mod_ref, h_ref):
    h_ref[...] = _normmod(x_ref[...], g_ref[...], mod_ref[...], 0, 1)


S5_T = SEQ
S5_SEG = DEC_SEQ // S5_SUB
S5_TILES_P = N_P // S5_T
S5_PER_SEG = S5_SEG // S5_T


def _s5_tile_pos(i):
    k = i - S5_TILES_P
    per_batch = S5_SUB * S5_PER_SEG
    lat_row = S5_TILES_P // S5_SUB + (k // per_batch) * S5_PER_SEG + k % S5_PER_SEG
    lat_col = (k % per_batch) // S5_PER_SEG
    is_p = i < S5_TILES_P
    return jnp.where(is_p, i // S5_SUB, lat_row), jnp.where(is_p, i % S5_SUB, lat_col)


def normmod_time_major(x, g, mod):
    nt = N_TOK // S5_T
    return pl.pallas_call(
        _normmod_kernel,
        grid=(nt,),
        in_specs=[pl.BlockSpec((S5_T, D), lambda i: (i, 0)), pl.BlockSpec((1, D), lambda i: (0, 0)),
                  _mod_spec(S5_T)],
        out_specs=pl.BlockSpec((S5_T, D), lambda i: _s5_tile_pos(i)),
        out_shape=jax.ShapeDtypeStruct((N_TOK // S5_SUB, S5_SUB * D), F32),
        compiler_params=_cparams("parallel"),
        name="normmod",
    )(x, g, mod)


def _s5_kernel(chained, n, u_ref, wb_ref, wc_ref, a_ref, an_ref, dsk_ref, h0_ref, y_ref, fin_ref,
               bu_ref, ini_ref):
    d = pl.program_id(2)
    rows = S5_SUB * n
    chunk = 512
    nch = S5_NCH
    for r in range(rows // chunk):
        rs = slice(r * chunk, (r + 1) * chunk)
        bu = _dot(u_ref[rs, :].astype(BF16), wb_ref[...])
        for c in range(2 * nch):
            bu_ref[c, rs, :] = bu[:, c * LANES:(c + 1) * LANES]
    ar = [jnp.broadcast_to(a_ref[0:1, c * LANES:(c + 1) * LANES], (S5_SUB, LANES)) for c in range(nch)]
    ai = [jnp.broadcast_to(a_ref[1:2, c * LANES:(c + 1) * LANES], (S5_SUB, LANES)) for c in range(nch)]

    def step_index(s):
        return jnp.where(d == 0, s, n - 1 - s)

    def step_rows(t):
        return pl.ds(pl.multiple_of(t * S5_SUB, S5_SUB), S5_SUB)

    def advance(h, t):
        out = [None] * (2 * nch)
        for c in range(nch):
            br = bu_ref[c, step_rows(t), :]
            bi = bu_ref[nch + c, step_rows(t), :]
            out[c] = ar[c] * h[c] - ai[c] * h[nch + c] + br
            out[nch + c] = ar[c] * h[nch + c] + ai[c] * h[c] + bi
        return out

    unroll = 4
    zeros = [jnp.zeros((S5_SUB, LANES), F32) for _ in range(2 * nch)]

    if chained:
        def local_body(s, h):
            h = list(h)
            for k in range(unroll):
                h = advance(h, step_index(s * unroll + k))
            return tuple(h)

        ends = lax.fori_loop(0, n // unroll, local_body, tuple(zeros))
        sub_row = lax.broadcasted_iota(jnp.int32, (S5_SUB, LANES), 0)
        cur = [h0_ref[:, c * LANES:(c + 1) * LANES] for c in range(2 * nch)]
        anr = [an_ref[0:1, c * LANES:(c + 1) * LANES] for c in range(nch)]
        ani = [an_ref[1:2, c * LANES:(c + 1) * LANES] for c in range(nch)]
        for kk in range(S5_SUB):
            j = jnp.where(d == 0, kk, S5_SUB - 1 - kk)
            nxt = [None] * (2 * nch)
            for c in range(2 * nch):
                ini_ref[c, pl.ds(j, 1), :] = cur[c]
            for c in range(nch):
                er = jnp.sum(jnp.where(sub_row == j, ends[c], 0.0), axis=0, keepdims=True)
                ei = jnp.sum(jnp.where(sub_row == j, ends[nch + c], 0.0), axis=0, keepdims=True)
                nxt[c] = er + anr[c] * cur[c] - ani[c] * cur[nch + c]
                nxt[nch + c] = ei + anr[c] * cur[nch + c] + ani[c] * cur[c]
            cur = nxt
        h_init = [ini_ref[c] for c in range(2 * nch)]
    else:
        h_init = zeros

    def body(s, h):
        h = list(h)
        for k in range(unroll):
            t = step_index(s * unroll + k)
            h = advance(h, t)
            for c in range(2 * nch):
                bu_ref[c, step_rows(t), :] = h[c]
        return tuple(h)

    fin = lax.fori_loop(0, n // unroll, body, tuple(h_init))
    for c in range(2 * nch):
        fin_ref[:, c * LANES:(c + 1) * LANES] = fin[c]

    @pl.when(d == 0)
    def _():
        y_ref[...] = dsk_ref[...] * u_ref[...]

    for r in range(rows // chunk):
        rs = slice(r * chunk, (r + 1) * chunk)
        hs = jnp.concatenate([bu_ref[c, rs, :].astype(BF16) for c in range(2 * nch)], axis=1)
        y_ref[rs, :] += _dot(hs, wc_ref[...])


def s5_scan(h, wb, wc, a, an, dskip, h0, row0, n_blocks, n, chained):
    rows = S5_SUB * n
    base = row0 // rows
    kern = functools.partial(_s5_kernel, chained, n)
    return pl.pallas_call(
        kern,
        grid=(n_blocks, S5_NGB, 2),
        in_specs=[pl.BlockSpec((rows, LANES), lambda r, c, d: (base + r, c)),
                  pl.BlockSpec((None, None, LANES, 2 * S5_HALF), lambda r, c, d: (d, c, 0, 0)),
                  pl.BlockSpec((None, None, 2 * S5_HALF, LANES), lambda r, c, d: (d, c, 0, 0)),
                  pl.BlockSpec((None, None, 2, S5_HALF), lambda r, c, d: (d, c, 0, 0)),
                  pl.BlockSpec((None, None, 2, S5_HALF), lambda r, c, d: (d, c, 0, 0)),
                  pl.BlockSpec((1, LANES), lambda r, c, d: (0, c)),
                  pl.BlockSpec((None, None, None, 1, 2 * S5_HALF), lambda r, c, d: (r, d, c, 0, 0))],
        out_specs=[pl.BlockSpec((rows, LANES), lambda r, c, d: (r, c)),
                   pl.BlockSpec((None, None, None, S5_SUB, 2 * S5_HALF), lambda r, c, d: (r, d, c, 0, 0))],
        out_shape=[jax.ShapeDtypeStruct((n_blocks * rows, D), F32),
                   jax.ShapeDtypeStruct((n_blocks, 2, S5_NGB, S5_SUB, 2 * S5_HALF), F32)],
        scratch_shapes=[pltpu.VMEM((2 * S5_NCH, rows, LANES), F32),
                        pltpu.VMEM((2 * S5_NCH, S5_SUB, LANES), F32)],
        compiler_params=_cparams("parallel", "parallel", "arbitrary"),
        name="s5_scan_%d" % n,
    )(h, wb, wc, a, an, dskip, h0)


def _glu_res_kernel(yp_ref, ys_ref, w_ref, x_ref, mod_ref, out_ref):
    y = jnp.where(pl.program_id(0) < S5_TILES_P, yp_ref[...], ys_ref[...])
    g = 0.5 * y * (1.0 + jnp.tanh(math.sqrt(2.0 / math.pi) * (y + 0.044715 * (y * y * y))))
    t = _dot(g.astype(BF16), w_ref[...])
    out_ref[...] = x_ref[...] + mod_ref[2:3, :] * (t[:, :D] * _sigmoid(t[:, D:]))


def glu_residual(y_p, y_s, w, x, mod):
    nt = N_TOK // S5_T
    n_blk_p = S5_TILES_P // S5_SUB

    def yp_map(i):
        r, c = _s5_tile_pos(jnp.minimum(i, S5_TILES_P - 1))
        return r, c

    def ys_map(i):
        r, c = _s5_tile_pos(jnp.maximum(i, S5_TILES_P))
        return r - n_blk_p, c

    return pl.pallas_call(
        _glu_res_kernel,
        grid=(nt,),
        in_specs=[pl.BlockSpec((S5_T, D), yp_map), pl.BlockSpec((S5_T, D), ys_map),
                  pl.BlockSpec((D, 2 * D), lambda i: (0, 0)),
                  pl.BlockSpec((S5_T, D), lambda i: (i, 0)), _mod_spec(S5_T)],
        out_specs=pl.BlockSpec((S5_T, D), lambda i: (i, 0)),
        out_shape=jax.ShapeDtypeStruct((N_TOK, D), F32),
        compiler_params=_cparams("parallel"),
        name="glu_residual",
    )(y_p.reshape(N_P // S5_SUB, S5_SUB * D), y_s.reshape(N_S // S5_SUB, S5_SUB * D), w, x, mod)


def _ffn_kernel(x_ref, g_ref, mod_ref, wa_ref, wb_ref, wo_ref, out_ref, h_scr, acc_scr):
    f = pl.program_id(1)

    @pl.when(f == 0)
    def _():
        h_scr[...] = _normmod(x_ref[...], g_ref[...], mod_ref[...], 3, 4).astype(BF16)
        acc_scr[...] = jnp.zeros_like(acc_scr)

    h = h_scr[...]
    a = _dot(h, wa_ref[...])
    b = _dot(h, wb_ref[...])
    act = (a * _sigmoid(a) * b).astype(BF16)
    acc_scr[...] += _dot(act, wo_ref[...])

    @pl.when(f == pl.num_programs(1) - 1)
    def _():
        out_ref[...] = x_ref[...] + mod_ref[5:6, :] * acc_scr[...]


def ffn_residual(x, g, mod, w_in, w_out):
    nt = N_TOK // TM
    nf = D_FF // FF_TILE
    return pl.pallas_call(
        _ffn_kernel,
        grid=(nt, nf),
        in_specs=[pl.BlockSpec((TM, D), lambda i, f: (i, 0)), pl.BlockSpec((1, D), lambda i, f: (0, 0)),
                  _mod_spec(TM),
                  pl.BlockSpec((D, FF_TILE), lambda i, f: (0, f)),
                  pl.BlockSpec((D, FF_TILE), lambda i, f: (0, f + nf)),
                  pl.BlockSpec((FF_TILE, D), lambda i, f: (f, 0))],
        out_specs=pl.BlockSpec((TM, D), lambda i, f: (i, 0)),
        out_shape=jax.ShapeDtypeStruct((N_TOK, D), F32),
        scratch_shapes=[pltpu.VMEM((TM, D), BF16), pltpu.VMEM((TM, D), F32)],
        compiler_params=_cparams("parallel", "arbitrary"),
        name="ffn",
    )(x, g, mod, w_in, w_in, w_out)


TS = 512
N_SLOTS = 2 * N_TOK + N_EXPERTS * TS
NT_S = N_SLOTS // TS
PIECE = 256
MAX_PIECES = (TM + 16 + PIECE - 1) // PIECE


def _moe_route_kernel(x_ref, g_ref, mod_ref, wr_ref, br_ref, h_ref, route_ref, rows_ref, run_ref, carry_scr):
    @pl.when(pl.program_id(0) == 0)
    def _():
        carry_scr[...] = jnp.zeros_like(carry_scr)

    h = _normmod(x_ref[...], g_ref[...], mod_ref[...], 3, 4).astype(BF16)
    h_ref[...] = h
    logits = _dot(h, wr_ref[...]) + br_ref[...]
    lane = lax.broadcasted_iota(jnp.int32, logits.shape, 1)
    neg = jnp.float32(-jnp.inf)
    lg = jnp.where(lane < N_EXPERTS, logits, neg)
    v1 = jnp.max(lg, axis=-1, keepdims=True)
    i1 = jnp.min(jnp.where(lg == v1, lane, LANES), axis=-1, keepdims=True)
    lg2 = jnp.where(lane == i1, neg, lg)
    v2 = jnp.max(lg2, axis=-1, keepdims=True)
    i2 = jnp.min(jnp.where(lg2 == v2, lane, LANES), axis=-1, keepdims=True)
    e2 = jnp.exp(v2 - v1)
    g1 = 1.0 / (1.0 + e2)
    g2 = e2 / (1.0 + e2)
    oh1 = lane == i1
    oh2 = lane == i2
    sel = jnp.where(oh1 | oh2, 1.0, 0.0)
    r = lax.broadcasted_iota(jnp.int32, (TM, TM), 0)
    c = lax.broadcasted_iota(jnp.int32, (TM, TM), 1)
    tri = jnp.where(c < r, 1.0, 0.0).astype(BF16)
    rank = _dot(tri, sel.astype(BF16)) + carry_scr[0:1, :]
    r1 = jnp.sum(jnp.where(oh1, rank, 0.0), axis=-1, keepdims=True)
    r2 = jnp.sum(jnp.where(oh2, rank, 0.0), axis=-1, keepdims=True)
    cols = (i1.astype(F32), i2.astype(F32), g1, g2, r1, r2)
    route = jnp.zeros(logits.shape, F32)
    for k, v in enumerate(cols):
        route = jnp.where(lane == k, v, route)
    route_ref[...] = route
    rows_ref[...] = route.T[0:8, :]
    carry_scr[...] = carry_scr[...] + jnp.sum(sel, axis=0, keepdims=True)
    run_ref[...] = carry_scr[...]


def moe_route(x, g, mod, w_router, b_router):
    nt = N_TOK // TM
    return pl.pallas_call(
        _moe_route_kernel,
        grid=(nt,),
        in_specs=[pl.BlockSpec((TM, D), lambda i: (i, 0)), pl.BlockSpec((1, D), lambda i: (0, 0)), _mod_spec(TM),
                  pl.BlockSpec((D, LANES), lambda i: (0, 0)), pl.BlockSpec((1, LANES), lambda i: (0, 0))],
        out_specs=[pl.BlockSpec((TM, D), lambda i: (i, 0)), pl.BlockSpec((TM, LANES), lambda i: (i, 0)),
                   pl.BlockSpec((8, TM), lambda i: (0, i)),
                   pl.BlockSpec((None, 8, LANES), lambda i: (i, 0, 0))],
        out_shape=[jax.ShapeDtypeStruct((N_TOK, D), BF16), jax.ShapeDtypeStruct((N_TOK, LANES), F32),
                   jax.ShapeDtypeStruct((8, N_TOK), F32),
                   jax.ShapeDtypeStruct((nt, 8, LANES), F32)],
        scratch_shapes=[pltpu.VMEM((8, LANES), F32)],
        compiler_params=_cparams("arbitrary"),
        name="moe_route",
    )(x, g, mod, w_router, b_router)


def _moe_gather_kernel(used_ref, clo_ref, chi_ref, pos_ref, gates_ref, h_ref, xs_ref, gate_ref, gat_scr, gsum_scr):
    i = pl.program_id(0)
    gat_scr[...] = jnp.zeros_like(gat_scr)
    gsum_scr[...] = jnp.zeros_like(gsum_scr)

    @pl.when(used_ref[i] > 0)
    def _():
        slot = i * TS + lax.broadcasted_iota(jnp.int32, (TS, TM), 0)

        def body(c, carry):
            base = pl.multiple_of(c * TM, TM)
            m1 = pos_ref[0:1, pl.ds(base, TM)] == slot
            m2 = pos_ref[1:2, pl.ds(base, TM)] == slot
            pick = jnp.where(m1 | m2, 1.0, 0.0).astype(BF16)
            gat_scr[...] += _dot(pick, h_ref[pl.ds(base, TM), :])
            g = jnp.where(m1, gates_ref[0:1, pl.ds(base, TM)], 0.0) + jnp.where(m2, gates_ref[1:2, pl.ds(base, TM)], 0.0)
            gsum_scr[...] += jnp.sum(g, axis=-1, keepdims=True)
            return carry

        lax.fori_loop(clo_ref[i], chi_ref[i] + 1, body, 0)

    xs_ref[...] = gat_scr[...].astype(BF16)
    gate_ref[...] = gsum_scr[...]


def moe_gather(used, clo, chi, pos_rows, gate_rows, h_b):
    whole = lambda shape: pl.BlockSpec(shape, lambda i, *_: (0, 0), pipeline_mode=pl.Buffered(1))
    grid_spec = pltpu.PrefetchScalarGridSpec(
        num_scalar_prefetch=3,
        grid=(NT_S,),
        in_specs=[whole((8, N_TOK)), whole((8, N_TOK)), whole((N_TOK, D))],
        out_specs=[pl.BlockSpec((TS, D), lambda i, *_: (i, 0)), pl.BlockSpec((TS, 1), lambda i, *_: (i, 0))],
        scratch_shapes=[pltpu.VMEM((TS, D), F32), pltpu.VMEM((TS, 1), F32)],
    )
    return pl.pallas_call(
        _moe_gather_kernel,
        grid_spec=grid_spec,
        out_shape=[jax.ShapeDtypeStruct((N_SLOTS, D), BF16), jax.ShapeDtypeStruct((N_SLOTS, 1), F32)],
        compiler_params=_cparams("arbitrary"),
        name="moe_gather",
    )(used, clo, chi, pos_rows, gate_rows, h_b)


def _moe_ffn_kernel(texp_ref, used_ref, xs_ref, gate_ref, wa_ref, wb_ref, wo_ref, ys_ref, acc_scr):
    i = pl.program_id(0)
    f = pl.program_id(1)
    live = used_ref[i] > 0

    @pl.when(f == 0)
    def _():
        acc_scr[...] = jnp.zeros_like(acc_scr)

    @pl.when(live)
    def _():
        xs = xs_ref[...]
        a = _dot(xs, wa_ref[...])
        b = _dot(xs, wb_ref[...])
        act = (a * _sigmoid(a) * b).astype(BF16)
        acc_scr[...] += _dot(act, wo_ref[...])

    @pl.when(f == pl.num_programs(1) - 1)
    def _():
        ys_ref[...] = (gate_ref[...] * acc_scr[...]).astype(BF16)


def moe_ffn(texp, used, xs, gate_col, w_in, w_out):
    nf = D_FF // FF_TILE
    grid_spec = pltpu.PrefetchScalarGridSpec(
        num_scalar_prefetch=2,
        grid=(NT_S, nf),
        in_specs=[pl.BlockSpec((TS, D), lambda i, f, *_: (i, 0)),
                  pl.BlockSpec((TS, 1), lambda i, f, *_: (i, 0)),
                  pl.BlockSpec((None, D, FF_TILE), lambda i, f, texp, *_: (texp[i], 0, f)),
                  pl.BlockSpec((None, D, FF_TILE), lambda i, f, texp, *_: (texp[i], 0, f + nf)),
                  pl.BlockSpec((None, FF_TILE, D), lambda i, f, texp, *_: (texp[i], f, 0))],
        out_specs=pl.BlockSpec((TS, D), lambda i, f, *_: (i, 0)),
        scratch_shapes=[pltpu.VMEM((TS, D), F32)],
    )
    return pl.pallas_call(
        _moe_ffn_kernel,
        grid_spec=grid_spec,
        out_shape=jax.ShapeDtypeStruct((N_SLOTS, D), BF16),
        compiler_params=_cparams("parallel", "arbitrary"),
        name="moe_ffn",
    )(texp, used, xs, gate_col, w_in, w_in, w_out)


def _moe_combine_kernel(start_ref, npc_ref, lo_ref, hi_ref, off_ref, x_ref, mod_ref, route_ref, ys_hbm, out_ref,
                        buf, acc_scr, sem):
    i = pl.program_id(0)

    def piece_copy(e, k):
        s = pl.multiple_of(start_ref[i * N_EXPERTS + e] + k * PIECE, 16)
        return pltpu.make_async_copy(ys_hbm.at[pl.ds(s, PIECE), :], buf.at[e, pl.ds(k * PIECE, PIECE), :],
                                     sem.at[e, k])

    for e in range(N_EXPERTS):
        for k in range(MAX_PIECES):
            @pl.when(k < npc_ref[i * N_EXPERTS + e])
            def _():
                piece_copy(e, k).start()

    acc_scr[...] = jnp.zeros_like(acc_scr)
    route = route_ref[...]
    e1 = route[:, 0:1].astype(jnp.int32)
    e2 = route[:, 1:2].astype(jnp.int32)
    pos1 = route[:, 4:5].astype(jnp.int32)
    pos2 = route[:, 5:6].astype(jnp.int32)
    for e in range(N_EXPERTS):
        pos1 = pos1 + jnp.where(e1 == e, off_ref[e], 0)
        pos2 = pos2 + jnp.where(e2 == e, off_ref[e], 0)
    lane = lax.broadcasted_iota(jnp.int32, (TM, PIECE), 1)
    for e in range(N_EXPERTS):
        lo = lo_ref[i * N_EXPERTS + e]
        hi = hi_ref[i * N_EXPERTS + e]
        p1 = jnp.where((pos1 >= lo) & (pos1 < hi), pos1, -1)
        p2 = jnp.where((pos2 >= lo) & (pos2 < hi), pos2, -1)
        for k in range(MAX_PIECES):
            @pl.when(k < npc_ref[i * N_EXPERTS + e])
            def _():
                piece_copy(e, k).wait()
                base = start_ref[i * N_EXPERTS + e] + k * PIECE
                pick = jnp.where((p1 - base == lane) | (p2 - base == lane), 1.0, 0.0).astype(BF16)
                acc_scr[...] += _dot(pick, buf[e, k * PIECE:(k + 1) * PIECE, :])

    out_ref[...] = x_ref[...] + mod_ref[5:6, :] * acc_scr[...]


def moe_combine(start, npc, lo, hi, off, x, mod, route, ys):
    nt = N_TOK // TM
    grid_spec = pltpu.PrefetchScalarGridSpec(
        num_scalar_prefetch=5,
        grid=(nt,),
        in_specs=[pl.BlockSpec((TM, D), lambda i, *_: (i, 0)),
                  pl.BlockSpec((None, 6, D), lambda i, *_: (_group_of_tile(i, TM), 0, 0)),
                  pl.BlockSpec((TM, LANES), lambda i, *_: (i, 0)),
                  pl.BlockSpec(memory_space=pl.ANY)],
        out_specs=pl.BlockSpec((TM, D), lambda i, *_: (i, 0)),
        scratch_shapes=[pltpu.VMEM((N_EXPERTS, MAX_PIECES * PIECE, D), BF16), pltpu.VMEM((TM, D), F32),
                        pltpu.SemaphoreType.DMA((N_EXPERTS, MAX_PIECES))],
    )
    return pl.pallas_call(
        _moe_combine_kernel,
        grid_spec=grid_spec,
        out_shape=jax.ShapeDtypeStruct((N_TOK, D), F32),
        compiler_params=_cparams("arbitrary"),
        name="moe_combine",
    )(start, npc, lo, hi, off, x, mod, route, ys)


def moe_residual(x, g, mod, w_router, b_router, w_in, w_out):
    nt = N_TOK // TM
    i32 = jnp.int32
    h_b, route, rows, run = moe_route(x, g, mod, w_router, b_router)
    run = run[:, 0, :N_EXPERTS].astype(i32)
    run_prev = jnp.concatenate([jnp.zeros((1, N_EXPERTS), i32), run[:-1]], axis=0)
    total = run[-1]
    padded = (total + TS - 1) // TS * TS
    off_end = jnp.cumsum(padded)
    off = off_end - padded
    experts = jnp.arange(N_EXPERTS, dtype=i32)
    off_of = lambda e_row: jnp.sum(jnp.where(e_row[None, :].astype(i32) == experts[:, None], off[:, None], 0), axis=0)
    pos1 = rows[4].astype(i32) + off_of(rows[0])
    pos2 = rows[5].astype(i32) + off_of(rows[1])
    pos_rows = jnp.concatenate([pos1[None], pos2[None], jnp.full((6, N_TOK), -1, i32)], axis=0)
    gate_rows = jnp.concatenate([rows[2:4], jnp.zeros((6, N_TOK), F32)], axis=0)
    tile_start = jnp.arange(NT_S, dtype=i32) * TS
    used = (tile_start < off_end[-1]).astype(i32)
    texp = jnp.minimum(jnp.sum((tile_start[:, None] >= off_end[None, :]).astype(i32), axis=1), N_EXPERTS - 1)
    mine = texp[:, None] == experts[None, :]
    of_tile = lambda per_e: jnp.sum(jnp.where(mine[:, None, :], per_e[None], 0), axis=2)
    rank0 = tile_start - jnp.sum(jnp.where(mine, off[None, :], 0), axis=1)
    rank1 = jnp.minimum(rank0 + TS, jnp.sum(jnp.where(mine, total[None, :], 0), axis=1))
    clo = jnp.sum((of_tile(run) <= rank0[:, None]).astype(i32), axis=1)
    chi = jnp.sum((of_tile(run_prev) < rank1[:, None]).astype(i32), axis=1) - 1
    clo = jnp.where(used > 0, jnp.minimum(clo, nt - 1), 0).astype(i32)
    chi = jnp.where(used > 0, chi, -1).astype(i32)
    xs, gate_col = moe_gather(used, clo, chi, pos_rows, gate_rows, h_b)
    ys = moe_ffn(texp, used, xs, gate_col, w_in, w_out)
    lo = off[None, :] + run_prev
    hi = off[None, :] + run
    start = jnp.minimum(lo // 16 * 16, N_SLOTS - MAX_PIECES * PIECE)
    npc = jnp.where(hi > lo, (hi - start + PIECE - 1) // PIECE, 0)
    flat = lambda a: a.reshape(nt * N_EXPERTS).astype(i32)
    return moe_combine(flat(start), flat(npc), flat(lo), flat(hi), off.astype(i32), x, mod, route, ys)


def _final_norm_kernel(x_ref, g_ref, o_ref):
    o_ref[...] = _rms(x_ref[...], g_ref[...])


def final_norm(x, g, row0, n_rows):
    base = row0 // TM
    return pl.pallas_call(
        _final_norm_kernel,
        grid=(n_rows // TM,),
        in_specs=[pl.BlockSpec((TM, D), lambda i: (base + i, 0)), pl.BlockSpec((1, D), lambda i: (0, 0))],
        out_specs=pl.BlockSpec((TM, D), lambda i: (i, 0)),
        out_shape=jax.ShapeDtypeStruct((n_rows, D), F32),
        compiler_params=_cparams("parallel"),
        name="final_norm",
    )(x, g)


def _rope_tables():
    half = 16
    freqs = ROPE_THETA ** (-jnp.arange(half, dtype=F32) / half)
    t = jnp.arange(DEC_SEQ, dtype=jnp.int32)
    row = (t // GRID_W).astype(F32)[:, None] * freqs[None, :]
    col = (t % GRID_W).astype(F32)[:, None] * freqs[None, :]
    cos = jnp.concatenate([jnp.cos(row), jnp.cos(row), jnp.cos(col), jnp.cos(col)], axis=1)
    sin = jnp.concatenate([-jnp.sin(row), jnp.sin(row), -jnp.sin(col), jnp.sin(col)], axis=1)
    cos = jnp.concatenate([jnp.ones((TM, 64), F32), cos], axis=0)
    sin = jnp.concatenate([jnp.zeros((TM, 64), F32), sin], axis=0)
    return jnp.tile(cos, (1, 2)), jnp.tile(sin, (1, 2))


def _mla_weights(w_dq, w_uq, w_dkv, w_ukv, w_o):
    w1 = jnp.concatenate([w_dq, w_dkv, jnp.zeros((D, LANES - MLA_D_ROPE), F32)], axis=1).astype(BF16)
    uq = w_uq.reshape(MLA_Q_RANK, MLA_HEADS, MLA_DK)
    wuq = jnp.concatenate([uq[:, :, :MLA_D_NOPE].reshape(MLA_Q_RANK, -1),
                           uq[:, :, MLA_D_NOPE:].reshape(MLA_Q_RANK, -1)], axis=1).astype(BF16)
    ukv = w_ukv.reshape(MLA_KV_RANK, MLA_HEADS, MLA_D_NOPE + MLA_D_V)
    wukv = jnp.concatenate([ukv[:, :, :MLA_D_NOPE].reshape(MLA_KV_RANK, -1),
                            ukv[:, :, MLA_D_NOPE:].reshape(MLA_KV_RANK, -1)], axis=1).astype(BF16)
    return w1, wuq, wukv, w_o.astype(BF16)


def _s5_weights(a_re, a_im, log_dt, b_re, b_im, c_re, c_im, seg_len):
    dt = jnp.exp(log_dt)[..., None]
    mag = jnp.exp(a_re * dt)
    abar_re, abar_im = mag * jnp.cos(a_im * dt), mag * jnp.sin(a_im * dt)
    mag_n = jnp.exp(a_re * dt * seg_len)
    apow_re, apow_im = mag_n * jnp.cos(a_im * dt * seg_len), mag_n * jnp.sin(a_im * dt * seg_len)
    den = a_re * a_re + a_im * a_im
    coef_re = ((abar_re - 1.0) * a_re + abar_im * a_im) / den
    coef_im = (abar_im * a_re - (abar_re - 1.0) * a_im) / den
    bbar_re = coef_re[..., None] * b_re - coef_im[..., None] * b_im
    bbar_im = coef_re[..., None] * b_im + coef_im[..., None] * b_re
    eye = jnp.eye(S5_GB, dtype=F32)

    def in_block(m):
        m = m.reshape(2, S5_NGB, S5_GB, S5_STATE, S5_GROUP)
        return jnp.einsum('dbgpc,gh->dbgchp', m, eye).reshape(2, S5_NGB, LANES, S5_HALF)

    def out_block(m):
        m = m.reshape(2, S5_NGB, S5_GB, S5_GROUP, S5_STATE)
        return jnp.einsum('dbgcp,gh->dbgphc', m, eye).reshape(2, S5_NGB, S5_HALF, LANES)

    wb = jnp.concatenate([in_block(bbar_re), in_block(bbar_im)], axis=3).astype(BF16)
    wc = jnp.concatenate([out_block(c_re), out_block(-c_im)], axis=2).astype(BF16)
    lanes = lambda m: m.reshape(2, S5_NGB, 1, S5_HALF)
    a = jnp.concatenate([lanes(abar_re), lanes(abar_im)], axis=2)
    an = jnp.concatenate([lanes(apow_re), lanes(apow_im)], axis=2)
    return wb, wc, a, an


def kernel(x_prompt, x_sample, c, c_ctx, cache_mla_ckv, cache_mla_krope, state_s5_re, state_s5_im, cache_diff_k, cache_diff_v, ada_w, ada_b, norm_mix, norm_ffn, norm_final, mla_w_dq, mla_q_norm, mla_w_uq, mla_w_dkv, mla_kv_norm, mla_w_ukv, mla_w_o, s5_a_re, s5_a_im, s5_log_dt, s5_b_re, s5_b_im, s5_c_re, s5_c_im, s5_d, s5_w_glu, diff_w_qkv, diff_lq1, diff_lk1, diff_lq2, diff_lk2, diff_subln, diff_w_o, ffn_w_in, ffn_w_out, moe_w_router, moe_b_router, moe_w_in, moe_w_out):
    x = jnp.concatenate([x_prompt.reshape(N_P, D), x_sample.reshape(N_S, D)], axis=0)
    cond8 = jnp.concatenate([c_ctx[None], c, jnp.zeros((8 - N_GROUPS, D), F32)], axis=0)
    mods = ada_all(cond8, ada_w, ada_b).reshape(DEPTH, 8, 6, D)[:, :N_GROUPS]
    cos_t, sin_t = _rope_tables()

    new_ckv, new_kr, new_s5_re, new_s5_im, new_dk, new_dv = [], [], [], [], [], []
    for i in range(DEPTH):
        mod = mods[i]
        gmix = norm_mix[i].reshape(1, D)
        gffn = norm_ffn[i].reshape(1, D)
        j = i // 3
        kind = i % 3
        if kind == 0:
            w1, wuq, wukv, wo = _mla_weights(mla_w_dq[j], mla_w_uq[j], mla_w_dkv[j], mla_w_ukv[j], mla_w_o[j])
            q3, ckv, kr = mla_tokens(x, gmix, mod, w1, mla_q_norm[j].reshape(1, -1), wuq,
                                     mla_kv_norm[j].reshape(1, -1), cos_t, sin_t)
            ckv_p = ckv[:N_P].reshape(BATCH, SEQ, MLA_KV_RANK)
            kr_p = kr[:N_P].reshape(BATCH, SEQ, MLA_D_ROPE)
            new_ckv.append(ckv_p)
            new_kr.append(kr_p)
            ckv_s = jnp.concatenate([cache_mla_ckv[:, j], ckv[N_P:].reshape(DEC_BATCH, DEC_SEQ, -1)], axis=1)
            kr_s = jnp.concatenate([cache_mla_krope[:, j], kr[N_P:].reshape(DEC_BATCH, DEC_SEQ, -1)], axis=1)
            k3p, v3p = mla_kv(ckv_p, kr_p, wukv, SEQ)
            k3s, v3s = mla_kv(ckv_s, kr_s, wukv, 512)
            o_p = mla_attention(q3, k3p, v3p, 0, SEQ, SEQ)
            o_s = mla_attention(q3, k3s, v3s, N_P, DEC_SEQ, 512)
            x = proj_residual(o_p, o_s, wo, x, mod)
        elif kind == 1:
            h = normmod_time_major(x, gmix, mod).reshape(N_TOK, D)
            dsk = s5_d[j].reshape(1, D)
            seg = S5_SEG
            wb, wc, a, an = _s5_weights(s5_a_re[j], s5_a_im[j], s5_log_dt[j], s5_b_re[j], s5_b_im[j],
                                        s5_c_re[j], s5_c_im[j], seg)
            zero_h0 = jnp.zeros((BATCH // S5_SUB, 2, S5_NGB, 1, 2 * S5_HALF), F32)
            y_p, fin = s5_scan(h, wb, wc, a, an, dsk, zero_h0, 0, BATCH // S5_SUB, SEQ, False)
            fin = fin.reshape(BATCH // S5_SUB, 2, S5_NGB, S5_SUB, 2, S5_GB, S5_STATE)
            fin = jnp.transpose(fin, (0, 3, 1, 4, 2, 5, 6)).reshape(BATCH, 2, 2, S5_GROUPS, S5_STATE)
            new_s5_re.append(fin[:, :, 0])
            new_s5_im.append(fin[:, :, 1])
            h0 = jnp.stack([state_s5_re[:, j], state_s5_im[:, j]], axis=2)
            h0 = h0.reshape(DEC_BATCH, 2, 2, S5_NGB, S5_HALF)
            h0 = jnp.transpose(h0, (0, 1, 3, 2, 4)).reshape(DEC_BATCH, 2, S5_NGB, 1, 2 * S5_HALF)
            y_s, _ = s5_scan(h, wb, wc, a, an, dsk, h0, N_P, DEC_BATCH, seg, True)
            x = glu_residual(y_p, y_s, s5_w_glu[j].astype(BF16), x, mod)
        else:
            lam_init = 0.8 - 0.6 * math.exp(-0.3 * i)
            q, kc, vb, k, v = diff_tokens(x, gmix, mod, diff_w_qkv[j].astype(BF16), cos_t, sin_t)
            new_dk.append(k.reshape(BATCH, SEQ, 2 * DIFF_HEADS, DIFF_DH))
            new_dv.append(v.reshape(BATCH, SEQ, DIFF_HEADS, 2 * DIFF_DH))
            lvecs = [a_.reshape(1, DIFF_DH) for a_ in (diff_lq1[j], diff_lk1[j], diff_lq2[j], diff_lk2[j])]
            subln = diff_subln[j].reshape(1, 2 * DIFF_DH)
            k_p = kc[:N_P].reshape(BATCH, SEQ, D)
            v_p = vb[:N_P].reshape(BATCH, SEQ, D)
            k_s = jnp.concatenate([cache_diff_k[:, j].reshape(DEC_BATCH, PAST, D).astype(BF16),
                                   kc[N_P:].reshape(DEC_BATCH, DEC_SEQ, D)], axis=1)
            v_s = jnp.concatenate([cache_diff_v[:, j].reshape(DEC_BATCH, PAST, D).astype(BF16),
                                   vb[N_P:].reshape(DEC_BATCH, DEC_SEQ, D)], axis=1)
            o_p = diff_attention(lam_init, lvecs, subln, q, k_p, v_p, 0, SEQ, SEQ)
            o_s = diff_attention(lam_init, lvecs, subln, q, k_s, v_s, N_P, DEC_SEQ, 256)
            x = proj_residual(o_p, o_s, diff_w_o[j].astype(BF16), x, mod)
        f = i // 2
        if i % 2 == 0:
            x = ffn_residual(x, gffn, mod, ffn_w_in[f].astype(BF16), ffn_w_out[f].astype(BF16))
        else:
            wr = jnp.concatenate([moe_w_router[f], jnp.zeros((D, LANES - N_EXPERTS), F32)], axis=1).astype(BF16)
            br = jnp.concatenate([moe_b_router[f], jnp.zeros((LANES - N_EXPERTS,), F32)]).reshape(1, LANES)
            x = moe_residual(x, gffn, mod, wr, br, moe_w_in[f].astype(BF16), moe_w_out[f].astype(BF16))
    y_p = final_norm(x, norm_final.reshape(1, D), 0, N_P)
    y_s = final_norm(x, norm_final.reshape(1, D), N_P, N_S)
    return (y_p.reshape(BATCH, SEQ, D), y_s.reshape(DEC_BATCH, DEC_SEQ, D),
            jnp.stack(new_ckv, axis=1), jnp.stack(new_kr, axis=1),
            jnp.stack(new_s5_re, axis=1), jnp.stack(new_s5_im, axis=1),
            jnp.stack(new_dk, axis=1), jnp.stack(new_dv, axis=1))
```

```python
import functools
import math

import jax
import jax.numpy as jnp
from jax import lax
from jax.experimental import pallas as pl
from jax.experimental.pallas import tpu as pltpu

D = 1024
BATCH = 16
SEQ = 256
DEPTH = 4
DEC_BATCH = 2
DEC_SEQ = 4096
PAST = 512
GRID_W = 64
N_P = BATCH * SEQ
N_S = DEC_BATCH * DEC_SEQ
N_TOK = N_P + N_S
N_GROUPS = 1 + DEC_BATCH

MLA_HEADS = 8
MLA_Q_RANK = 384
MLA_KV_RANK = 256
MLA_D_NOPE = 128
MLA_D_ROPE = 64
MLA_D_V = 128
MLA_DK = MLA_D_NOPE + MLA_D_ROPE

S5_GROUP = 16
S5_GROUPS = D // S5_GROUP
S5_STATE = 64
S5_GB = 8
S5_NGB = S5_GROUPS // S5_GB
S5_HALF = S5_GB * S5_STATE
S5_NCH = S5_HALF // 128
S5_SUB = 8

DIFF_HEADS = 8
DIFF_DH = D // (2 * DIFF_HEADS)

D_FF = 2816
N_EXPERTS = 8
ROPE_THETA = 10000.0
EPS = 1e-6

LOG2E = math.log2(math.e)
TM = 512
FF_TILE = 1408
LANES = 128
VMEM_LIMIT = 56 * 1024 * 1024

F32 = jnp.float32
BF16 = jnp.bfloat16


def _cparams(*sem):
    return pltpu.CompilerParams(dimension_semantics=sem, vmem_limit_bytes=VMEM_LIMIT)


def _group_of_tile(i, tm):
    n_p = N_P // tm
    per = DEC_SEQ // tm
    return jnp.where(i < n_p, 0, 1 + (i - n_p) // per)


def _rope_tile(i, tm):
    n_p = N_P // tm
    per = DEC_SEQ // tm
    return jnp.where(i < n_p, 0, 1 + (i - n_p) % per)


def _rms(x, g):
    return x * lax.rsqrt(jnp.mean(x * x, axis=-1, keepdims=True) + EPS) * g


def _normmod(x, g, mod, k_shift, k_scale):
    return _rms(x, g) * (1.0 + mod[k_scale:k_scale + 1, :]) + mod[k_shift:k_shift + 1, :]


def _sigmoid(x):
    return 1.0 / (1.0 + jnp.exp(-x))


def _dot(a, b):
    return jnp.dot(a, b, preferred_element_type=F32)


def _dot_nt(a, b):
    return lax.dot_general(a, b, (((1,), (1,)), ((), ())), preferred_element_type=F32)


def _rope(x, cos, sin):
    lane = lax.broadcasted_iota(jnp.int32, x.shape, 1)
    nxt = pltpu.roll(x, LANES - 16, 1)
    prv = pltpu.roll(x, 16, 1)
    swapped = jnp.where((lane % 32) < 16, nxt, prv)
    return x * cos + swapped * sin


def _ada_kernel(c_ref, w_ref, b_ref, o_ref):
    c = c_ref[...]
    s = (c * _sigmoid(c)).astype(BF16)
    o_ref[...] = _dot(s, w_ref[...].astype(BF16)) + b_ref[...]


def ada_all(cond8, ada_w, ada_b):
    tn = 1536
    return pl.pallas_call(
        _ada_kernel,
        grid=(DEPTH, 6 * D // tn),
        in_specs=[pl.BlockSpec((8, D), lambda l, n: (0, 0)),
                  pl.BlockSpec((None, D, tn), lambda l, n: (l, 0, n)),
                  pl.BlockSpec((None, 1, tn), lambda l, n: (l, 0, n))],
        out_specs=pl.BlockSpec((None, 8, tn), lambda l, n: (l, 0, n)),
        out_shape=jax.ShapeDtypeStruct((DEPTH, 8, 6 * D), F32),
        compiler_params=_cparams("parallel", "parallel"),
        name="ada",
    )(cond8, ada_w, ada_b.reshape(DEPTH, 1, 6 * D))


def _mod_spec(tm):
    return pl.BlockSpec((None, 6, D), lambda i, *_: (_group_of_tile(i, tm), 0, 0))


def _mla_tok_kernel(x_ref, g_ref, mod_ref, w1_ref, qn_ref, wuq_ref, kvn_ref, cos_ref, sin_ref,
                    q_ref, ckv_ref, kr_ref):
    h = _normmod(x_ref[...], g_ref[...], mod_ref[...], 0, 1).astype(BF16)
    t1 = _dot(h, w1_ref[...])
    ql = _rms(t1[:, :MLA_Q_RANK], qn_ref[...]).astype(BF16)
    q = _dot(ql, wuq_ref[...]) * (MLA_DK ** -0.5 * LOG2E)
    c0 = MLA_Q_RANK
    ckv_ref[...] = _rms(t1[:, c0:c0 + MLA_KV_RANK], kvn_ref[...])
    cos = cos_ref[...]
    sin = sin_ref[...]
    kr = _rope(t1[:, c0 + MLA_KV_RANK:c0 + MLA_KV_RANK + LANES], cos, sin)
    kr_ref[...] = kr[:, :MLA_D_ROPE]
    n_nope = MLA_HEADS * MLA_D_NOPE
    for pair in range(MLA_HEADS // 2):
        qr = _rope(q[:, n_nope + pair * LANES:n_nope + (pair + 1) * LANES], cos, sin).astype(BF16)
        for sub in range(2):
            hd = 2 * pair + sub
            q_ref[hd, :, 0:MLA_D_NOPE] = q[:, hd * MLA_D_NOPE:(hd + 1) * MLA_D_NOPE].astype(BF16)
            q_ref[hd, :, MLA_D_NOPE:MLA_DK] = qr[:, sub * MLA_D_ROPE:(sub + 1) * MLA_D_ROPE]


def mla_tokens(x, g, mod, w1, qn, wuq, kvn, cos_t, sin_t):
    nt = N_TOK // TM
    const = lambda shape: pl.BlockSpec(shape, lambda i: (0,) * len(shape))
    return pl.pallas_call(
        _mla_tok_kernel,
        grid=(nt,),
        in_specs=[pl.BlockSpec((TM, D), lambda i: (i, 0)), const((1, D)), _mod_spec(TM),
                  const(w1.shape), const((1, MLA_Q_RANK)), const(wuq.shape), const((1, MLA_KV_RANK)),
                  pl.BlockSpec((TM, LANES), lambda i: (_rope_tile(i, TM), 0)),
                  pl.BlockSpec((TM, LANES), lambda i: (_rope_tile(i, TM), 0))],
        out_specs=[pl.BlockSpec((MLA_HEADS, TM, MLA_DK), lambda i: (0, i, 0)),
                   pl.BlockSpec((TM, MLA_KV_RANK), lambda i: (i, 0)),
                   pl.BlockSpec((TM, MLA_D_ROPE), lambda i: (i, 0))],
        out_shape=[jax.ShapeDtypeStruct((MLA_HEADS, N_TOK, MLA_DK), BF16),
                   jax.ShapeDtypeStruct((N_TOK, MLA_KV_RANK), F32),
                   jax.ShapeDtypeStruct((N_TOK, MLA_D_ROPE), F32)],
        compiler_params=_cparams("parallel"),
        name="mla_tokens",
    )(x, g, mod, w1, qn, wuq, kvn, cos_t, sin_t)


def _mla_kv_kernel(ckv_ref, kr_ref, w_ref, k_ref, v_ref):
    kv = _dot(ckv_ref[...].astype(BF16), w_ref[...])
    kr = kr_ref[...].astype(BF16)
    n_nope = MLA_HEADS * MLA_D_NOPE
    for hd in range(MLA_HEADS):
        k_ref[hd, :, 0:MLA_D_NOPE] = kv[:, hd * MLA_D_NOPE:(hd + 1) * MLA_D_NOPE].astype(BF16)
        k_ref[hd, :, MLA_D_NOPE:MLA_DK] = kr
        v_ref[hd] = kv[:, n_nope + hd * MLA_D_V:n_nope + (hd + 1) * MLA_D_V].astype(BF16)


def mla_kv(ckv, kr, wukv, ts):
    nb, s, _ = ckv.shape
    return pl.pallas_call(
        _mla_kv_kernel,
        grid=(nb, s // ts),
        in_specs=[pl.BlockSpec((None, ts, MLA_KV_RANK), lambda b, t: (b, t, 0)),
                  pl.BlockSpec((None, ts, MLA_D_ROPE), lambda b, t: (b, t, 0)),
                  pl.BlockSpec(wukv.shape, lambda b, t: (0, 0))],
        out_specs=[pl.BlockSpec((None, MLA_HEADS, ts, MLA_DK), lambda b, t: (b, 0, t, 0)),
                   pl.BlockSpec((None, MLA_HEADS, ts, MLA_D_V), lambda b, t: (b, 0, t, 0))],
        out_shape=[jax.ShapeDtypeStruct((nb, MLA_HEADS, s, MLA_DK), BF16),
                   jax.ShapeDtypeStruct((nb, MLA_HEADS, s, MLA_D_V), BF16)],
        compiler_params=_cparams("parallel", "parallel"),
        name="mla_kv",
    )(ckv, kr, wukv)


def _mla_attn_kernel(hps, q_ref, k_ref, v_ref, o_ref):
    for hd in range(hps):
        s = _dot_nt(q_ref[hd], k_ref[hd])
        p = jnp.exp2(s - jnp.max(s, axis=-1, keepdims=True))
        l = jnp.sum(p, axis=-1, keepdims=True)
        o = _dot(p.astype(BF16), v_ref[hd]) / l
        o_ref[:, hd * MLA_D_V:(hd + 1) * MLA_D_V] = o.astype(BF16)


def mla_attention(q3, k3, v3, row0, seq, tq, hps):
    nb, _, s, _ = k3.shape
    nq = seq // tq
    base = row0 // tq
    return pl.pallas_call(
        functools.partial(_mla_attn_kernel, hps),
        grid=(nb, MLA_HEADS // hps, nq),
        in_specs=[pl.BlockSpec((hps, tq, MLA_DK), lambda b, h, i: (h, base + b * nq + i, 0)),
                  pl.BlockSpec((None, hps, s, MLA_DK), lambda b, h, i: (b, h, 0, 0)),
                  pl.BlockSpec((None, hps, s, MLA_D_V), lambda b, h, i: (b, h, 0, 0))],
        out_specs=pl.BlockSpec((tq, hps * MLA_D_V), lambda b, h, i: (b * nq + i, h)),
        out_shape=jax.ShapeDtypeStruct((nb * seq, MLA_HEADS * MLA_D_V), BF16),
        compiler_params=_cparams("parallel", "parallel", "parallel"),
        name="mla_attn_%d" % s,
    )(q3, k3, v3)


def _proj_res_kernel(op_ref, os_ref, w_ref, x_ref, mod_ref, out_ref):
    o = jnp.where(pl.program_id(0) < N_P // TM, op_ref[...], os_ref[...])
    out_ref[...] = x_ref[...] + mod_ref[2:3, :] * _dot(o, w_ref[...])


def proj_residual(o_p, o_s, w, x, mod):
    nt = N_TOK // TM
    n_p = N_P // TM
    return pl.pallas_call(
        _proj_res_kernel,
        grid=(nt,),
        in_specs=[pl.BlockSpec((TM, D), lambda i: (jnp.minimum(i, n_p - 1), 0)),
                  pl.BlockSpec((TM, D), lambda i: (jnp.maximum(i - n_p, 0), 0)),
                  pl.BlockSpec((D, D), lambda i: (0, 0)),
                  pl.BlockSpec((TM, D), lambda i: (i, 0)), _mod_spec(TM)],
        out_specs=pl.BlockSpec((TM, D), lambda i: (i, 0)),
        out_shape=jax.ShapeDtypeStruct((N_TOK, D), F32),
        compiler_params=_cparams("parallel"),
        name="proj_residual",
    )(o_p, o_s, w, x, mod)


def _diff_tok_kernel(latent, x_ref, g_ref, mod_ref, w_ref, *refs):
    if latent:
        cos_ref, sin_ref, q_ref, kc_ref, vc_ref = refs
        cos = cos_ref[...]
        sin = sin_ref[...]
        rot = lambda t: _rope(t, cos, sin)
    else:
        q_ref, kc_ref, vc_ref, k_ref, v_ref = refs
        rot = lambda t: t
    h = _normmod(x_ref[...], g_ref[...], mod_ref[...], 0, 1).astype(BF16)
    for c in range(D // LANES):
        sl = slice(c * LANES, (c + 1) * LANES)
        q = _dot(h, w_ref[:, c * LANES:(c + 1) * LANES]) * (DIFF_DH ** -0.5 * LOG2E)
        k = _dot(h, w_ref[:, D + c * LANES:D + (c + 1) * LANES])
        v = _dot(h, w_ref[:, 2 * D + c * LANES:2 * D + (c + 1) * LANES])
        q_ref[:, sl] = rot(q).astype(BF16)
        kc_ref[:, sl] = rot(k).astype(BF16)
        vc_ref[:, sl] = v.astype(BF16)
        if not latent:
            k_ref[:, sl] = k
            v_ref[:, sl] = v


def diff_tokens(x, g, mod, wqkv, cos_t, sin_t, latent):
    n_rows = N_S if latent else N_P
    base = (N_P if latent else 0) // TM
    row = lambda i: (i, 0)
    in_specs = [pl.BlockSpec((TM, D), lambda i: (base + i, 0)), pl.BlockSpec((1, D), lambda i: (0, 0)),
                pl.BlockSpec((None, 6, D), lambda i: (_group_of_tile(base + i, TM), 0, 0)),
                pl.BlockSpec((D, 3 * D), lambda i: (0, 0))]
    args = [x, g, mod, wqkv]
    out_shape = [jax.ShapeDtypeStruct((n_rows, D), BF16)] * 3
    if latent:
        rope_spec = pl.BlockSpec((TM, LANES), lambda i: (_rope_tile(base + i, TM), 0))
        in_specs += [rope_spec, rope_spec]
        args += [cos_t, sin_t]
    else:
        out_shape += [jax.ShapeDtypeStruct((n_rows, D), F32)] * 2
    return pl.pallas_call(
        functools.partial(_diff_tok_kernel, latent),
        grid=(n_rows // TM,),
        in_specs=in_specs,
        out_specs=[pl.BlockSpec((TM, D), row)] * len(out_shape),
        out_shape=out_shape,
        compiler_params=_cparams("parallel"),
        name="diff_tokens_lat" if latent else "diff_tokens_ctx",
    )(*args)


def _diff_attn_kernel(lam_init, n_kv, pairs, lq1_ref, lk1_ref, lq2_ref, lk2_ref, sub_ref, q_ref, *refs):
    kv_refs = refs[:2 * n_kv]
    o_ref = refs[2 * n_kv]
    lam = (jnp.exp(jnp.sum(lq1_ref[...] * lk1_ref[...], axis=-1, keepdims=True))
           - jnp.exp(jnp.sum(lq2_ref[...] * lk2_ref[...], axis=-1, keepdims=True)) + lam_init)
    for pr in range(pairs):
        lanes = slice(pr * LANES, (pr + 1) * LANES)
        q = q_ref[:, lanes]
        lane = lax.broadcasted_iota(jnp.int32, q.shape, 1)
        zero = jnp.zeros_like(q)

        def probs(qh):
            s = jnp.concatenate([_dot_nt(qh, kv_refs[2 * i][:, lanes]) for i in range(n_kv)], axis=1)
            p = jnp.exp2(s - jnp.max(s, axis=-1, keepdims=True))
            return p, jnp.sum(p, axis=-1, keepdims=True)

        p1, l1 = probs(jnp.where(lane < DIFF_DH, q, zero))
        p2, l2 = probs(jnp.where(lane >= DIFF_DH, q, zero))
        att = (p1 + (-lam * l1 / l2) * p2).astype(BF16)
        acc = jnp.zeros(q.shape, F32)
        col = 0
        for i in range(n_kv):
            n = kv_refs[2 * i + 1].shape[0]
            acc = acc + _dot(att[:, col:col + n], kv_refs[2 * i + 1][:, lanes])
            col += n
        o = acc / l1
        o_ref[:, lanes] = (_rms(o, sub_ref[...]) * (1.0 - lam_init)).astype(BF16)


def diff_attention(lam_init, lvecs, subln, q, kvs, seq, tq, pairs):
    nb = kvs[0][0].shape[0]
    nq = seq // tq
    n_keys = sum(k.shape[1] for k, _ in kvs)
    width = pairs * LANES
    vec = pl.BlockSpec((1, DIFF_DH), lambda b, h, i: (0, 0))
    kv_specs, kv_args = [], []
    for k, v in kvs:
        spec = pl.BlockSpec((None, k.shape[1], width), lambda b, h, i: (b, 0, h))
        kv_specs += [spec, spec]
        kv_args += [k, v]
    return pl.pallas_call(
        functools.partial(_diff_attn_kernel, lam_init, len(kvs), pairs),
        grid=(nb, DIFF_HEADS // pairs, nq),
        in_specs=[vec, vec, vec, vec, pl.BlockSpec((1, 2 * DIFF_DH), lambda b, h, i: (0, 0)),
                  pl.BlockSpec((tq, width), lambda b, h, i: (b * nq + i, h))] + kv_specs,
        out_specs=pl.BlockSpec((tq, width), lambda b, h, i: (b * nq + i, h)),
        out_shape=jax.ShapeDtypeStruct((nb * seq, D), BF16),
        compiler_params=_cparams("parallel", "parallel", "parallel"),
        name="diff_attn_%d" % n_keys,
    )(*lvecs, subln, q, *kv_args)


def _normmod_kernel(x_ref, g_ref, mod_ref, h_ref):
    h_ref[...] = _normmod(x_ref[...], g_ref[...], mod_ref[...], 0, 1)


S5_T = SEQ
S5_SEG = DEC_SEQ // S5_SUB
S5_TILES_P = N_P // S5_T
S5_PER_SEG = S5_SEG // S5_T


def _s5_tile_pos(i):
    k = i - S5_TILES_P
    per_batch = S5_SUB * S5_PER_SEG
    lat_row = S5_TILES_P // S5_SUB + (k // per_batch) * S5_PER_SEG + k % S5_PER_SEG
    lat_col = (k % per_batch) // S5_PER_SEG
    is_p = i < S5_TILES_P
    return jnp.where(is_p, i // S5_SUB, lat_row), jnp.where(is_p, i % S5_SUB, lat_col)


def normmod_time_major(x, g, mod):
    nt = N_TOK // S5_T
    return pl.pallas_call(
        _normmod_kernel,
        grid=(nt,),
        in_specs=[pl.BlockSpec((S5_T, D), lambda i: (i, 0)), pl.BlockSpec((1, D), lambda i: (0, 0)),
                  _mod_spec(S5_T)],
        out_specs=pl.BlockSpec((S5_T, D), lambda i: _s5_tile_pos(i)),
        out_shape=jax.ShapeDtypeStruct((N_TOK // S5_SUB, S5_SUB * D), F32),
        compiler_params=_cparams("parallel"),
        name="normmod",
    )(x, g, mod)


def _s5_kernel(chained, n, u_ref, wb_ref, wc_ref, a_ref, an_ref, dsk_ref, h0_ref, y_ref, fin_ref,
               bu_ref, ini_ref):
    d = pl.program_id(2)
    rows = S5_SUB * n
    chunk = 512
    nch = S5_NCH
    for r in range(rows // chunk):
        rs = slice(r * chunk, (r + 1) * chunk)
        bu = _dot(u_ref[rs, :].astype(BF16), wb_ref[...])
        for c in range(2 * nch):
            bu_ref[c, rs, :] = bu[:, c * LANES:(c + 1) * LANES]
    ar = [jnp.broadcast_to(a_ref[0:1, c * LANES:(c + 1) * LANES], (S5_SUB, LANES)) for c in range(nch)]
    ai = [jnp.broadcast_to(a_ref[1:2, c * LANES:(c + 1) * LANES], (S5_SUB, LANES)) for c in range(nch)]

    def step_index(s):
        return jnp.where(d == 0, s, n - 1 - s)

    def step_rows(t):
        return pl.ds(pl.multiple_of(t * S5_SUB, S5_SUB), S5_SUB)

    def advance(h, t):
        out = [None] * (2 * nch)
        for c in range(nch):
            br = bu_ref[c, step_rows(t), :]
            bi = bu_ref[nch + c, step_rows(t), :]
            out[c] = ar[c] * h[c] - ai[c] * h[nch + c] + br
            out[nch + c] = ar[c] * h[nch + c] + ai[c] * h[c] + bi
        return out

    unroll = 4
    zeros = [jnp.zeros((S5_SUB, LANES), F32) for _ in range(2 * nch)]

    if chained:
        def local_body(s, h):
            h = list(h)
            for k in range(unroll):
                h = advance(h, step_index(s * unroll + k))
            return tuple(h)

        ends = lax.fori_loop(0, n // unroll, local_body, tuple(zeros))
        sub_row = lax.broadcasted_iota(jnp.int32, (S5_SUB, LANES), 0)
        cur = [h0_ref[:, c * LANES:(c + 1) * LANES] for c in range(2 * nch)]
        anr = [an_ref[0:1, c * LANES:(c + 1) * LANES] for c in range(nch)]
        ani = [an_ref[1:2, c * LANES:(c + 1) * LANES] for c in range(nch)]
        for kk in range(S5_SUB):
            j = jnp.where(d == 0, kk, S5_SUB - 1 - kk)
            nxt = [None] * (2 * nch)
            for c in range(2 * nch):
                ini_ref[c, pl.ds(j, 1), :] = cur[c]
            for c in range(nch):
                er = jnp.sum(jnp.where(sub_row == j, ends[c], 0.0), axis=0, keepdims=True)
                ei = jnp.sum(jnp.where(sub_row == j, ends[nch + c], 0.0), axis=0, keepdims=True)
                nxt[c] = er + anr[c] * cur[c] - ani[c] * cur[nch + c]
                nxt[nch + c] = ei + anr[c] * cur[nch + c] + ani[c] * cur[c]
            cur = nxt
        h_init = [ini_ref[c] for c in range(2 * nch)]
    else:
        h_init = zeros

    def body(s, h):
        h = list(h)
        for k in range(unroll):
            t = step_index(s * unroll + k)
            h = advance(h, t)
            for c in range(2 * nch):
                bu_ref[c, step_rows(t), :] = h[c]
        return tuple(h)

    fin = lax.fori_loop(0, n // unroll, body, tuple(h_init))
    for c in range(2 * nch):
        fin_ref[:, c * LANES:(c + 1) * LANES] = fin[c]

    @pl.when(d == 0)
    def _():
        y_ref[...] = dsk_ref[...] * u_ref[...]

    for r in range(rows // chunk):
        rs = slice(r * chunk, (r + 1) * chunk)
        hs = jnp.concatenate([bu_ref[c, rs, :].astype(BF16) for c in range(2 * nch)], axis=1)
        y_ref[rs, :] += _dot(hs, wc_ref[...])


def s5_scan(h, wb, wc, a, an, dskip, h0, row0, n_blocks, n, chained):
    rows = S5_SUB * n
    base = row0 // rows
    kern = functools.partial(_s5_kernel, chained, n)
    return pl.pallas_call(
        kern,
        grid=(n_blocks, S5_NGB, 2),
        in_specs=[pl.BlockSpec((rows, LANES), lambda r, c, d: (base + r, c)),
                  pl.BlockSpec((None, None, LANES, 2 * S5_HALF), lambda r, c, d: (d, c, 0, 0)),
                  pl.BlockSpec((None, None, 2 * S5_HALF, LANES), lambda r, c, d: (d, c, 0, 0)),
                  pl.BlockSpec((None, None, 2, S5_HALF), lambda r, c, d: (d, c, 0, 0)),
                  pl.BlockSpec((None, None, 2, S5_HALF), lambda r, c, d: (d, c, 0, 0)),
                  pl.BlockSpec((1, LANES), lambda r, c, d: (0, c)),
                  pl.BlockSpec((None, None, None, 1, 2 * S5_HALF), lambda r, c, d: (r, d, c, 0, 0))],
        out_specs=[pl.BlockSpec((rows, LANES), lambda r, c, d: (r, c)),
                   pl.BlockSpec((None, None, None, S5_SUB, 2 * S5_HALF), lambda r, c, d: (r, d, c, 0, 0))],
        out_shape=[jax.ShapeDtypeStruct((n_blocks * rows, D), F32),
                   jax.ShapeDtypeStruct((n_blocks, 2, S5_NGB, S5_SUB, 2 * S5_HALF), F32)],
        scratch_shapes=[pltpu.VMEM((2 * S5_NCH, rows, LANES), F32),
                        pltpu.VMEM((2 * S5_NCH, S5_SUB, LANES), F32)],
        compiler_params=_cparams("parallel", "parallel", "arbitrary"),
        name="s5_scan_%d" % n,
    )(h, wb, wc, a, an, dskip, h0)


def _glu_res_kernel(yp_ref, ys_ref, w_ref, x_ref, mod_ref, out_ref):
    y = jnp.where(pl.program_id(0) < S5_TILES_P, yp_ref[...], ys_ref[...])
    g = 0.5 * y * (1.0 + jnp.tanh(math.sqrt(2.0 / math.pi) * (y + 0.044715 * (y * y * y))))
    t = _dot(g.astype(BF16), w_ref[...])
    out_ref[...] = x_ref[...] + mod_ref[2:3, :] * (t[:, :D] * _sigmoid(t[:, D:]))


def glu_residual(y_p, y_s, w, x, mod):
    nt = N_TOK // S5_T
    n_blk_p = S5_TILES_P // S5_SUB

    def yp_map(i):
        r, c = _s5_tile_pos(jnp.minimum(i, S5_TILES_P - 1))
        return r, c

    def ys_map(i):
        r, c = _s5_tile_pos(jnp.maximum(i, S5_TILES_P))
        return r - n_blk_p, c

    return pl.pallas_call(
        _glu_res_kernel,
        grid=(nt,),
        in_specs=[pl.BlockSpec((S5_T, D), yp_map), pl.BlockSpec((S5_T, D), ys_map),
                  pl.BlockSpec((D, 2 * D), lambda i: (0, 0)),
                  pl.BlockSpec((S5_T, D), lambda i: (i, 0)), _mod_spec(S5_T)],
        out_specs=pl.BlockSpec((S5_T, D), lambda i: (i, 0)),
        out_shape=jax.ShapeDtypeStruct((N_TOK, D), F32),
        compiler_params=_cparams("parallel"),
        name="glu_residual",
    )(y_p.reshape(N_P // S5_SUB, S5_SUB * D), y_s.reshape(N_S // S5_SUB, S5_SUB * D), w, x, mod)


def _ffn_kernel(x_ref, g_ref, mod_ref, wa_ref, wb_ref, wo_ref, out_ref, h_scr, acc_scr):
    f = pl.program_id(1)

    @pl.when(f == 0)
    def _():
        h_scr[...] = _normmod(x_ref[...], g_ref[...], mod_ref[...], 3, 4).astype(BF16)
        acc_scr[...] = jnp.zeros_like(acc_scr)

    h = h_scr[...]
    a = _dot(h, wa_ref[...])
    b = _dot(h, wb_ref[...])
    act = (a * _sigmoid(a) * b).astype(BF16)
    acc_scr[...] += _dot(act, wo_ref[...])

    @pl.when(f == pl.num_programs(1) - 1)
    def _():
        out_ref[...] = x_ref[...] + mod_ref[5:6, :] * acc_scr[...]


def ffn_residual(x, g, mod, w_in, w_out):
    nt = N_TOK // TM
    nf = D_FF // FF_TILE
    return pl.pallas_call(
        _ffn_kernel,
        grid=(nt, nf),
        in_specs=[pl.BlockSpec((TM, D), lambda i, f: (i, 0)), pl.BlockSpec((1, D), lambda i, f: (0, 0)),
                  _mod_spec(TM),
                  pl.BlockSpec((D, FF_TILE), lambda i, f: (0, f)),
                  pl.BlockSpec((D, FF_TILE), lambda i, f: (0, f + nf)),
                  pl.BlockSpec((FF_TILE, D), lambda i, f: (f, 0))],
        out_specs=pl.BlockSpec((TM, D), lambda i, f: (i, 0)),
        out_shape=jax.ShapeDtypeStruct((N_TOK, D), F32),
        scratch_shapes=[pltpu.VMEM((TM, D), BF16), pltpu.VMEM((TM, D), F32)],
        compiler_params=_cparams("parallel", "arbitrary"),
        name="ffn",
    )(x, g, mod, w_in, w_in, w_out)


TS = 512
N_SLOTS = 2 * N_TOK + N_EXPERTS * TS
NT_S = N_SLOTS // TS
PIECE = 256
MAX_PIECES = (TM + 16 + PIECE - 1) // PIECE


def _moe_route_kernel(x_ref, g_ref, mod_ref, wr_ref, br_ref, h_ref, route_ref, rows_ref, run_ref, carry_scr):
    @pl.when(pl.program_id(0) == 0)
    def _():
        carry_scr[...] = jnp.zeros_like(carry_scr)

    h = _normmod(x_ref[...], g_ref[...], mod_ref[...], 3, 4).astype(BF16)
    h_ref[...] = h
    logits = _dot(h, wr_ref[...]) + br_ref[...]
    lane = lax.broadcasted_iota(jnp.int32, logits.shape, 1)
    neg = jnp.float32(-jnp.inf)
    lg = jnp.where(lane < N_EXPERTS, logits, neg)
    v1 = jnp.max(lg, axis=-1, keepdims=True)
    i1 = jnp.min(jnp.where(lg == v1, lane, LANES), axis=-1, keepdims=True)
    lg2 = jnp.where(lane == i1, neg, lg)
    v2 = jnp.max(lg2, axis=-1, keepdims=True)
    i2 = jnp.min(jnp.where(lg2 == v2, lane, LANES), axis=-1, keepdims=True)
    e2 = jnp.exp(v2 - v1)
    g1 = 1.0 / (1.0 + e2)
    g2 = e2 / (1.0 + e2)
    oh1 = lane == i1
    oh2 = lane == i2
    sel = jnp.where(oh1 | oh2, 1.0, 0.0)
    r = lax.broadcasted_iota(jnp.int32, (TM, TM), 0)
    c = lax.broadcasted_iota(jnp.int32, (TM, TM), 1)
    tri = jnp.where(c < r, 1.0, 0.0).astype(BF16)
    rank = _dot(tri, sel.astype(BF16)) + carry_scr[0:1, :]
    r1 = jnp.sum(jnp.where(oh1, rank, 0.0), axis=-1, keepdims=True)
    r2 = jnp.sum(jnp.where(oh2, rank, 0.0), axis=-1, keepdims=True)
    cols = (i1.astype(F32), i2.astype(F32), g1, g2, r1, r2)
    route = jnp.zeros(logits.shape, F32)
    for k, v in enumerate(cols):
        route = jnp.where(lane == k, v, route)
    route_ref[...] = route
    rows_ref[...] = route.T[0:8, :]
    carry_scr[...] = carry_scr[...] + jnp.sum(sel, axis=0, keepdims=True)
    run_ref[...] = carry_scr[...]


def moe_route(x, g, mod, w_router, b_router):
    nt = N_TOK // TM
    return pl.pallas_call(
        _moe_route_kernel,
        grid=(nt,),
        in_specs=[pl.BlockSpec((TM, D), lambda i: (i, 0)), pl.BlockSpec((1, D), lambda i: (0, 0)), _mod_spec(TM),
                  pl.BlockSpec((D, LANES), lambda i: (0, 0)), pl.BlockSpec((1, LANES), lambda i: (0, 0))],
        out_specs=[pl.BlockSpec((TM, D), lambda i: (i, 0)), pl.BlockSpec((TM, LANES), lambda i: (i, 0)),
                   pl.BlockSpec((8, TM), lambda i: (0, i)),
                   pl.BlockSpec((None, 8, LANES), lambda i: (i, 0, 0))],
        out_shape=[jax.ShapeDtypeStruct((N_TOK, D), BF16), jax.ShapeDtypeStruct((N_TOK, LANES), F32),
                   jax.ShapeDtypeStruct((8, N_TOK), F32),
                   jax.ShapeDtypeStruct((nt, 8, LANES), F32)],
        scratch_shapes=[pltpu.VMEM((8, LANES), F32)],
        compiler_params=_cparams("arbitrary"),
        name="moe_route",
    )(x, g, mod, w_router, b_router)


def _moe_gather_kernel(used_ref, clo_ref, chi_ref, pos_ref, gates_ref, h_ref, xs_ref, gate_ref, gat_scr, gsum_scr):
    i = pl.program_id(0)
    gat_scr[...] = jnp.zeros_like(gat_scr)
    gsum_scr[...] = jnp.zeros_like(gsum_scr)

    @pl.when(used_ref[i] > 0)
    def _():
        slot = i * TS + lax.broadcasted_iota(jnp.int32, (TS, TM), 0)

        def body(c, carry):
            base = pl.multiple_of(c * TM, TM)
            m1 = pos_ref[0:1, pl.ds(base, TM)] == slot
            m2 = pos_ref[1:2, pl.ds(base, TM)] == slot
            pick = jnp.where(m1 | m2, 1.0, 0.0).astype(BF16)
            gat_scr[...] += _dot(pick, h_ref[pl.ds(base, TM), :])
            g = jnp.where(m1, gates_ref[0:1, pl.ds(base, TM)], 0.0) + jnp.where(m2, gates_ref[1:2, pl.ds(base, TM)], 0.0)
            gsum_scr[...] += jnp.sum(g, axis=-1, keepdims=True)
            return carry

        lax.fori_loop(clo_ref[i], chi_ref[i] + 1, body, 0)

    xs_ref[...] = gat_scr[...].astype(BF16)
    gate_ref[...] = gsum_scr[...]


def moe_gather(used, clo, chi, pos_rows, gate_rows, h_b):
    whole = lambda shape: pl.BlockSpec(shape, lambda i, *_: (0, 0), pipeline_mode=pl.Buffered(1))
    grid_spec = pltpu.PrefetchScalarGridSpec(
        num_scalar_prefetch=3,
        grid=(NT_S,),
        in_specs=[whole((8, N_TOK)), whole((8, N_TOK)), whole((N_TOK, D))],
        out_specs=[pl.BlockSpec((TS, D), lambda i, *_: (i, 0)), pl.BlockSpec((TS, 1), lambda i, *_: (i, 0))],
        scratch_shapes=[pltpu.VMEM((TS, D), F32), pltpu.VMEM((TS, 1), F32)],
    )
    return pl.pallas_call(
        _moe_gather_kernel,
        grid_spec=grid_spec,
        out_shape=[jax.ShapeDtypeStruct((N_SLOTS, D), BF16), jax.ShapeDtypeStruct((N_SLOTS, 1), F32)],
        compiler_params=_cparams("arbitrary"),
        name="moe_gather",
    )(used, clo, chi, pos_rows, gate_rows, h_b)


def _moe_ffn_kernel(texp_ref, used_ref, xs_ref, gate_ref, wa_ref, wb_ref, wo_ref, ys_ref, acc_scr):
    i = pl.program_id(0)
    f = pl.program_id(1)
    live = used_ref[i] > 0

    @pl.when(f == 0)
    def _():
        acc_scr[...] = jnp.zeros_like(acc_scr)

    @pl.when(live)
    def _():
        xs = xs_ref[...]
        a = _dot(xs, wa_ref[...])
        b = _dot(xs, wb_ref[...])
        act = (a * _sigmoid(a) * b).astype(BF16)
        acc_scr[...] += _dot(act, wo_ref[...])

    @pl.when(f == pl.num_programs(1) - 1)
    def _():
        ys_ref[...] = (gate_ref[...] * acc_scr[...]).astype(BF16)


def moe_ffn(texp, used, xs, gate_col, w_in, w_out):
    nf = D_FF // FF_TILE
    grid_spec = pltpu.PrefetchScalarGridSpec(
        num_scalar_prefetch=2,
        grid=(NT_S, nf),
        in_specs=[pl.BlockSpec((TS, D), lambda i, f, *_: (i, 0)),
                  pl.BlockSpec((TS, 1), lambda i, f, *_: (i, 0)),
                  pl.BlockSpec((None, D, FF_TILE), lambda i, f, texp, *_: (texp[i], 0, f)),
                  pl.BlockSpec((None, D, FF_TILE), lambda i, f, texp, *_: (texp[i], 0, f + nf)),
                  pl.BlockSpec((None, FF_TILE, D), lambda i, f, texp, *_: (texp[i], f, 0))],
        out_specs=pl.BlockSpec((TS, D), lambda i, f, *_: (i, 0)),
        scratch_shapes=[pltpu.VMEM((TS, D), F32)],
    )
    return pl.pallas_call(
        _moe_ffn_kernel,
        grid_spec=grid_spec,
        out_shape=jax.ShapeDtypeStruct((N_SLOTS, D), BF16),
        compiler_params=_cparams("parallel", "arbitrary"),
        name="moe_ffn",
    )(texp, used, xs, gate_col, w_in, w_in, w_out)


def _moe_combine_kernel(start_ref, npc_ref, lo_ref, hi_ref, off_ref, x_ref, mod_ref, route_ref, ys_hbm, out_ref,
                        buf, acc_scr, sem):
    i = pl.program_id(0)

    def piece_copy(e, k):
        s = pl.multiple_of(start_ref[i * N_EXPERTS + e] + k * PIECE, 16)
        return pltpu.make_async_copy(ys_hbm.at[pl.ds(s, PIECE), :], buf.at[e, pl.ds(k * PIECE, PIECE), :],
                                     sem.at[e, k])

    for e in range(N_EXPERTS):
        for k in range(MAX_PIECES):
            @pl.when(k < npc_ref[i * N_EXPERTS + e])
            def _():
                piece_copy(e, k).start()

    acc_scr[...] = jnp.zeros_like(acc_scr)
    route = route_ref[...]
    e1 = route[:, 0:1].astype(jnp.int32)
    e2 = route[:, 1:2].astype(jnp.int32)
    pos1 = route[:, 4:5].astype(jnp.int32)
    pos2 = route[:, 5:6].astype(jnp.int32)
    for e in range(N_EXPERTS):
        pos1 = pos1 + jnp.where(e1 == e, off_ref[e], 0)
        pos2 = pos2 + jnp.where(e2 == e, off_ref[e], 0)
    lane = lax.broadcasted_iota(jnp.int32, (TM, PIECE), 1)
    for e in range(N_EXPERTS):
        lo = lo_ref[i * N_EXPERTS + e]
        hi = hi_ref[i * N_EXPERTS + e]
        p1 = jnp.where((pos1 >= lo) & (pos1 < hi), pos1, -1)
        p2 = jnp.where((pos2 >= lo) & (pos2 < hi), pos2, -1)
        for k in range(MAX_PIECES):
            @pl.when(k < npc_ref[i * N_EXPERTS + e])
            def _():
                piece_copy(e, k).wait()
                base = start_ref[i * N_EXPERTS + e] + k * PIECE
                pick = jnp.where((p1 - base == lane) | (p2 - base == lane), 1.0, 0.0).astype(BF16)
                acc_scr[...] += _dot(pick, buf[e, k * PIECE:(k + 1) * PIECE, :])

    out_ref[...] = x_ref[...] + mod_ref[5:6, :] * acc_scr[...]


def moe_combine(start, npc, lo, hi, off, x, mod, route, ys):
    nt = N_TOK // TM
    grid_spec = pltpu.PrefetchScalarGridSpec(
        num_scalar_prefetch=5,
        grid=(nt,),
        in_specs=[pl.BlockSpec((TM, D), lambda i, *_: (i, 0)),
                  pl.BlockSpec((None, 6, D), lambda i, *_: (_group_of_tile(i, TM), 0, 0)),
                  pl.BlockSpec((TM, LANES), lambda i, *_: (i, 0)),
                  pl.BlockSpec(memory_space=pl.ANY)],
        out_specs=pl.BlockSpec((TM, D), lambda i, *_: (i, 0)),
        scratch_shapes=[pltpu.VMEM((N_EXPERTS, MAX_PIECES * PIECE, D), BF16), pltpu.VMEM((TM, D), F32),
                        pltpu.SemaphoreType.DMA((N_EXPERTS, MAX_PIECES))],
    )
    return pl.pallas_call(
        _moe_combine_kernel,
        grid_spec=grid_spec,
        out_shape=jax.ShapeDtypeStruct((N_TOK, D), F32),
        compiler_params=_cparams("arbitrary"),
        name="moe_combine",
    )(start, npc, lo, hi, off, x, mod, route, ys)


def moe_residual(x, g, mod, w_router, b_router, w_in, w_out):
    nt = N_TOK // TM
    i32 = jnp.int32
    h_b, route, rows, run = moe_route(x, g, mod, w_router, b_router)
    run = run[:, 0, :N_EXPERTS].astype(i32)
    run_prev = jnp.concatenate([jnp.zeros((1, N_EXPERTS), i32), run[:-1]], axis=0)
    total = run[-1]
    padded = (total + TS - 1) // TS * TS
    off_end = jnp.cumsum(padded)
    off = off_end - padded
    experts = jnp.arange(N_EXPERTS, dtype=i32)
    off_of = lambda e_row: jnp.sum(jnp.where(e_row[None, :].astype(i32) == experts[:, None], off[:, None], 0), axis=0)
    pos1 = rows[4].astype(i32) + off_of(rows[0])
    pos2 = rows[5].astype(i32) + off_of(rows[1])
    pos_rows = jnp.concatenate([pos1[None], pos2[None], jnp.full((6, N_TOK), -1, i32)], axis=0)
    gate_rows = jnp.concatenate([rows[2:4], jnp.zeros((6, N_TOK), F32)], axis=0)
    tile_start = jnp.arange(NT_S, dtype=i32) * TS
    used = (tile_start < off_end[-1]).astype(i32)
    texp = jnp.minimum(jnp.sum((tile_start[:, None] >= off_end[None, :]).astype(i32), axis=1), N_EXPERTS - 1)
    mine = texp[:, None] == experts[None, :]
    of_tile = lambda per_e: jnp.sum(jnp.where(mine[:, None, :], per_e[None], 0), axis=2)
    rank0 = tile_start - jnp.sum(jnp.where(mine, off[None, :], 0), axis=1)
    rank1 = jnp.minimum(rank0 + TS, jnp.sum(jnp.where(mine, total[None, :], 0), axis=1))
    clo = jnp.sum((of_tile(run) <= rank0[:, None]).astype(i32), axis=1)
    chi = jnp.sum((of_tile(run_prev) < rank1[:, None]).astype(i32), axis=1) - 1
    clo = jnp.where(used > 0, jnp.minimum(clo, nt - 1), 0).astype(i32)
    chi = jnp.where(used > 0, chi, -1).astype(i32)
    xs, gate_col = moe_gather(used, clo, chi, pos_rows, gate_rows, h_b)
    ys = moe_ffn(texp, used, xs, gate_col, w_in, w_out)
    lo = off[None, :] + run_prev
    hi = off[None, :] + run
    start = jnp.minimum(lo // 16 * 16, N_SLOTS - MAX_PIECES * PIECE)
    npc = jnp.where(hi > lo, (hi - start + PIECE - 1) // PIECE, 0)
    flat = lambda a: a.reshape(nt * N_EXPERTS).astype(i32)
    return moe_combine(flat(start), flat(npc), flat(lo), flat(hi), off.astype(i32), x, mod, route, ys)


def _final_norm_kernel(x_ref, g_ref, o_ref):
    o_ref[...] = _rms(x_ref[...], g_ref[...])


def final_norm(x, g, row0, n_rows):
    base = row0 // TM
    return pl.pallas_call(
        _final_norm_kernel,
        grid=(n_rows // TM,),
        in_specs=[pl.BlockSpec((TM, D), lambda i: (base + i, 0)), pl.BlockSpec((1, D), lambda i: (0, 0))],
        out_specs=pl.BlockSpec((TM, D), lambda i: (i, 0)),
        out_shape=jax.ShapeDtypeStruct((n_rows, D), F32),
        compiler_params=_cparams("parallel"),
        name="final_norm",
    )(x, g)


def _rope_tables():
    half = 16
    freqs = ROPE_THETA ** (-jnp.arange(half, dtype=F32) / half)
    t = jnp.arange(DEC_SEQ, dtype=jnp.int32)
    row = (t // GRID_W).astype(F32)[:, None] * freqs[None, :]
    col = (t % GRID_W).astype(F32)[:, None] * freqs[None, :]
    cos = jnp.concatenate([jnp.cos(row), jnp.cos(row), jnp.cos(col), jnp.cos(col)], axis=1)
    sin = jnp.concatenate([-jnp.sin(row), jnp.sin(row), -jnp.sin(col), jnp.sin(col)], axis=1)
    cos = jnp.concatenate([jnp.ones((TM, 64), F32), cos], axis=0)
    sin = jnp.concatenate([jnp.zeros((TM, 64), F32), sin], axis=0)
    return jnp.tile(cos, (1, 2)), jnp.tile(sin, (1, 2))


def _mla_weights(w_dq, w_uq, w_dkv, w_ukv, w_o):
    w1 = jnp.concatenate([w_dq, w_dkv, jnp.zeros((D, LANES - MLA_D_ROPE), F32)], axis=1).astype(BF16)
    uq = w_uq.reshape(MLA_Q_RANK, MLA_HEADS, MLA_DK)
    wuq = jnp.concatenate([uq[:, :, :MLA_D_NOPE].reshape(MLA_Q_RANK, -1),
                           uq[:, :, MLA_D_NOPE:].reshape(MLA_Q_RANK, -1)], axis=1).astype(BF16)
    ukv = w_ukv.reshape(MLA_KV_RANK, MLA_HEADS, MLA_D_NOPE + MLA_D_V)
    wukv = jnp.concatenate([ukv[:, :, :MLA_D_NOPE].reshape(MLA_KV_RANK, -1),
                            ukv[:, :, MLA_D_NOPE:].reshape(MLA_KV_RANK, -1)], axis=1).astype(BF16)
    return w1, wuq, wukv, w_o.astype(BF16)


def _s5_weights(a_re, a_im, log_dt, b_re, b_im, c_re, c_im, seg_len):
    dt = jnp.exp(log_dt)[..., None]
    mag = jnp.exp(a_re * dt)
    abar_re, abar_im = mag * jnp.cos(a_im * dt), mag * jnp.sin(a_im * dt)
    mag_n = jnp.exp(a_re * dt * seg_len)
    apow_re, apow_im = mag_n * jnp.cos(a_im * dt * seg_len), mag_n * jnp.sin(a_im * dt * seg_len)
    den = a_re * a_re + a_im * a_im
    coef_re = ((abar_re - 1.0) * a_re + abar_im * a_im) / den
    coef_im = (abar_im * a_re - (abar_re - 1.0) * a_im) / den
    bbar_re = coef_re[..., None] * b_re - coef_im[..., None] * b_im
    bbar_im = coef_re[..., None] * b_im + coef_im[..., None] * b_re
    eye = jnp.eye(S5_GB, dtype=F32)

    def in_block(m):
        m = m.reshape(2, S5_NGB, S5_GB, S5_STATE, S5_GROUP)
        return jnp.einsum('dbgpc,gh->dbgchp', m, eye).reshape(2, S5_NGB, LANES, S5_HALF)

    def out_block(m):
        m = m.reshape(2, S5_NGB, S5_GB, S5_GROUP, S5_STATE)
        return jnp.einsum('dbgcp,gh->dbgphc', m, eye).reshape(2, S5_NGB, S5_HALF, LANES)

    wb = jnp.concatenate([in_block(bbar_re), in_block(bbar_im)], axis=3).astype(BF16)
    wc = jnp.concatenate([out_block(c_re), out_block(-c_im)], axis=2).astype(BF16)
    lanes = lambda m: m.reshape(2, S5_NGB, 1, S5_HALF)
    a = jnp.concatenate([lanes(abar_re), lanes(abar_im)], axis=2)
    an = jnp.concatenate([lanes(apow_re), lanes(apow_im)], axis=2)
    return wb, wc, a, an


def kernel(x_prompt, x_sample, c, c_ctx, cache_mla_ckv, cache_mla_krope, state_s5_re, state_s5_im, cache_diff_k, cache_diff_v, ada_w, ada_b, norm_mix, norm_ffn, norm_final, mla_w_dq, mla_q_norm, mla_w_uq, mla_w_dkv, mla_kv_norm, mla_w_ukv, mla_w_o, s5_a_re, s5_a_im, s5_log_dt, s5_b_re, s5_b_im, s5_c_re, s5_c_im, s5_d, s5_w_glu, diff_w_qkv, diff_lq1, diff_lk1, diff_lq2, diff_lk2, diff_subln, diff_w_o, ffn_w_in, ffn_w_out, moe_w_router, moe_b_router, moe_w_in, moe_w_out):
    x = jnp.concatenate([x_prompt.reshape(N_P, D), x_sample.reshape(N_S, D)], axis=0)
    cond8 = jnp.concatenate([c_ctx[None], c, jnp.zeros((8 - N_GROUPS, D), F32)], axis=0)
    mods = ada_all(cond8, ada_w, ada_b).reshape(DEPTH, 8, 6, D)[:, :N_GROUPS]
    cos_t, sin_t = _rope_tables()

    new_ckv, new_kr, new_s5_re, new_s5_im, new_dk, new_dv = [], [], [], [], [], []
    for i in range(DEPTH):
        mod = mods[i]
        gmix = norm_mix[i].reshape(1, D)
        gffn = norm_ffn[i].reshape(1, D)
        j = i // 3
        kind = i % 3
        if kind == 0:
            w1, wuq, wukv, wo = _mla_weights(mla_w_dq[j], mla_w_uq[j], mla_w_dkv[j], mla_w_ukv[j], mla_w_o[j])
            q3, ckv, kr = mla_tokens(x, gmix, mod, w1, mla_q_norm[j].reshape(1, -1), wuq,
                                     mla_kv_norm[j].reshape(1, -1), cos_t, sin_t)
            ckv_p = ckv[:N_P].reshape(BATCH, SEQ, MLA_KV_RANK)
            kr_p = kr[:N_P].reshape(BATCH, SEQ, MLA_D_ROPE)
            new_ckv.append(ckv_p)
            new_kr.append(kr_p)
            ckv_s = jnp.concatenate([cache_mla_ckv[:, j], ckv[N_P:].reshape(DEC_BATCH, DEC_SEQ, -1)], axis=1)
            kr_s = jnp.concatenate([cache_mla_krope[:, j], kr[N_P:].reshape(DEC_BATCH, DEC_SEQ, -1)], axis=1)
            k3p, v3p = mla_kv(ckv_p, kr_p, wukv, SEQ)
            k3s, v3s = mla_kv(ckv_s, kr_s, wukv, 512)
            o_p = mla_attention(q3, k3p, v3p, 0, SEQ, SEQ, MLA_HEADS)
            o_s = mla_attention(q3, k3s, v3s, N_P, DEC_SEQ, 256, 2)
            x = proj_residual(o_p, o_s, wo, x, mod)
        elif kind == 1:
            h = normmod_time_major(x, gmix, mod).reshape(N_TOK, D)
            dsk = s5_d[j].reshape(1, D)
            seg = S5_SEG
            wb, wc, a, an = _s5_weights(s5_a_re[j], s5_a_im[j], s5_log_dt[j], s5_b_re[j], s5_b_im[j],
                                        s5_c_re[j], s5_c_im[j], seg)
            zero_h0 = jnp.zeros((BATCH // S5_SUB, 2, S5_NGB, 1, 2 * S5_HALF), F32)
            y_p, fin = s5_scan(h, wb, wc, a, an, dsk, zero_h0, 0, BATCH // S5_SUB, SEQ, False)
            fin = fin.reshape(BATCH // S5_SUB, 2, S5_NGB, S5_SUB, 2, S5_GB, S5_STATE)
            fin = jnp.transpose(fin, (0, 3, 1, 4, 2, 5, 6)).reshape(BATCH, 2, 2, S5_GROUPS, S5_STATE)
            new_s5_re.append(fin[:, :, 0])
            new_s5_im.append(fin[:, :, 1])
            h0 = jnp.stack([state_s5_re[:, j], state_s5_im[:, j]], axis=2)
            h0 = h0.reshape(DEC_BATCH, 2, 2, S5_NGB, S5_HALF)
            h0 = jnp.transpose(h0, (0, 1, 3, 2, 4)).reshape(DEC_BATCH, 2, S5_NGB, 1, 2 * S5_HALF)
            y_s, _ = s5_scan(h, wb, wc, a, an, dsk, h0, N_P, DEC_BATCH, seg, True)
            x = glu_residual(y_p, y_s, s5_w_glu[j].astype(BF16), x, mod)
        else:
            lam_init = 0.8 - 0.6 * math.exp(-0.3 * i)
            wqkv = diff_w_qkv[j].astype(BF16)
            q_p, k_p, v_p, k, v = diff_tokens(x, gmix, mod, wqkv, cos_t, sin_t, latent=False)
            q_s, k_s, v_s = diff_tokens(x, gmix, mod, wqkv, cos_t, sin_t, latent=True)
            new_dk.append(k.reshape(BATCH, SEQ, 2 * DIFF_HEADS, DIFF_DH))
            new_dv.append(v.reshape(BATCH, SEQ, DIFF_HEADS, 2 * DIFF_DH))
            lvecs = [a_.reshape(1, DIFF_DH) for a_ in (diff_lq1[j], diff_lk1[j], diff_lq2[j], diff_lk2[j])]
            subln = diff_subln[j].reshape(1, 2 * DIFF_DH)
            ctx_kv = [(k_p.reshape(BATCH, SEQ, D), v_p.reshape(BATCH, SEQ, D))]
            lat_kv = [(cache_diff_k[:, j].reshape(DEC_BATCH, PAST, D).astype(BF16),
                       cache_diff_v[:, j].reshape(DEC_BATCH, PAST, D).astype(BF16)),
                      (k_s.reshape(DEC_BATCH, DEC_SEQ, D), v_s.reshape(DEC_BATCH, DEC_SEQ, D))]
            o_p = diff_attention(lam_init, lvecs, subln, q_p, ctx_kv, SEQ, SEQ, DIFF_HEADS)
            o_s = diff_attention(lam_init, lvecs, subln, q_s, lat_kv, DEC_SEQ, 256, 1)
            x = proj_residual(o_p, o_s, diff_w_o[j].astype(BF16), x, mod)
        f = i // 2
        if i % 2 == 0:
            x = ffn_residual(x, gffn, mod, ffn_w_in[f].astype(BF16), ffn_w_out[f].astype(BF16))
        else:
            wr = jnp.concatenate([moe_w_router[f], jnp.zeros((D, LANES - N_EXPERTS), F32)], axis=1).astype(BF16)
            br = jnp.concatenate([moe_b_router[f], jnp.zeros((LANES - N_EXPERTS,), F32)]).reshape(1, LANES)
            x = moe_residual(x, gffn, mod, wr, br, moe_w_in[f].astype(BF16), moe_w_out[f].astype(BF16))
    y_p = final_norm(x, norm_final.reshape(1, D), 0, N_P)
    y_s = final_norm(x, norm_final.reshape(1, D), N_P, N_S)
    return (y_p.reshape(BATCH, SEQ, D), y_s.reshape(DEC_BATCH, DEC_SEQ, D),
            jnp.stack(new_ckv, axis=1), jnp.stack(new_kr, axis=1),
            jnp.stack(new_s5_re, axis=1), jnp.stack(new_s5_im, axis=1),
            jnp.stack(new_dk, axis=1), jnp.stack(new_dv, axis=1))
```

```python
import functools
import math

import jax
import jax.numpy as jnp
from jax import lax
from jax.experimental import pallas as pl
from jax.experimental.pallas import tpu as pltpu

D = 1024
BATCH = 16
SEQ = 256
DEPTH = 4
DEC_BATCH = 2
DEC_SEQ = 4096
PAST = 512
GRID_W = 64
N_P = BATCH * SEQ
N_S = DEC_BATCH * DEC_SEQ
N_TOK = N_P + N_S
N_GROUPS = 1 + DEC_BATCH

MLA_HEADS = 8
MLA_Q_RANK = 384
MLA_KV_RANK = 256
MLA_D_NOPE = 128
MLA_D_ROPE = 64
MLA_D_V = 128
MLA_DK = MLA_D_NOPE + MLA_D_ROPE

S5_GROUP = 16
S5_GROUPS = D // S5_GROUP
S5_STATE = 64
S5_GB = 8
S5_NGB = S5_GROUPS // S5_GB
S5_HALF = S5_GB * S5_STATE
S5_NCH = S5_HALF // 128
S5_SUB = 8

DIFF_HEADS = 8
DIFF_DH = D // (2 * DIFF_HEADS)

D_FF = 2816
N_EXPERTS = 8
ROPE_THETA = 10000.0
EPS = 1e-6

LOG2E = math.log2(math.e)
TM = 512
FF_TILE = 1408
LANES = 128
VMEM_LIMIT = 56 * 1024 * 1024

F32 = jnp.float32
BF16 = jnp.bfloat16


def _cparams(*sem):
    return pltpu.CompilerParams(dimension_semantics=sem, vmem_limit_bytes=VMEM_LIMIT)


def _group_of_tile(i, tm):
    n_p = N_P // tm
    per = DEC_SEQ // tm
    return jnp.where(i < n_p, 0, 1 + (i - n_p) // per)


def _rope_tile(i, tm):
    n_p = N_P // tm
    per = DEC_SEQ // tm
    return jnp.where(i < n_p, 0, 1 + (i - n_p) % per)


def _rms(x, g):
    return x * lax.rsqrt(jnp.mean(x * x, axis=-1, keepdims=True) + EPS) * g


def _normmod(x, g, mod, k_shift, k_scale):
    return _rms(x, g) * (1.0 + mod[k_scale:k_scale + 1, :]) + mod[k_shift:k_shift + 1, :]


def _sigmoid(x):
    return 1.0 / (1.0 + jnp.exp(-x))


def _dot(a, b):
    return jnp.dot(a, b, preferred_element_type=F32)


def _dot_nt(a, b):
    return lax.dot_general(a, b, (((1,), (1,)), ((), ())), preferred_element_type=F32)


def _rope(x, cos, sin):
    lane = lax.broadcasted_iota(jnp.int32, x.shape, 1)
    nxt = pltpu.roll(x, LANES - 16, 1)
    prv = pltpu.roll(x, 16, 1)
    swapped = jnp.where((lane % 32) < 16, nxt, prv)
    return x * cos + swapped * sin


def _ada_kernel(c_ref, w_ref, b_ref, o_ref):
    c = c_ref[...]
    s = (c * _sigmoid(c)).astype(BF16)
    o_ref[...] = _dot(s, w_ref[...].astype(BF16)) + b_ref[...]


def ada_all(cond8, ada_w, ada_b):
    tn = 1536
    return pl.pallas_call(
        _ada_kernel,
        grid=(DEPTH, 6 * D // tn),
        in_specs=[pl.BlockSpec((8, D), lambda l, n: (0, 0)),
                  pl.BlockSpec((None, D, tn), lambda l, n: (l, 0, n)),
                  pl.BlockSpec((None, 1, tn), lambda l, n: (l, 0, n))],
        out_specs=pl.BlockSpec((None, 8, tn), lambda l, n: (l, 0, n)),
        out_shape=jax.ShapeDtypeStruct((DEPTH, 8, 6 * D), F32),
        compiler_params=_cparams("parallel", "parallel"),
        name="ada",
    )(cond8, ada_w, ada_b.reshape(DEPTH, 1, 6 * D))


def _mod_spec(tm):
    return pl.BlockSpec((None, 6, D), lambda i, *_: (_group_of_tile(i, tm), 0, 0))


def _mla_tok_kernel(x_ref, g_ref, mod_ref, w1_ref, qn_ref, wuq_ref, kvn_ref, cos_ref, sin_ref,
                    q_ref, ckv_ref, kr_ref):
    h = _normmod(x_ref[...], g_ref[...], mod_ref[...], 0, 1).astype(BF16)
    t1 = _dot(h, w1_ref[...])
    ql = _rms(t1[:, :MLA_Q_RANK], qn_ref[...]).astype(BF16)
    q = _dot(ql, wuq_ref[...]) * (MLA_DK ** -0.5 * LOG2E)
    c0 = MLA_Q_RANK
    ckv_ref[...] = _rms(t1[:, c0:c0 + MLA_KV_RANK], kvn_ref[...])
    cos = cos_ref[...]
    sin = sin_ref[...]
    kr = _rope(t1[:, c0 + MLA_KV_RANK:c0 + MLA_KV_RANK + LANES], cos, sin)
    kr_ref[...] = kr[:, :MLA_D_ROPE]
    n_nope = MLA_HEADS * MLA_D_NOPE
    for pair in range(MLA_HEADS // 2):
        qr = _rope(q[:, n_nope + pair * LANES:n_nope + (pair + 1) * LANES], cos, sin).astype(BF16)
        for sub in range(2):
            hd = 2 * pair + sub
            q_ref[hd, :, 0:MLA_D_NOPE] = q[:, hd * MLA_D_NOPE:(hd + 1) * MLA_D_NOPE].astype(BF16)
            q_ref[hd, :, MLA_D_NOPE:MLA_DK] = qr[:, sub * MLA_D_ROPE:(sub + 1) * MLA_D_ROPE]


def mla_tokens(x, g, mod, w1, qn, wuq, kvn, cos_t, sin_t):
    nt = N_TOK // TM
    const = lambda shape: pl.BlockSpec(shape, lambda i: (0,) * len(shape))
    return pl.pallas_call(
        _mla_tok_kernel,
        grid=(nt,),
        in_specs=[pl.BlockSpec((TM, D), lambda i: (i, 0)), const((1, D)), _mod_spec(TM),
                  const(w1.shape), const((1, MLA_Q_RANK)), const(wuq.shape), const((1, MLA_KV_RANK)),
                  pl.BlockSpec((TM, LANES), lambda i: (_rope_tile(i, TM), 0)),
                  pl.BlockSpec((TM, LANES), lambda i: (_rope_tile(i, TM), 0))],
        out_specs=[pl.BlockSpec((MLA_HEADS, TM, MLA_DK), lambda i: (0, i, 0)),
                   pl.BlockSpec((TM, MLA_KV_RANK), lambda i: (i, 0)),
                   pl.BlockSpec((TM, MLA_D_ROPE), lambda i: (i, 0))],
        out_shape=[jax.ShapeDtypeStruct((MLA_HEADS, N_TOK, MLA_DK), BF16),
                   jax.ShapeDtypeStruct((N_TOK, MLA_KV_RANK), F32),
                   jax.ShapeDtypeStruct((N_TOK, MLA_D_ROPE), F32)],
        compiler_params=_cparams("parallel"),
        name="mla_tokens",
    )(x, g, mod, w1, qn, wuq, kvn, cos_t, sin_t)


def _mla_kv_kernel(ckv_ref, kr_ref, w_ref, k_ref, v_ref):
    kv = _dot(ckv_ref[...].astype(BF16), w_ref[...])
    kr = kr_ref[...].astype(BF16)
    n_nope = MLA_HEADS * MLA_D_NOPE
    for hd in range(MLA_HEADS):
        k_ref[hd, :, 0:MLA_D_NOPE] = kv[:, hd * MLA_D_NOPE:(hd + 1) * MLA_D_NOPE].astype(BF16)
        k_ref[hd, :, MLA_D_NOPE:MLA_DK] = kr
        v_ref[hd] = kv[:, n_nope + hd * MLA_D_V:n_nope + (hd + 1) * MLA_D_V].astype(BF16)


def mla_kv(ckv, kr, wukv, ts):
    nb, s, _ = ckv.shape
    return pl.pallas_call(
        _mla_kv_kernel,
        grid=(nb, s // ts),
        in_specs=[pl.BlockSpec((None, ts, MLA_KV_RANK), lambda b, t: (b, t, 0)),
                  pl.BlockSpec((None, ts, MLA_D_ROPE), lambda b, t: (b, t, 0)),
                  pl.BlockSpec(wukv.shape, lambda b, t: (0, 0))],
        out_specs=[pl.BlockSpec((None, MLA_HEADS, ts, MLA_DK), lambda b, t: (b, 0, t, 0)),
                   pl.BlockSpec((None, MLA_HEADS, ts, MLA_D_V), lambda b, t: (b, 0, t, 0))],
        out_shape=[jax.ShapeDtypeStruct((nb, MLA_HEADS, s, MLA_DK), BF16),
                   jax.ShapeDtypeStruct((nb, MLA_HEADS, s, MLA_D_V), BF16)],
        compiler_params=_cparams("parallel", "parallel"),
        name="mla_kv",
    )(ckv, kr, wukv)


def _mla_attn_kernel(hps, q_ref, k_ref, v_ref, o_ref):
    for hd in range(hps):
        s = _dot_nt(q_ref[hd], k_ref[hd])
        p = jnp.exp2(s - jnp.max(s, axis=-1, keepdims=True))
        l = jnp.sum(p, axis=-1, keepdims=True)
        o = _dot(p.astype(BF16), v_ref[hd]) / l
        o_ref[:, hd * MLA_D_V:(hd + 1) * MLA_D_V] = o.astype(BF16)


def mla_attention(q3, k3, v3, row0, seq, tq, hps):
    nb, _, s, _ = k3.shape
    nq = seq // tq
    base = row0 // tq
    return pl.pallas_call(
        functools.partial(_mla_attn_kernel, hps),
        grid=(nb, MLA_HEADS // hps, nq),
        in_specs=[pl.BlockSpec((hps, tq, MLA_DK), lambda b, h, i: (h, base + b * nq + i, 0)),
                  pl.BlockSpec((None, hps, s, MLA_DK), lambda b, h, i: (b, h, 0, 0)),
                  pl.BlockSpec((None, hps, s, MLA_D_V), lambda b, h, i: (b, h, 0, 0))],
        out_specs=pl.BlockSpec((tq, hps * MLA_D_V), lambda b, h, i: (b * nq + i, h)),
        out_shape=jax.ShapeDtypeStruct((nb * seq, MLA_HEADS * MLA_D_V), BF16),
        compiler_params=_cparams("parallel", "parallel", "parallel"),
        name="mla_attn_%d" % s,
    )(q3, k3, v3)


def _proj_res_kernel(op_ref, os_ref, w_ref, x_ref, mod_ref, out_ref):
    o = jnp.where(pl.program_id(0) < N_P // TM, op_ref[...], os_ref[...])
    out_ref[...] = x_ref[...] + mod_ref[2:3, :] * _dot(o, w_ref[...])


def proj_residual(o_p, o_s, w, x, mod):
    nt = N_TOK // TM
    n_p = N_P // TM
    return pl.pallas_call(
        _proj_res_kernel,
        grid=(nt,),
        in_specs=[pl.BlockSpec((TM, D), lambda i: (jnp.minimum(i, n_p - 1), 0)),
                  pl.BlockSpec((TM, D), lambda i: (jnp.maximum(i - n_p, 0), 0)),
                  pl.BlockSpec((D, D), lambda i: (0, 0)),
                  pl.BlockSpec((TM, D), lambda i: (i, 0)), _mod_spec(TM)],
        out_specs=pl.BlockSpec((TM, D), lambda i: (i, 0)),
        out_shape=jax.ShapeDtypeStruct((N_TOK, D), F32),
        compiler_params=_cparams("parallel"),
        name="proj_residual",
    )(o_p, o_s, w, x, mod)


def _diff_tok_kernel(latent, x_ref, g_ref, mod_ref, w_ref, *refs):
    if latent:
        cos_ref, sin_ref, q_ref, kc_ref, vc_ref = refs
        cos = cos_ref[...]
        sin = sin_ref[...]
        rot = lambda t: _rope(t, cos, sin)
    else:
        q_ref, kc_ref, vc_ref, k_ref, v_ref = refs
        rot = lambda t: t
    h = _normmod(x_ref[...], g_ref[...], mod_ref[...], 0, 1).astype(BF16)
    for c in range(D // LANES):
        sl = slice(c * LANES, (c + 1) * LANES)
        q = _dot(h, w_ref[:, c * LANES:(c + 1) * LANES]) * (DIFF_DH ** -0.5 * LOG2E)
        k = _dot(h, w_ref[:, D + c * LANES:D + (c + 1) * LANES])
        v = _dot(h, w_ref[:, 2 * D + c * LANES:2 * D + (c + 1) * LANES])
        q_ref[:, sl] = rot(q).astype(BF16)
        kc_ref[:, sl] = rot(k).astype(BF16)
        vc_ref[:, sl] = v.astype(BF16)
        if not latent:
            k_ref[:, sl] = k
            v_ref[:, sl] = v


def diff_tokens(x, g, mod, wqkv, cos_t, sin_t, latent):
    n_rows = N_S if latent else N_P
    base = (N_P if latent else 0) // TM
    row = lambda i: (i, 0)
    in_specs = [pl.BlockSpec((TM, D), lambda i: (base + i, 0)), pl.BlockSpec((1, D), lambda i: (0, 0)),
                pl.BlockSpec((None, 6, D), lambda i: (_group_of_tile(base + i, TM), 0, 0)),
                pl.BlockSpec((D, 3 * D), lambda i: (0, 0))]
    args = [x, g, mod, wqkv]
    out_shape = [jax.ShapeDtypeStruct((n_rows, D), BF16)] * 3
    if latent:
        rope_spec = pl.BlockSpec((TM, LANES), lambda i: (_rope_tile(base + i, TM), 0))
        in_specs += [rope_spec, rope_spec]
        args += [cos_t, sin_t]
    else:
        out_shape += [jax.ShapeDtypeStruct((n_rows, D), F32)] * 2
    return pl.pallas_call(
        functools.partial(_diff_tok_kernel, latent),
        grid=(n_rows // TM,),
        in_specs=in_specs,
        out_specs=[pl.BlockSpec((TM, D), row)] * len(out_shape),
        out_shape=out_shape,
        compiler_params=_cparams("parallel"),
        name="diff_tokens_lat" if latent else "diff_tokens_ctx",
    )(*args)


def _diff_attn_kernel(lam_init, n_kv, pairs, lq1_ref, lk1_ref, lq2_ref, lk2_ref, sub_ref, q_ref, *refs):
    kv_refs = refs[:2 * n_kv]
    o_ref = refs[2 * n_kv]
    lam = (jnp.exp(jnp.sum(lq1_ref[...] * lk1_ref[...], axis=-1, keepdims=True))
           - jnp.exp(jnp.sum(lq2_ref[...] * lk2_ref[...], axis=-1, keepdims=True)) + lam_init)
    for pr in range(pairs):
        lanes = slice(pr * LANES, (pr + 1) * LANES)
        q = q_ref[:, lanes]
        lane = lax.broadcasted_iota(jnp.int32, q.shape, 1)
        zero = jnp.zeros_like(q)

        def probs(qh):
            s = jnp.concatenate([_dot_nt(qh, kv_refs[2 * i][:, lanes]) for i in range(n_kv)], axis=1)
            p = jnp.exp2(s - jnp.max(s, axis=-1, keepdims=True))
            return p, jnp.sum(p, axis=-1, keepdims=True)

        p1, l1 = probs(jnp.where(lane < DIFF_DH, q, zero))
        p2, l2 = probs(jnp.where(lane >= DIFF_DH, q, zero))
        att = (p1 + (-lam * l1 / l2) * p2).astype(BF16)
        acc = jnp.zeros(q.shape, F32)
        col = 0
        for i in range(n_kv):
            n = kv_refs[2 * i + 1].shape[0]
            acc = acc + _dot(att[:, col:col + n], kv_refs[2 * i + 1][:, lanes])
            col += n
        o = acc / l1
        o_ref[:, lanes] = (_rms(o, sub_ref[...]) * (1.0 - lam_init)).astype(BF16)


def diff_attention(lam_init, lvecs, subln, q, kvs, seq, tq, pairs):
    nb = kvs[0][0].shape[0]
    nq = seq // tq
    n_keys = sum(k.shape[1] for k, _ in kvs)
    width = pairs * LANES
    vec = pl.BlockSpec((1, DIFF_DH), lambda b, h, i: (0, 0))
    kv_specs, kv_args = [], []
    for k, v in kvs:
        spec = pl.BlockSpec((None, k.shape[1], width), lambda b, h, i: (b, 0, h))
        kv_specs += [spec, spec]
        kv_args += [k, v]
    return pl.pallas_call(
        functools.partial(_diff_attn_kernel, lam_init, len(kvs), pairs),
        grid=(nb, DIFF_HEADS // pairs, nq),
        in_specs=[vec, vec, vec, vec, pl.BlockSpec((1, 2 * DIFF_DH), lambda b, h, i: (0, 0)),
                  pl.BlockSpec((tq, width), lambda b, h, i: (b * nq + i, h))] + kv_specs,
        out_specs=pl.BlockSpec((tq, width), lambda b, h, i: (b * nq + i, h)),
        out_shape=jax.ShapeDtypeStruct((nb * seq, D), BF16),
        compiler_params=_cparams("parallel", "parallel", "parallel"),
        name="diff_attn_%d" % n_keys,
    )(*lvecs, subln, q, *kv_args)


def _normmod_kernel(x_ref, g_ref, mod_ref, h_ref):
    h_ref[...] = _normmod(x_ref[...], g_ref[...], mod_ref[...], 0, 1)


S5_T = SEQ
S5_SEG = DEC_SEQ // S5_SUB
S5_TILES_P = N_P // S5_T
S5_PER_SEG = S5_SEG // S5_T


def _s5_tile_pos(i):
    k = i - S5_TILES_P
    per_batch = S5_SUB * S5_PER_SEG
    lat_row = S5_TILES_P // S5_SUB + (k // per_batch) * S5_PER_SEG + k % S5_PER_SEG
    lat_col = (k % per_batch) // S5_PER_SEG
    is_p = i < S5_TILES_P
    return jnp.where(is_p, i // S5_SUB, lat_row), jnp.where(is_p, i % S5_SUB, lat_col)


def normmod_time_major(x, g, mod):
    nt = N_TOK // S5_T
    return pl.pallas_call(
        _normmod_kernel,
        grid=(nt,),
        in_specs=[pl.BlockSpec((S5_T, D), lambda i: (i, 0)), pl.BlockSpec((1, D), lambda i: (0, 0)),
                  _mod_spec(S5_T)],
        out_specs=pl.BlockSpec((S5_T, D), lambda i: _s5_tile_pos(i)),
        out_shape=jax.ShapeDtypeStruct((N_TOK // S5_SUB, S5_SUB * D), F32),
        compiler_params=_cparams("parallel"),
        name="normmod",
    )(x, g, mod)


def _s5_kernel(chained, n, u_ref, wb_ref, wc_ref, a_ref, an_ref, dsk_ref, h0_ref, y_ref, fin_ref,
               bu_ref, ini_ref):
    d = pl.program_id(2)
    rows = S5_SUB * n
    chunk = 512
    nch = S5_NCH
    for r in range(rows // chunk):
        rs = slice(r * chunk, (r + 1) * chunk)
        bu = _dot(u_ref[rs, :].astype(BF16), wb_ref[...])
        for c in range(2 * nch):
            bu_ref[c, rs, :] = bu[:, c * LANES:(c + 1) * LANES]
    ar = [jnp.broadcast_to(a_ref[0:1, c * LANES:(c + 1) * LANES], (S5_SUB, LANES)) for c in range(nch)]
    ai = [jnp.broadcast_to(a_ref[1:2, c * LANES:(c + 1) * LANES], (S5_SUB, LANES)) for c in range(nch)]

    def step_index(s):
        return jnp.where(d == 0, s, n - 1 - s)

    def step_rows(t):
        return pl.ds(pl.multiple_of(t * S5_SUB, S5_SUB), S5_SUB)

    def advance(h, t):
        out = [None] * (2 * nch)
        for c in range(nch):
            br = bu_ref[c, step_rows(t), :]
            bi = bu_ref[nch + c, step_rows(t), :]
            out[c] = ar[c] * h[c] - ai[c] * h[nch + c] + br
            out[nch + c] = ar[c] * h[nch + c] + ai[c] * h[c] + bi
        return out

    unroll = 4
    zeros = [jnp.zeros((S5_SUB, LANES), F32) for _ in range(2 * nch)]

    if chained:
        def local_body(s, h):
            h = list(h)
            for k in range(unroll):
                h = advance(h, step_index(s * unroll + k))
            return tuple(h)

        ends = lax.fori_loop(0, n // unroll, local_body, tuple(zeros))
        sub_row = lax.broadcasted_iota(jnp.int32, (S5_SUB, LANES), 0)
        cur = [h0_ref[:, c * LANES:(c + 1) * LANES] for c in range(2 * nch)]
        anr = [an_ref[0:1, c * LANES:(c + 1) * LANES] for c in range(nch)]
        ani = [an_ref[1:2, c * LANES:(c + 1) * LANES] for c in range(nch)]
        for kk in range(S5_SUB):
            j = jnp.where(d == 0, kk, S5_SUB - 1 - kk)
            nxt = [None] * (2 * nch)
            for c in range(2 * nch):
                ini_ref[c, pl.ds(j, 1), :] = cur[c]
            for c in range(nch):
                er = jnp.sum(jnp.where(sub_row == j, ends[c], 0.0), axis=0, keepdims=True)
                ei = jnp.sum(jnp.where(sub_row == j, ends[nch + c], 0.0), axis=0, keepdims=True)
                nxt[c] = er + anr[c] * cur[c] - ani[c] * cur[nch + c]
                nxt[nch + c] = ei + anr[c] * cur[nch + c] + ani[c] * cur[c]
            cur = nxt
        h_init = [ini_ref[c] for c in range(2 * nch)]
    else:
        h_init = zeros

    def body(s, h):
        h = list(h)
        for k in range(unroll):
            t = step_index(s * unroll + k)
            h = advance(h, t)
            for c in range(2 * nch):
                bu_ref[c, step_rows(t), :] = h[c]
        return tuple(h)

    fin = lax.fori_loop(0, n // unroll, body, tuple(h_init))
    for c in range(2 * nch):
        fin_ref[:, c * LANES:(c + 1) * LANES] = fin[c]

    @pl.when(d == 0)
    def _():
        y_ref[...] = dsk_ref[...] * u_ref[...]

    for r in range(rows // chunk):
        rs = slice(r * chunk, (r + 1) * chunk)
        hs = jnp.concatenate([bu_ref[c, rs, :].astype(BF16) for c in range(2 * nch)], axis=1)
        y_ref[rs, :] += _dot(hs, wc_ref[...])


def s5_scan(h, wb, wc, a, an, dskip, h0, row0, n_blocks, n, chained):
    rows = S5_SUB * n
    base = row0 // rows
    kern = functools.partial(_s5_kernel, chained, n)
    return pl.pallas_call(
        kern,
        grid=(n_blocks, S5_NGB, 2),
        in_specs=[pl.BlockSpec((rows, LANES), lambda r, c, d: (base + r, c)),
                  pl.BlockSpec((None, None, LANES, 2 * S5_HALF), lambda r, c, d: (d, c, 0, 0)),
                  pl.BlockSpec((None, None, 2 * S5_HALF, LANES), lambda r, c, d: (d, c, 0, 0)),
                  pl.BlockSpec((None, None, 2, S5_HALF), lambda r, c, d: (d, c, 0, 0)),
                  pl.BlockSpec((None, None, 2, S5_HALF), lambda r, c, d: (d, c, 0, 0)),
                  pl.BlockSpec((1, LANES), lambda r, c, d: (0, c)),
                  pl.BlockSpec((None, None, None, 1, 2 * S5_HALF), lambda r, c, d: (r, d, c, 0, 0))],
        out_specs=[pl.BlockSpec((rows, LANES), lambda r, c, d: (r, c)),
                   pl.BlockSpec((None, None, None, S5_SUB, 2 * S5_HALF), lambda r, c, d: (r, d, c, 0, 0))],
        out_shape=[jax.ShapeDtypeStruct((n_blocks * rows, D), F32),
                   jax.ShapeDtypeStruct((n_blocks, 2, S5_NGB, S5_SUB, 2 * S5_HALF), F32)],
        scratch_shapes=[pltpu.VMEM((2 * S5_NCH, rows, LANES), F32),
                        pltpu.VMEM((2 * S5_NCH, S5_SUB, LANES), F32)],
        compiler_params=_cparams("parallel", "parallel", "arbitrary"),
        name="s5_scan_%d" % n,
    )(h, wb, wc, a, an, dskip, h0)


def _glu_res_kernel(yp_ref, ys_ref, w_ref, x_ref, mod_ref, out_ref):
    y = jnp.where(pl.program_id(0) < S5_TILES_P, yp_ref[...], ys_ref[...])
    g = 0.5 * y * (1.0 + jnp.tanh(math.sqrt(2.0 / math.pi) * (y + 0.044715 * (y * y * y))))
    t = _dot(g.astype(BF16), w_ref[...])
    out_ref[...] = x_ref[...] + mod_ref[2:3, :] * (t[:, :D] * _sigmoid(t[:, D:]))


def glu_residual(y_p, y_s, w, x, mod):
    nt = N_TOK // S5_T
    n_blk_p = S5_TILES_P // S5_SUB

    def yp_map(i):
        r, c = _s5_tile_pos(jnp.minimum(i, S5_TILES_P - 1))
        return r, c

    def ys_map(i):
        r, c = _s5_tile_pos(jnp.maximum(i, S5_TILES_P))
        return r - n_blk_p, c

    return pl.pallas_call(
        _glu_res_kernel,
        grid=(nt,),
        in_specs=[pl.BlockSpec((S5_T, D), yp_map), pl.BlockSpec((S5_T, D), ys_map),
                  pl.BlockSpec((D, 2 * D), lambda i: (0, 0)),
                  pl.BlockSpec((S5_T, D), lambda i: (i, 0)), _mod_spec(S5_T)],
        out_specs=pl.BlockSpec((S5_T, D), lambda i: (i, 0)),
        out_shape=jax.ShapeDtypeStruct((N_TOK, D), F32),
        compiler_params=_cparams("parallel"),
        name="glu_residual",
    )(y_p.reshape(N_P // S5_SUB, S5_SUB * D), y_s.reshape(N_S // S5_SUB, S5_SUB * D), w, x, mod)


def _ffn_kernel(x_ref, g_ref, mod_ref, wa_ref, wb_ref, wo_ref, out_ref, h_scr, acc_scr):
    f = pl.program_id(1)

    @pl.when(f == 0)
    def _():
        h_scr[...] = _normmod(x_ref[...], g_ref[...], mod_ref[...], 3, 4).astype(BF16)
        acc_scr[...] = jnp.zeros_like(acc_scr)

    h = h_scr[...]
    a = _dot(h, wa_ref[...])
    b = _dot(h, wb_ref[...])
    act = (a * _sigmoid(a) * b).astype(BF16)
    acc_scr[...] += _dot(act, wo_ref[...])

    @pl.when(f == pl.num_programs(1) - 1)
    def _():
        out_ref[...] = x_ref[...] + mod_ref[5:6, :] * acc_scr[...]


def ffn_residual(x, g, mod, w_in, w_out, layer):
    nt = N_TOK // TM
    nf = D_FF // FF_TILE
    return pl.pallas_call(
        _ffn_kernel,
        grid=(nt, nf),
        in_specs=[pl.BlockSpec((TM, D), lambda i, f: (i, 0)), pl.BlockSpec((1, D), lambda i, f: (0, 0)),
                  _mod_spec(TM),
                  pl.BlockSpec((None, D, FF_TILE), lambda i, f: (layer, 0, f)),
                  pl.BlockSpec((None, D, FF_TILE), lambda i, f: (layer, 0, f + nf)),
                  pl.BlockSpec((None, FF_TILE, D), lambda i, f: (layer, f, 0))],
        out_specs=pl.BlockSpec((TM, D), lambda i, f: (i, 0)),
        out_shape=jax.ShapeDtypeStruct((N_TOK, D), F32),
        scratch_shapes=[pltpu.VMEM((TM, D), BF16), pltpu.VMEM((TM, D), F32)],
        compiler_params=_cparams("parallel", "arbitrary"),
        name="ffn",
    )(x, g, mod, w_in, w_in, w_out)


TS = 512
N_SLOTS = 2 * N_TOK + N_EXPERTS * TS
NT_S = N_SLOTS // TS
TG = 256
PIECE = 256
MAX_PIECES = (TM + 16 + PIECE - 1) // PIECE


def _moe_route_kernel(x_ref, g_ref, mod_ref, wr_ref, br_ref, h_ref, route_ref, rows_ref, run_ref, carry_scr):
    @pl.when(pl.program_id(0) == 0)
    def _():
        carry_scr[...] = jnp.zeros_like(carry_scr)

    h = _normmod(x_ref[...], g_ref[...], mod_ref[...], 3, 4).astype(BF16)
    h_ref[...] = h
    logits = _dot(h, wr_ref[...]) + br_ref[...]
    lane = lax.broadcasted_iota(jnp.int32, logits.shape, 1)
    neg = jnp.float32(-jnp.inf)
    lg = jnp.where(lane < N_EXPERTS, logits, neg)
    v1 = jnp.max(lg, axis=-1, keepdims=True)
    i1 = jnp.min(jnp.where(lg == v1, lane, LANES), axis=-1, keepdims=True)
    lg2 = jnp.where(lane == i1, neg, lg)
    v2 = jnp.max(lg2, axis=-1, keepdims=True)
    i2 = jnp.min(jnp.where(lg2 == v2, lane, LANES), axis=-1, keepdims=True)
    e2 = jnp.exp(v2 - v1)
    g1 = 1.0 / (1.0 + e2)
    g2 = e2 / (1.0 + e2)
    oh1 = lane == i1
    oh2 = lane == i2
    sel = jnp.where(oh1 | oh2, 1.0, 0.0)
    r = lax.broadcasted_iota(jnp.int32, (TM, TM), 0)
    c = lax.broadcasted_iota(jnp.int32, (TM, TM), 1)
    tri = jnp.where(c < r, 1.0, 0.0).astype(BF16)
    rank = _dot(tri, sel.astype(BF16)) + carry_scr[0:1, :]
    r1 = jnp.sum(jnp.where(oh1, rank, 0.0), axis=-1, keepdims=True)
    r2 = jnp.sum(jnp.where(oh2, rank, 0.0), axis=-1, keepdims=True)
    cols = (i1.astype(F32), i2.astype(F32), g1, g2, r1, r2)
    route = jnp.zeros(logits.shape, F32)
    for k, v in enumerate(cols):
        route = jnp.where(lane == k, v, route)
    route_ref[...] = route
    rows_ref[...] = route.T[0:8, :]
    carry_scr[...] = carry_scr[...] + jnp.sum(sel, axis=0, keepdims=True)
    run_ref[...] = carry_scr[...]


def moe_route(x, g, mod, w_router, b_router):
    nt = N_TOK // TM
    return pl.pallas_call(
        _moe_route_kernel,
        grid=(nt,),
        in_specs=[pl.BlockSpec((TM, D), lambda i: (i, 0)), pl.BlockSpec((1, D), lambda i: (0, 0)), _mod_spec(TM),
                  pl.BlockSpec((D, LANES), lambda i: (0, 0)), pl.BlockSpec((1, LANES), lambda i: (0, 0))],
        out_specs=[pl.BlockSpec((TM, D), lambda i: (i, 0)), pl.BlockSpec((TM, LANES), lambda i: (i, 0)),
                   pl.BlockSpec((8, TM), lambda i: (0, i)),
                   pl.BlockSpec((None, 8, LANES), lambda i: (i, 0, 0))],
        out_shape=[jax.ShapeDtypeStruct((N_TOK, D), BF16), jax.ShapeDtypeStruct((N_TOK, LANES), F32),
                   jax.ShapeDtypeStruct((8, N_TOK), F32),
                   jax.ShapeDtypeStruct((nt, 8, LANES), F32)],
        scratch_shapes=[pltpu.VMEM((8, LANES), F32)],
        compiler_params=_cparams("arbitrary"),
        name="moe_route",
    )(x, g, mod, w_router, b_router)


def _moe_gather_kernel(used_ref, clo_ref, chi_ref, pos_ref, gates_ref, h_ref, xs_ref, gate_ref, gat_scr, gsum_scr):
    i = pl.program_id(0)
    gat_scr[...] = jnp.zeros_like(gat_scr)
    gsum_scr[...] = jnp.zeros_like(gsum_scr)

    @pl.when(used_ref[i] > 0)
    def _():
        slot = i * TG + lax.broadcasted_iota(jnp.int32, (TG, TM), 0)

        def body(c, carry):
            base = pl.multiple_of(c * TM, TM)
            m1 = pos_ref[0:1, pl.ds(base, TM)] == slot
            m2 = pos_ref[1:2, pl.ds(base, TM)] == slot
            pick = jnp.where(m1 | m2, 1.0, 0.0).astype(BF16)
            gat_scr[...] += _dot(pick, h_ref[pl.ds(base, TM), :])
            g = jnp.where(m1, gates_ref[0:1, pl.ds(base, TM)], 0.0) + jnp.where(m2, gates_ref[1:2, pl.ds(base, TM)], 0.0)
            gsum_scr[...] += jnp.sum(g, axis=-1, keepdims=True)
            return carry

        lax.fori_loop(clo_ref[i], chi_ref[i] + 1, body, 0)

    xs_ref[...] = gat_scr[...].astype(BF16)
    gate_ref[...] = gsum_scr[...]


def moe_gather(used, clo, chi, pos_rows, gate_rows, h_b):
    whole = lambda shape: pl.BlockSpec(shape, lambda i, *_: (0, 0), pipeline_mode=pl.Buffered(1))
    grid_spec = pltpu.PrefetchScalarGridSpec(
        num_scalar_prefetch=3,
        grid=(N_SLOTS // TG,),
        in_specs=[whole((8, N_TOK)), whole((8, N_TOK)), whole((N_TOK, D))],
        out_specs=[pl.BlockSpec((TG, D), lambda i, *_: (i, 0)), pl.BlockSpec((TG, 1), lambda i, *_: (i, 0))],
        scratch_shapes=[pltpu.VMEM((TG, D), F32), pltpu.VMEM((TG, 1), F32)],
    )
    return pl.pallas_call(
        _moe_gather_kernel,
        grid_spec=grid_spec,
        out_shape=[jax.ShapeDtypeStruct((N_SLOTS, D), BF16), jax.ShapeDtypeStruct((N_SLOTS, 1), F32)],
        compiler_params=_cparams("arbitrary"),
        name="moe_gather",
    )(used, clo, chi, pos_rows, gate_rows, h_b)


def _moe_ffn_kernel(texp_ref, used_ref, xs_ref, gate_ref, wa_ref, wb_ref, wo_ref, ys_ref, acc_scr):
    i = pl.program_id(0)
    f = pl.program_id(1)
    live = used_ref[i] > 0

    @pl.when(f == 0)
    def _():
        acc_scr[...] = jnp.zeros_like(acc_scr)

    @pl.when(live)
    def _():
        xs = xs_ref[...]
        a = _dot(xs, wa_ref[...])
        b = _dot(xs, wb_ref[...])
        act = (a * _sigmoid(a) * b).astype(BF16)
        acc_scr[...] += _dot(act, wo_ref[...])

    @pl.when(f == pl.num_programs(1) - 1)
    def _():
        ys_ref[...] = (gate_ref[...] * acc_scr[...]).astype(BF16)


def moe_ffn(texp, used, xs, gate_col, w_in, w_out, layer):
    nf = D_FF // FF_TILE
    grid_spec = pltpu.PrefetchScalarGridSpec(
        num_scalar_prefetch=2,
        grid=(NT_S, nf),
        in_specs=[pl.BlockSpec((TS, D), lambda i, f, *_: (i, 0)),
                  pl.BlockSpec((TS, 1), lambda i, f, *_: (i, 0)),
                  pl.BlockSpec((None, None, D, FF_TILE), lambda i, f, texp, *_: (layer, texp[i], 0, f)),
                  pl.BlockSpec((None, None, D, FF_TILE), lambda i, f, texp, *_: (layer, texp[i], 0, f + nf)),
                  pl.BlockSpec((None, None, FF_TILE, D), lambda i, f, texp, *_: (layer, texp[i], f, 0))],
        out_specs=pl.BlockSpec((TS, D), lambda i, f, *_: (i, 0)),
        scratch_shapes=[pltpu.VMEM((TS, D), F32)],
    )
    return pl.pallas_call(
        _moe_ffn_kernel,
        grid_spec=grid_spec,
        out_shape=jax.ShapeDtypeStruct((N_SLOTS, D), BF16),
        compiler_params=_cparams("parallel", "arbitrary"),
        name="moe_ffn",
    )(texp, used, xs, gate_col, w_in, w_in, w_out)


def _moe_combine_kernel(start_ref, npc_ref, lo_ref, hi_ref, off_ref, x_ref, mod_ref, route_ref, ys_hbm, out_ref,
                        buf, acc_scr, sem):
    i = pl.program_id(0)
    n_tiles = pl.num_programs(0)

    def piece_copy(tile, e, k):
        half = tile % 2
        s = pl.multiple_of(start_ref[tile * N_EXPERTS + e] + k * PIECE, 16)
        return pltpu.make_async_copy(ys_hbm.at[pl.ds(s, PIECE), :], buf.at[half, e, pl.ds(k * PIECE, PIECE), :],
                                     sem.at[half, e, k])

    def start_tile(tile):
        for e in range(N_EXPERTS):
            for k in range(MAX_PIECES):
                @pl.when(k < npc_ref[tile * N_EXPERTS + e])
                def _():
                    piece_copy(tile, e, k).start()

    @pl.when(i == 0)
    def _():
        start_tile(i)

    @pl.when(i + 1 < n_tiles)
    def _():
        start_tile(i + 1)

    half = i % 2
    acc_scr[...] = jnp.zeros_like(acc_scr)
    route = route_ref[...]
    e1 = route[:, 0:1].astype(jnp.int32)
    e2 = route[:, 1:2].astype(jnp.int32)
    pos1 = route[:, 4:5].astype(jnp.int32)
    pos2 = route[:, 5:6].astype(jnp.int32)
    for e in range(N_EXPERTS):
        pos1 = pos1 + jnp.where(e1 == e, off_ref[e], 0)
        pos2 = pos2 + jnp.where(e2 == e, off_ref[e], 0)
    lane = lax.broadcasted_iota(jnp.int32, (TM, PIECE), 1)
    for e in range(N_EXPERTS):
        lo = lo_ref[i * N_EXPERTS + e]
        hi = hi_ref[i * N_EXPERTS + e]
        p1 = jnp.where((pos1 >= lo) & (pos1 < hi), pos1, -1)
        p2 = jnp.where((pos2 >= lo) & (pos2 < hi), pos2, -1)
        for k in range(MAX_PIECES):
            @pl.when(k < npc_ref[i * N_EXPERTS + e])
            def _():
                piece_copy(i, e, k).wait()
                base = start_ref[i * N_EXPERTS + e] + k * PIECE
                pick = jnp.where((p1 - base == lane) | (p2 - base == lane), 1.0, 0.0).astype(BF16)
                acc_scr[...] += _dot(pick, buf[half, e, pl.ds(k * PIECE, PIECE), :])

    out_ref[...] = x_ref[...] + mod_ref[5:6, :] * acc_scr[...]


def moe_combine(start, npc, lo, hi, off, x, mod, route, ys):
    nt = N_TOK // TM
    grid_spec = pltpu.PrefetchScalarGridSpec(
        num_scalar_prefetch=5,
        grid=(nt,),
        in_specs=[pl.BlockSpec((TM, D), lambda i, *_: (i, 0)),
                  pl.BlockSpec((None, 6, D), lambda i, *_: (_group_of_tile(i, TM), 0, 0)),
                  pl.BlockSpec((TM, LANES), lambda i, *_: (i, 0)),
                  pl.BlockSpec(memory_space=pl.ANY)],
        out_specs=pl.BlockSpec((TM, D), lambda i, *_: (i, 0)),
        scratch_shapes=[pltpu.VMEM((2, N_EXPERTS, MAX_PIECES * PIECE, D), BF16), pltpu.VMEM((TM, D), F32),
                        pltpu.SemaphoreType.DMA((2, N_EXPERTS, MAX_PIECES))],
    )
    return pl.pallas_call(
        _moe_combine_kernel,
        grid_spec=grid_spec,
        out_shape=jax.ShapeDtypeStruct((N_TOK, D), F32),
        compiler_params=_cparams("arbitrary"),
        name="moe_combine",
    )(start, npc, lo, hi, off, x, mod, route, ys)


def moe_residual(x, g, mod, w_router, b_router, w_in, w_out, layer):
    nt = N_TOK // TM
    i32 = jnp.int32
    h_b, route, rows, run = moe_route(x, g, mod, w_router, b_router)
    run = run[:, 0, :N_EXPERTS].astype(i32)
    run_prev = jnp.concatenate([jnp.zeros((1, N_EXPERTS), i32), run[:-1]], axis=0)
    total = run[-1]
    padded = (total + TS - 1) // TS * TS
    off_end = jnp.cumsum(padded)
    off = off_end - padded
    experts = jnp.arange(N_EXPERTS, dtype=i32)
    off_of = lambda e_row: jnp.sum(jnp.where(e_row[None, :].astype(i32) == experts[:, None], off[:, None], 0), axis=0)
    pos1 = rows[4].astype(i32) + off_of(rows[0])
    pos2 = rows[5].astype(i32) + off_of(rows[1])
    pos_rows = jnp.concatenate([pos1[None], pos2[None], jnp.full((6, N_TOK), -1, i32)], axis=0)
    gate_rows = jnp.concatenate([rows[2:4], jnp.zeros((6, N_TOK), F32)], axis=0)
    def tile_tables(tile):
        tile_start = jnp.arange(N_SLOTS // tile, dtype=i32) * tile
        texp = jnp.minimum(jnp.sum((tile_start[:, None] >= off_end[None, :]).astype(i32), axis=1), N_EXPERTS - 1)
        mine = texp[:, None] == experts[None, :]
        of_expert = lambda per_e: jnp.sum(jnp.where(mine, per_e[None, :], 0), axis=1)
        of_tile = lambda per_e: jnp.sum(jnp.where(mine[:, None, :], per_e[None], 0), axis=2)
        rank0 = tile_start - of_expert(off)
        rank1 = jnp.minimum(rank0 + tile, of_expert(total))
        used = rank0 < rank1
        clo = jnp.sum((of_tile(run) <= rank0[:, None]).astype(i32), axis=1)
        chi = jnp.sum((of_tile(run_prev) < rank1[:, None]).astype(i32), axis=1) - 1
        clo = jnp.where(used, jnp.minimum(clo, nt - 1), 0).astype(i32)
        chi = jnp.where(used, chi, -1).astype(i32)
        return texp, used.astype(i32), clo, chi

    _, used_g, clo, chi = tile_tables(TG)
    texp, used, _, _ = tile_tables(TS)
    xs, gate_col = moe_gather(used_g, clo, chi, pos_rows, gate_rows, h_b)
    ys = moe_ffn(texp, used, xs, gate_col, w_in, w_out, layer)
    lo = off[None, :] + run_prev
    hi = off[None, :] + run
    start = jnp.minimum(lo // 16 * 16, N_SLOTS - MAX_PIECES * PIECE)
    npc = jnp.where(hi > lo, (hi - start + PIECE - 1) // PIECE, 0)
    flat = lambda a: a.reshape(nt * N_EXPERTS).astype(i32)
    return moe_combine(flat(start), flat(npc), flat(lo), flat(hi), off.astype(i32), x, mod, route, ys)


def _final_norm_kernel(x_ref, g_ref, o_ref):
    o_ref[...] = _rms(x_ref[...], g_ref[...])


def final_norm(x, g, row0, n_rows):
    base = row0 // TM
    return pl.pallas_call(
        _final_norm_kernel,
        grid=(n_rows // TM,),
        in_specs=[pl.BlockSpec((TM, D), lambda i: (base + i, 0)), pl.BlockSpec((1, D), lambda i: (0, 0))],
        out_specs=pl.BlockSpec((TM, D), lambda i: (i, 0)),
        out_shape=jax.ShapeDtypeStruct((n_rows, D), F32),
        compiler_params=_cparams("parallel"),
        name="final_norm",
    )(x, g)


def _rope_tables():
    half = 16
    freqs = ROPE_THETA ** (-jnp.arange(half, dtype=F32) / half)
    t = jnp.arange(DEC_SEQ, dtype=jnp.int32)
    row = (t // GRID_W).astype(F32)[:, None] * freqs[None, :]
    col = (t % GRID_W).astype(F32)[:, None] * freqs[None, :]
    cos = jnp.concatenate([jnp.cos(row), jnp.cos(row), jnp.cos(col), jnp.cos(col)], axis=1)
    sin = jnp.concatenate([-jnp.sin(row), jnp.sin(row), -jnp.sin(col), jnp.sin(col)], axis=1)
    cos = jnp.concatenate([jnp.ones((TM, 64), F32), cos], axis=0)
    sin = jnp.concatenate([jnp.zeros((TM, 64), F32), sin], axis=0)
    return jnp.tile(cos, (1, 2)), jnp.tile(sin, (1, 2))


def _mla_weights(w_dq, w_uq, w_dkv, w_ukv, w_o):
    w1 = jnp.concatenate([w_dq, w_dkv, jnp.zeros((D, LANES - MLA_D_ROPE), F32)], axis=1).astype(BF16)
    uq = w_uq.reshape(MLA_Q_RANK, MLA_HEADS, MLA_DK)
    wuq = jnp.concatenate([uq[:, :, :MLA_D_NOPE].reshape(MLA_Q_RANK, -1),
                           uq[:, :, MLA_D_NOPE:].reshape(MLA_Q_RANK, -1)], axis=1).astype(BF16)
    ukv = w_ukv.reshape(MLA_KV_RANK, MLA_HEADS, MLA_D_NOPE + MLA_D_V)
    wukv = jnp.concatenate([ukv[:, :, :MLA_D_NOPE].reshape(MLA_KV_RANK, -1),
                            ukv[:, :, MLA_D_NOPE:].reshape(MLA_KV_RANK, -1)], axis=1).astype(BF16)
    return w1, wuq, wukv, w_o.astype(BF16)


def _s5_weights(a_re, a_im, log_dt, b_re, b_im, c_re, c_im, seg_len):
    dt = jnp.exp(log_dt)[..., None]
    mag = jnp.exp(a_re * dt)
    abar_re, abar_im = mag * jnp.cos(a_im * dt), mag * jnp.sin(a_im * dt)
    mag_n = jnp.exp(a_re * dt * seg_len)
    apow_re, apow_im = mag_n * jnp.cos(a_im * dt * seg_len), mag_n * jnp.sin(a_im * dt * seg_len)
    den = a_re * a_re + a_im * a_im
    coef_re = ((abar_re - 1.0) * a_re + abar_im * a_im) / den
    coef_im = (abar_im * a_re - (abar_re - 1.0) * a_im) / den
    bbar_re = coef_re[..., None] * b_re - coef_im[..., None] * b_im
    bbar_im = coef_re[..., None] * b_im + coef_im[..., None] * b_re
    eye = jnp.eye(S5_GB, dtype=F32)

    def in_block(m):
        m = m.reshape(2, S5_NGB, S5_GB, S5_STATE, S5_GROUP)
        return jnp.einsum('dbgpc,gh->dbgchp', m, eye).reshape(2, S5_NGB, LANES, S5_HALF)

    def out_block(m):
        m = m.reshape(2, S5_NGB, S5_GB, S5_GROUP, S5_STATE)
        return jnp.einsum('dbgcp,gh->dbgphc', m, eye).reshape(2, S5_NGB, S5_HALF, LANES)

    wb = jnp.concatenate([in_block(bbar_re), in_block(bbar_im)], axis=3).astype(BF16)
    wc = jnp.concatenate([out_block(c_re), out_block(-c_im)], axis=2).astype(BF16)
    lanes = lambda m: m.reshape(2, S5_NGB, 1, S5_HALF)
    a = jnp.concatenate([lanes(abar_re), lanes(abar_im)], axis=2)
    an = jnp.concatenate([lanes(apow_re), lanes(apow_im)], axis=2)
    return wb, wc, a, an


def kernel(x_prompt, x_sample, c, c_ctx, cache_mla_ckv, cache_mla_krope, state_s5_re, state_s5_im, cache_diff_k, cache_diff_v, ada_w, ada_b, norm_mix, norm_ffn, norm_final, mla_w_dq, mla_q_norm, mla_w_uq, mla_w_dkv, mla_kv_norm, mla_w_ukv, mla_w_o, s5_a_re, s5_a_im, s5_log_dt, s5_b_re, s5_b_im, s5_c_re, s5_c_im, s5_d, s5_w_glu, diff_w_qkv, diff_lq1, diff_lk1, diff_lq2, diff_lk2, diff_subln, diff_w_o, ffn_w_in, ffn_w_out, moe_w_router, moe_b_router, moe_w_in, moe_w_out):
    x = jnp.concatenate([x_prompt.reshape(N_P, D), x_sample.reshape(N_S, D)], axis=0)
    cond8 = jnp.concatenate([c_ctx[None], c, jnp.zeros((8 - N_GROUPS, D), F32)], axis=0)
    mods = ada_all(cond8, ada_w, ada_b).reshape(DEPTH, 8, 6, D)[:, :N_GROUPS]
    cos_t, sin_t = _rope_tables()

    ffn_w_in_b, ffn_w_out_b = ffn_w_in.astype(BF16), ffn_w_out.astype(BF16)
    moe_w_in_b, moe_w_out_b = moe_w_in.astype(BF16), moe_w_out.astype(BF16)
    new_ckv, new_kr, new_s5_re, new_s5_im, new_dk, new_dv = [], [], [], [], [], []
    for i in range(DEPTH):
        mod = mods[i]
        gmix = norm_mix[i].reshape(1, D)
        gffn = norm_ffn[i].reshape(1, D)
        j = i // 3
        kind = i % 3
        if kind == 0:
            w1, wuq, wukv, wo = _mla_weights(mla_w_dq[j], mla_w_uq[j], mla_w_dkv[j], mla_w_ukv[j], mla_w_o[j])
            q3, ckv, kr = mla_tokens(x, gmix, mod, w1, mla_q_norm[j].reshape(1, -1), wuq,
                                     mla_kv_norm[j].reshape(1, -1), cos_t, sin_t)
            ckv_p = ckv[:N_P].reshape(BATCH, SEQ, MLA_KV_RANK)
            kr_p = kr[:N_P].reshape(BATCH, SEQ, MLA_D_ROPE)
            new_ckv.append(ckv_p)
            new_kr.append(kr_p)
            ckv_s = jnp.concatenate([cache_mla_ckv[:, j], ckv[N_P:].reshape(DEC_BATCH, DEC_SEQ, -1)], axis=1)
            kr_s = jnp.concatenate([cache_mla_krope[:, j], kr[N_P:].reshape(DEC_BATCH, DEC_SEQ, -1)], axis=1)
            k3p, v3p = mla_kv(ckv_p, kr_p, wukv, SEQ)
            k3s, v3s = mla_kv(ckv_s, kr_s, wukv, 512)
            o_p = mla_attention(q3, k3p, v3p, 0, SEQ, SEQ, MLA_HEADS)
            o_s = mla_attention(q3, k3s, v3s, N_P, DEC_SEQ, 256, 2)
            x = proj_residual(o_p, o_s, wo, x, mod)
        elif kind == 1:
            h = normmod_time_major(x, gmix, mod).reshape(N_TOK, D)
            dsk = s5_d[j].reshape(1, D)
            seg = S5_SEG
            wb, wc, a, an = _s5_weights(s5_a_re[j], s5_a_im[j], s5_log_dt[j], s5_b_re[j], s5_b_im[j],
                                        s5_c_re[j], s5_c_im[j], seg)
            zero_h0 = jnp.zeros((BATCH // S5_SUB, 2, S5_NGB, 1, 2 * S5_HALF), F32)
            y_p, fin = s5_scan(h, wb, wc, a, an, dsk, zero_h0, 0, BATCH // S5_SUB, SEQ, False)
            fin = fin.reshape(BATCH // S5_SUB, 2, S5_NGB, S5_SUB, 2, S5_GB, S5_STATE)
            fin = jnp.transpose(fin, (0, 3, 1, 4, 2, 5, 6)).reshape(BATCH, 2, 2, S5_GROUPS, S5_STATE)
            new_s5_re.append(fin[:, :, 0])
            new_s5_im.append(fin[:, :, 1])
            h0 = jnp.stack([state_s5_re[:, j], state_s5_im[:, j]], axis=2)
            h0 = h0.reshape(DEC_BATCH, 2, 2, S5_NGB, S5_HALF)
            h0 = jnp.transpose(h0, (0, 1, 3, 2, 4)).reshape(DEC_BATCH, 2, S5_NGB, 1, 2 * S5_HALF)
            y_s, _ = s5_scan(h, wb, wc, a, an, dsk, h0, N_P, DEC_BATCH, seg, True)
            x = glu_residual(y_p, y_s, s5_w_glu[j].astype(BF16), x, mod)
        else:
            lam_init = 0.8 - 0.6 * math.exp(-0.3 * i)
            wqkv = diff_w_qkv[j].astype(BF16)
            q_p, k_p, v_p, k, v = diff_tokens(x, gmix, mod, wqkv, cos_t, sin_t, latent=False)
            q_s, k_s, v_s = diff_tokens(x, gmix, mod, wqkv, cos_t, sin_t, latent=True)
            new_dk.append(k.reshape(BATCH, SEQ, 2 * DIFF_HEADS, DIFF_DH))
            new_dv.append(v.reshape(BATCH, SEQ, DIFF_HEADS, 2 * DIFF_DH))
            lvecs = [a_.reshape(1, DIFF_DH) for a_ in (diff_lq1[j], diff_lk1[j], diff_lq2[j], diff_lk2[j])]
            subln = diff_subln[j].reshape(1, 2 * DIFF_DH)
            ctx_kv = [(k_p.reshape(BATCH, SEQ, D), v_p.reshape(BATCH, SEQ, D))]
            lat_kv = [(cache_diff_k[:, j].reshape(DEC_BATCH, PAST, D).astype(BF16),
                       cache_diff_v[:, j].reshape(DEC_BATCH, PAST, D).astype(BF16)),
                      (k_s.reshape(DEC_BATCH, DEC_SEQ, D), v_s.reshape(DEC_BATCH, DEC_SEQ, D))]
            o_p = diff_attention(lam_init, lvecs, subln, q_p, ctx_kv, SEQ, SEQ, DIFF_HEADS)
            o_s = diff_attention(lam_init, lvecs, subln, q_s, lat_kv, DEC_SEQ, 256, 1)
            x = proj_residual(o_p, o_s, diff_w_o[j].astype(BF16), x, mod)
        f = i // 2
        if i % 2 == 0:
            x = ffn_residual(x, gffn, mod, ffn_w_in_b, ffn_w_out_b, f)
        else:
            wr = jnp.concatenate([moe_w_router[f], jnp.zeros((D, LANES - N_EXPERTS), F32)], axis=1).astype(BF16)
            br = jnp.concatenate([moe_b_router[f], jnp.zeros((LANES - N_EXPERTS,), F32)]).reshape(1, LANES)
            x = moe_residual(x, gffn, mod, wr, br, moe_w_in_b, moe_w_out_b, f)
    y_p = final_norm(x, norm_final.reshape(1, D), 0, N_P)
    y_s = final_norm(x, norm_final.reshape(1, D), N_P, N_S)
    return (y_p.reshape(BATCH, SEQ, D), y_s.reshape(DEC_BATCH, DEC_SEQ, D),
            jnp.stack(new_ckv, axis=1), jnp.stack(new_kr, axis=1),
            jnp.stack(new_s5_re, axis=1), jnp.stack(new_s5_im, axis=1),
            jnp.stack(new_dk, axis=1), jnp.stack(new_dv, axis=1))
```

```python
import functools
import math

import jax
import jax.numpy as jnp
from jax import lax
from jax.experimental import pallas as pl
from jax.experimental.pallas import tpu as pltpu

D = 1024
BATCH = 16
SEQ = 256
DEPTH = 4
DEC_BATCH = 2
DEC_SEQ = 4096
PAST = 512
GRID_W = 64
N_P = BATCH * SEQ
N_S = DEC_BATCH * DEC_SEQ
N_TOK = N_P + N_S
N_GROUPS = 1 + DEC_BATCH

MLA_HEADS = 8
MLA_Q_RANK = 384
MLA_KV_RANK = 256
MLA_D_NOPE = 128
MLA_D_ROPE = 64
MLA_D_V = 128
MLA_DK = MLA_D_NOPE + MLA_D_ROPE

S5_GROUP = 16
S5_GROUPS = D // S5_GROUP
S5_STATE = 64
S5_GB = 8
S5_NGB = S5_GROUPS // S5_GB
S5_HALF = S5_GB * S5_STATE
S5_NCH = S5_HALF // 128
S5_SUB = 8

DIFF_HEADS = 8
DIFF_DH = D // (2 * DIFF_HEADS)

D_FF = 2816
N_EXPERTS = 8
ROPE_THETA = 10000.0
EPS = 1e-6

LOG2E = math.log2(math.e)
TM = 512
FF_TILE = D_FF
LANES = 128
VMEM_LIMIT = 56 * 1024 * 1024

F32 = jnp.float32
BF16 = jnp.bfloat16


def _cparams(*sem):
    return pltpu.CompilerParams(dimension_semantics=sem, vmem_limit_bytes=VMEM_LIMIT)


def _group_of_tile(i, tm):
    n_p = N_P // tm
    per = DEC_SEQ // tm
    return jnp.where(i < n_p, 0, 1 + (i - n_p) // per)


def _rope_tile(i, tm):
    n_p = N_P // tm
    per = DEC_SEQ // tm
    return jnp.where(i < n_p, 0, 1 + (i - n_p) % per)


def _rms(x, g):
    return x * lax.rsqrt(jnp.mean(x * x, axis=-1, keepdims=True) + EPS) * g


def _normmod(x, g, mod, k_shift, k_scale):
    return _rms(x, g) * (1.0 + mod[k_scale:k_scale + 1, :]) + mod[k_shift:k_shift + 1, :]


def _sigmoid(x):
    return 1.0 / (1.0 + jnp.exp(-x))


def _dot(a, b):
    return jnp.dot(a, b, preferred_element_type=F32)


def _dot_nt(a, b):
    return lax.dot_general(a, b, (((1,), (1,)), ((), ())), preferred_element_type=F32)


def _rope(x, cos, sin):
    lane = lax.broadcasted_iota(jnp.int32, x.shape, 1)
    nxt = pltpu.roll(x, LANES - 16, 1)
    prv = pltpu.roll(x, 16, 1)
    swapped = jnp.where((lane % 32) < 16, nxt, prv)
    return x * cos + swapped * sin


def _ada_kernel(c_ref, w_ref, b_ref, o_ref):
    c = c_ref[...]
    s = (c * _sigmoid(c)).astype(BF16)
    o_ref[...] = _dot(s, w_ref[...].astype(BF16)) + b_ref[...]


def ada_all(cond8, ada_w, ada_b):
    tn = 1536
    return pl.pallas_call(
        _ada_kernel,
        grid=(DEPTH, 6 * D // tn),
        in_specs=[pl.BlockSpec((8, D), lambda l, n: (0, 0)),
                  pl.BlockSpec((None, D, tn), lambda l, n: (l, 0, n)),
                  pl.BlockSpec((None, 1, tn), lambda l, n: (l, 0, n))],
        out_specs=pl.BlockSpec((None, 8, tn), lambda l, n: (l, 0, n)),
        out_shape=jax.ShapeDtypeStruct((DEPTH, 8, 6 * D), F32),
        compiler_params=_cparams("parallel", "parallel"),
        name="ada",
    )(cond8, ada_w, ada_b.reshape(DEPTH, 1, 6 * D))


def _mod_spec(tm):
    return pl.BlockSpec((None, 6, D), lambda i, *_: (_group_of_tile(i, tm), 0, 0))


def _mla_tok_kernel(x_ref, g_ref, mod_ref, w1_ref, qn_ref, wuq_ref, kvn_ref, cos_ref, sin_ref,
                    q_ref, ckv_ref, kr_ref):
    h = _normmod(x_ref[...], g_ref[...], mod_ref[...], 0, 1).astype(BF16)
    t1 = _dot(h, w1_ref[...])
    ql = _rms(t1[:, :MLA_Q_RANK], qn_ref[...]).astype(BF16)
    q = _dot(ql, wuq_ref[...]) * (MLA_DK ** -0.5 * LOG2E)
    c0 = MLA_Q_RANK
    ckv_ref[...] = _rms(t1[:, c0:c0 + MLA_KV_RANK], kvn_ref[...])
    cos = cos_ref[...]
    sin = sin_ref[...]
    kr = _rope(t1[:, c0 + MLA_KV_RANK:c0 + MLA_KV_RANK + LANES], cos, sin)
    kr_ref[...] = kr[:, :MLA_D_ROPE]
    n_nope = MLA_HEADS * MLA_D_NOPE
    for pair in range(MLA_HEADS // 2):
        qr = _rope(q[:, n_nope + pair * LANES:n_nope + (pair + 1) * LANES], cos, sin).astype(BF16)
        for sub in range(2):
            hd = 2 * pair + sub
            q_ref[hd, :, 0:MLA_D_NOPE] = q[:, hd * MLA_D_NOPE:(hd + 1) * MLA_D_NOPE].astype(BF16)
            q_ref[hd, :, MLA_D_NOPE:MLA_DK] = qr[:, sub * MLA_D_ROPE:(sub + 1) * MLA_D_ROPE]


def mla_tokens(x, g, mod, w1, qn, wuq, kvn, cos_t, sin_t):
    nt = N_TOK // TM
    const = lambda shape: pl.BlockSpec(shape, lambda i: (0,) * len(shape))
    return pl.pallas_call(
        _mla_tok_kernel,
        grid=(nt,),
        in_specs=[pl.BlockSpec((TM, D), lambda i: (i, 0)), const((1, D)), _mod_spec(TM),
                  const(w1.shape), const((1, MLA_Q_RANK)), const(wuq.shape), const((1, MLA_KV_RANK)),
                  pl.BlockSpec((TM, LANES), lambda i: (_rope_tile(i, TM), 0)),
                  pl.BlockSpec((TM, LANES), lambda i: (_rope_tile(i, TM), 0))],
        out_specs=[pl.BlockSpec((MLA_HEADS, TM, MLA_DK), lambda i: (0, i, 0)),
                   pl.BlockSpec((TM, MLA_KV_RANK), lambda i: (i, 0)),
                   pl.BlockSpec((TM, MLA_D_ROPE), lambda i: (i, 0))],
        out_shape=[jax.ShapeDtypeStruct((MLA_HEADS, N_TOK, MLA_DK), BF16),
                   jax.ShapeDtypeStruct((N_TOK, MLA_KV_RANK), F32),
                   jax.ShapeDtypeStruct((N_TOK, MLA_D_ROPE), F32)],
        compiler_params=_cparams("parallel"),
        name="mla_tokens",
    )(x, g, mod, w1, qn, wuq, kvn, cos_t, sin_t)


def _mla_kv_kernel(ckv_ref, kr_ref, w_ref, k_ref, v_ref):
    kv = _dot(ckv_ref[...].astype(BF16), w_ref[...])
    kr = kr_ref[...].astype(BF16)
    n_nope = MLA_HEADS * MLA_D_NOPE
    for hd in range(MLA_HEADS):
        k_ref[hd, :, 0:MLA_D_NOPE] = kv[:, hd * MLA_D_NOPE:(hd + 1) * MLA_D_NOPE].astype(BF16)
        k_ref[hd, :, MLA_D_NOPE:MLA_DK] = kr
        v_ref[hd] = kv[:, n_nope + hd * MLA_D_V:n_nope + (hd + 1) * MLA_D_V].astype(BF16)


def mla_kv(ckv, kr, wukv, ts):
    nb, s, _ = ckv.shape
    return pl.pallas_call(
        _mla_kv_kernel,
        grid=(nb, s // ts),
        in_specs=[pl.BlockSpec((None, ts, MLA_KV_RANK), lambda b, t: (b, t, 0)),
                  pl.BlockSpec((None, ts, MLA_D_ROPE), lambda b, t: (b, t, 0)),
                  pl.BlockSpec(wukv.shape, lambda b, t: (0, 0))],
        out_specs=[pl.BlockSpec((None, MLA_HEADS, ts, MLA_DK), lambda b, t: (b, 0, t, 0)),
                   pl.BlockSpec((None, MLA_HEADS, ts, MLA_D_V), lambda b, t: (b, 0, t, 0))],
        out_shape=[jax.ShapeDtypeStruct((nb, MLA_HEADS, s, MLA_DK), BF16),
                   jax.ShapeDtypeStruct((nb, MLA_HEADS, s, MLA_D_V), BF16)],
        compiler_params=_cparams("parallel", "parallel"),
        name="mla_kv",
    )(ckv, kr, wukv)


def _mla_attn_kernel(hps, q_ref, k_ref, v_ref, o_ref):
    for hd in range(hps):
        s = _dot_nt(q_ref[hd], k_ref[hd])
        p = jnp.exp2(s - jnp.max(s, axis=-1, keepdims=True))
        l = jnp.sum(p, axis=-1, keepdims=True)
        o = _dot(p.astype(BF16), v_ref[hd]) / l
        o_ref[:, hd * MLA_D_V:(hd + 1) * MLA_D_V] = o.astype(BF16)


def mla_attention(q3, k3, v3, row0, seq, tq, hps):
    nb, _, s, _ = k3.shape
    nq = seq // tq
    base = row0 // tq
    return pl.pallas_call(
        functools.partial(_mla_attn_kernel, hps),
        grid=(nb, MLA_HEADS // hps, nq),
        in_specs=[pl.BlockSpec((hps, tq, MLA_DK), lambda b, h, i: (h, base + b * nq + i, 0)),
                  pl.BlockSpec((None, hps, s, MLA_DK), lambda b, h, i: (b, h, 0, 0)),
                  pl.BlockSpec((None, hps, s, MLA_D_V), lambda b, h, i: (b, h, 0, 0))],
        out_specs=pl.BlockSpec((tq, hps * MLA_D_V), lambda b, h, i: (b * nq + i, h)),
        out_shape=jax.ShapeDtypeStruct((nb * seq, MLA_HEADS * MLA_D_V), BF16),
        compiler_params=_cparams("parallel", "parallel", "parallel"),
        name="mla_attn_%d" % s,
    )(q3, k3, v3)


def _proj_res_kernel(op_ref, os_ref, w_ref, x_ref, mod_ref, out_ref):
    o = jnp.where(pl.program_id(0) < N_P // TM, op_ref[...], os_ref[...])
    out_ref[...] = x_ref[...] + mod_ref[2:3, :] * _dot(o, w_ref[...])


def proj_residual(o_p, o_s, w, x, mod):
    nt = N_TOK // TM
    n_p = N_P // TM
    return pl.pallas_call(
        _proj_res_kernel,
        grid=(nt,),
        in_specs=[pl.BlockSpec((TM, D), lambda i: (jnp.minimum(i, n_p - 1), 0)),
                  pl.BlockSpec((TM, D), lambda i: (jnp.maximum(i - n_p, 0), 0)),
                  pl.BlockSpec((D, D), lambda i: (0, 0)),
                  pl.BlockSpec((TM, D), lambda i: (i, 0)), _mod_spec(TM)],
        out_specs=pl.BlockSpec((TM, D), lambda i: (i, 0)),
        out_shape=jax.ShapeDtypeStruct((N_TOK, D), F32),
        compiler_params=_cparams("parallel"),
        name="proj_residual",
    )(o_p, o_s, w, x, mod)


def _diff_tok_kernel(latent, x_ref, g_ref, mod_ref, w_ref, *refs):
    if latent:
        cos_ref, sin_ref, q_ref, kc_ref, vc_ref = refs
        cos = cos_ref[...]
        sin = sin_ref[...]
        rot = lambda t: _rope(t, cos, sin)
    else:
        q_ref, kc_ref, vc_ref, k_ref, v_ref = refs
        rot = lambda t: t
    h = _normmod(x_ref[...], g_ref[...], mod_ref[...], 0, 1).astype(BF16)
    for c in range(D // LANES):
        sl = slice(c * LANES, (c + 1) * LANES)
        q = _dot(h, w_ref[:, c * LANES:(c + 1) * LANES]) * (DIFF_DH ** -0.5 * LOG2E)
        k = _dot(h, w_ref[:, D + c * LANES:D + (c + 1) * LANES])
        v = _dot(h, w_ref[:, 2 * D + c * LANES:2 * D + (c + 1) * LANES])
        q_ref[:, sl] = rot(q).astype(BF16)
        kc_ref[:, sl] = rot(k).astype(BF16)
        vc_ref[:, sl] = v.astype(BF16)
        if not latent:
            k_ref[:, sl] = k
            v_ref[:, sl] = v


def diff_tokens(x, g, mod, wqkv, cos_t, sin_t, latent):
    n_rows = N_S if latent else N_P
    base = (N_P if latent else 0) // TM
    row = lambda i: (i, 0)
    in_specs = [pl.BlockSpec((TM, D), lambda i: (base + i, 0)), pl.BlockSpec((1, D), lambda i: (0, 0)),
                pl.BlockSpec((None, 6, D), lambda i: (_group_of_tile(base + i, TM), 0, 0)),
                pl.BlockSpec((D, 3 * D), lambda i: (0, 0))]
    args = [x, g, mod, wqkv]
    out_shape = [jax.ShapeDtypeStruct((n_rows, D), BF16)] * 3
    if latent:
        rope_spec = pl.BlockSpec((TM, LANES), lambda i: (_rope_tile(base + i, TM), 0))
        in_specs += [rope_spec, rope_spec]
        args += [cos_t, sin_t]
    else:
        out_shape += [jax.ShapeDtypeStruct((n_rows, D), F32)] * 2
    return pl.pallas_call(
        functools.partial(_diff_tok_kernel, latent),
        grid=(n_rows // TM,),
        in_specs=in_specs,
        out_specs=[pl.BlockSpec((TM, D), row)] * len(out_shape),
        out_shape=out_shape,
        compiler_params=_cparams("parallel"),
        name="diff_tokens_lat" if latent else "diff_tokens_ctx",
    )(*args)


def _diff_attn_kernel(lam_init, n_kv, pairs, lq1_ref, lk1_ref, lq2_ref, lk2_ref, sub_ref, q_ref, *refs):
    kv_refs = refs[:2 * n_kv]
    o_ref = refs[2 * n_kv]
    lam = (jnp.exp(jnp.sum(lq1_ref[...] * lk1_ref[...], axis=-1, keepdims=True))
           - jnp.exp(jnp.sum(lq2_ref[...] * lk2_ref[...], axis=-1, keepdims=True)) + lam_init)
    for pr in range(pairs):
        lanes = slice(pr * LANES, (pr + 1) * LANES)
        q = q_ref[:, lanes]
        lane = lax.broadcasted_iota(jnp.int32, q.shape, 1)
        zero = jnp.zeros_like(q)

        def probs(qh):
            s = jnp.concatenate([_dot_nt(qh, kv_refs[2 * i][:, lanes]) for i in range(n_kv)], axis=1)
            p = jnp.exp2(s - jnp.max(s, axis=-1, keepdims=True))
            return p, jnp.sum(p, axis=-1, keepdims=True)

        p1, l1 = probs(jnp.where(lane < DIFF_DH, q, zero))
        p2, l2 = probs(jnp.where(lane >= DIFF_DH, q, zero))
        att = (p1 + (-lam * l1 / l2) * p2).astype(BF16)
        acc = jnp.zeros(q.shape, F32)
        col = 0
        for i in range(n_kv):
            n = kv_refs[2 * i + 1].shape[0]
            acc = acc + _dot(att[:, col:col + n], kv_refs[2 * i + 1][:, lanes])
            col += n
        o = acc / l1
        o_ref[:, lanes] = (_rms(o, sub_ref[...]) * (1.0 - lam_init)).astype(BF16)


def diff_attention(lam_init, lvecs, subln, q, kvs, seq, tq, pairs):
    nb = kvs[0][0].shape[0]
    nq = seq // tq
    n_keys = sum(k.shape[1] for k, _ in kvs)
    width = pairs * LANES
    vec = pl.BlockSpec((1, DIFF_DH), lambda b, h, i: (0, 0))
    kv_specs, kv_args = [], []
    for k, v in kvs:
        spec = pl.BlockSpec((None, k.shape[1], width), lambda b, h, i: (b, 0, h))
        kv_specs += [spec, spec]
        kv_args += [k, v]
    return pl.pallas_call(
        functools.partial(_diff_attn_kernel, lam_init, len(kvs), pairs),
        grid=(nb, DIFF_HEADS // pairs, nq),
        in_specs=[vec, vec, vec, vec, pl.BlockSpec((1, 2 * DIFF_DH), lambda b, h, i: (0, 0)),
                  pl.BlockSpec((tq, width), lambda b, h, i: (b * nq + i, h))] + kv_specs,
        out_specs=pl.BlockSpec((tq, width), lambda b, h, i: (b * nq + i, h)),
        out_shape=jax.ShapeDtypeStruct((nb * seq, D), BF16),
        compiler_params=_cparams("parallel", "parallel", "parallel"),
        name="diff_attn_%d" % n_keys,
    )(*lvecs, subln, q, *kv_args)


def _normmod_kernel(x_ref, g_ref, mod_ref, h_ref):
    h_ref[...] = _normmod(x_ref[...], g_ref[...], mod_ref[...], 0, 1)


S5_T = SEQ
S5_SEG = DEC_SEQ // S5_SUB
S5_TILES_P = N_P // S5_T
S5_PER_SEG = S5_SEG // S5_T


def _s5_tile_pos(i):
    k = i - S5_TILES_P
    per_batch = S5_SUB * S5_PER_SEG
    lat_row = S5_TILES_P // S5_SUB + (k // per_batch) * S5_PER_SEG + k % S5_PER_SEG
    lat_col = (k % per_batch) // S5_PER_SEG
    is_p = i < S5_TILES_P
    return jnp.where(is_p, i // S5_SUB, lat_row), jnp.where(is_p, i % S5_SUB, lat_col)


def normmod_time_major(x, g, mod):
    nt = N_TOK // S5_T
    return pl.pallas_call(
        _normmod_kernel,
        grid=(nt,),
        in_specs=[pl.BlockSpec((S5_T, D), lambda i: (i, 0)), pl.BlockSpec((1, D), lambda i: (0, 0)),
                  _mod_spec(S5_T)],
        out_specs=pl.BlockSpec((S5_T, D), lambda i: _s5_tile_pos(i)),
        out_shape=jax.ShapeDtypeStruct((N_TOK // S5_SUB, S5_SUB * D), F32),
        compiler_params=_cparams("parallel"),
        name="normmod",
    )(x, g, mod)


def _s5_kernel(chained, n, u_ref, wb_ref, wc_ref, a_ref, an_ref, dsk_ref, h0_ref, y_ref, fin_ref,
               bu_ref, ini_ref):
    d = pl.program_id(2)
    rows = S5_SUB * n
    chunk = 512
    nch = S5_NCH
    for r in range(rows // chunk):
        rs = slice(r * chunk, (r + 1) * chunk)
        bu = _dot(u_ref[rs, :].astype(BF16), wb_ref[...])
        for c in range(2 * nch):
            bu_ref[c, rs, :] = bu[:, c * LANES:(c + 1) * LANES]
    ar = [jnp.broadcast_to(a_ref[0:1, c * LANES:(c + 1) * LANES], (S5_SUB, LANES)) for c in range(nch)]
    ai = [jnp.broadcast_to(a_ref[1:2, c * LANES:(c + 1) * LANES], (S5_SUB, LANES)) for c in range(nch)]

    def step_index(s):
        return jnp.where(d == 0, s, n - 1 - s)

    def step_rows(t):
        return pl.ds(pl.multiple_of(t * S5_SUB, S5_SUB), S5_SUB)

    def advance(h, t):
        out = [None] * (2 * nch)
        for c in range(nch):
            br = bu_ref[c, step_rows(t), :]
            bi = bu_ref[nch + c, step_rows(t), :]
            out[c] = ar[c] * h[c] - ai[c] * h[nch + c] + br
            out[nch + c] = ar[c] * h[nch + c] + ai[c] * h[c] + bi
        return out

    unroll = 4
    zeros = [jnp.zeros((S5_SUB, LANES), F32) for _ in range(2 * nch)]

    if chained:
        def local_body(s, h):
            h = list(h)
            for k in range(unroll):
                h = advance(h, step_index(s * unroll + k))
            return tuple(h)

        ends = lax.fori_loop(0, n // unroll, local_body, tuple(zeros))
        sub_row = lax.broadcasted_iota(jnp.int32, (S5_SUB, LANES), 0)
        cur = [h0_ref[:, c * LANES:(c + 1) * LANES] for c in range(2 * nch)]
        anr = [an_ref[0:1, c * LANES:(c + 1) * LANES] for c in range(nch)]
        ani = [an_ref[1:2, c * LANES:(c + 1) * LANES] for c in range(nch)]
        for kk in range(S5_SUB):
            j = jnp.where(d == 0, kk, S5_SUB - 1 - kk)
            nxt = [None] * (2 * nch)
            for c in range(2 * nch):
                ini_ref[c, pl.ds(j, 1), :] = cur[c]
            for c in range(nch):
                er = jnp.sum(jnp.where(sub_row == j, ends[c], 0.0), axis=0, keepdims=True)
                ei = jnp.sum(jnp.where(sub_row == j, ends[nch + c], 0.0), axis=0, keepdims=True)
                nxt[c] = er + anr[c] * cur[c] - ani[c] * cur[nch + c]
                nxt[nch + c] = ei + anr[c] * cur[nch + c] + ani[c] * cur[c]
            cur = nxt
        h_init = [ini_ref[c] for c in range(2 * nch)]
    else:
        h_init = zeros

    def body(s, h):
        h = list(h)
        for k in range(unroll):
            t = step_index(s * unroll + k)
            h = advance(h, t)
            for c in range(2 * nch):
                bu_ref[c, step_rows(t), :] = h[c]
        return tuple(h)

    fin = lax.fori_loop(0, n // unroll, body, tuple(h_init))
    for c in range(2 * nch):
        fin_ref[:, c * LANES:(c + 1) * LANES] = fin[c]

    @pl.when(d == 0)
    def _():
        y_ref[...] = dsk_ref[...] * u_ref[...]

    for r in range(rows // chunk):
        rs = slice(r * chunk, (r + 1) * chunk)
        hs = jnp.concatenate([bu_ref[c, rs, :].astype(BF16) for c in range(2 * nch)], axis=1)
        y_ref[rs, :] += _dot(hs, wc_ref[...])


def s5_scan(h, wb, wc, a, an, dskip, h0, row0, n_blocks, n, chained):
    rows = S5_SUB * n
    base = row0 // rows
    kern = functools.partial(_s5_kernel, chained, n)
    return pl.pallas_call(
        kern,
        grid=(n_blocks, S5_NGB, 2),
        in_specs=[pl.BlockSpec((rows, LANES), lambda r, c, d: (base + r, c)),
                  pl.BlockSpec((None, None, LANES, 2 * S5_HALF), lambda r, c, d: (d, c, 0, 0)),
                  pl.BlockSpec((None, None, 2 * S5_HALF, LANES), lambda r, c, d: (d, c, 0, 0)),
                  pl.BlockSpec((None, None, 2, S5_HALF), lambda r, c, d: (d, c, 0, 0)),
                  pl.BlockSpec((None, None, 2, S5_HALF), lambda r, c, d: (d, c, 0, 0)),
                  pl.BlockSpec((1, LANES), lambda r, c, d: (0, c)),
                  pl.BlockSpec((None, None, None, 1, 2 * S5_HALF), lambda r, c, d: (r, d, c, 0, 0))],
        out_specs=[pl.BlockSpec((rows, LANES), lambda r, c, d: (r, c)),
                   pl.BlockSpec((None, None, None, S5_SUB, 2 * S5_HALF), lambda r, c, d: (r, d, c, 0, 0))],
        out_shape=[jax.ShapeDtypeStruct((n_blocks * rows, D), F32),
                   jax.ShapeDtypeStruct((n_blocks, 2, S5_NGB, S5_SUB, 2 * S5_HALF), F32)],
        scratch_shapes=[pltpu.VMEM((2 * S5_NCH, rows, LANES), F32),
                        pltpu.VMEM((2 * S5_NCH, S5_SUB, LANES), F32)],
        compiler_params=_cparams("parallel", "parallel", "arbitrary"),
        name="s5_scan_%d" % n,
    )(h, wb, wc, a, an, dskip, h0)


def _glu_res_kernel(yp_ref, ys_ref, w_ref, x_ref, mod_ref, out_ref):
    y = jnp.where(pl.program_id(0) < S5_TILES_P, yp_ref[...], ys_ref[...])
    g = 0.5 * y * (1.0 + jnp.tanh(math.sqrt(2.0 / math.pi) * (y + 0.044715 * (y * y * y))))
    t = _dot(g.astype(BF16), w_ref[...])
    out_ref[...] = x_ref[...] + mod_ref[2:3, :] * (t[:, :D] * _sigmoid(t[:, D:]))


def glu_residual(y_p, y_s, w, x, mod):
    nt = N_TOK // S5_T
    n_blk_p = S5_TILES_P // S5_SUB

    def yp_map(i):
        r, c = _s5_tile_pos(jnp.minimum(i, S5_TILES_P - 1))
        return r, c

    def ys_map(i):
        r, c = _s5_tile_pos(jnp.maximum(i, S5_TILES_P))
        return r - n_blk_p, c

    return pl.pallas_call(
        _glu_res_kernel,
        grid=(nt,),
        in_specs=[pl.BlockSpec((S5_T, D), yp_map), pl.BlockSpec((S5_T, D), ys_map),
                  pl.BlockSpec((D, 2 * D), lambda i: (0, 0)),
                  pl.BlockSpec((S5_T, D), lambda i: (i, 0)), _mod_spec(S5_T)],
        out_specs=pl.BlockSpec((S5_T, D), lambda i: (i, 0)),
        out_shape=jax.ShapeDtypeStruct((N_TOK, D), F32),
        compiler_params=_cparams("parallel"),
        name="glu_residual",
    )(y_p.reshape(N_P // S5_SUB, S5_SUB * D), y_s.reshape(N_S // S5_SUB, S5_SUB * D), w, x, mod)


FF_CHUNKS = ((0, 768), (768, 1536), (1536, 2304), (2304, FF_TILE))


def _swiglu_partial(h, wa_ref, wb_ref, wo_ref):
    out = None
    for lo, hi in FF_CHUNKS:
        a = _dot(h, wa_ref[:, lo:hi])
        b = _dot(h, wb_ref[:, lo:hi])
        act = (a * _sigmoid(a) * b).astype(BF16)
        part = _dot(act, wo_ref[lo:hi, :])
        out = part if out is None else out + part
    return out


def _ffn_kernel(x_ref, g_ref, mod_ref, wa_ref, wb_ref, wo_ref, out_ref, h_scr, acc_scr):
    f = pl.program_id(1)

    @pl.when(f == 0)
    def _():
        h_scr[...] = _normmod(x_ref[...], g_ref[...], mod_ref[...], 3, 4).astype(BF16)
        acc_scr[...] = jnp.zeros_like(acc_scr)

    acc_scr[...] += _swiglu_partial(h_scr[...], wa_ref, wb_ref, wo_ref)

    @pl.when(f == pl.num_programs(1) - 1)
    def _():
        out_ref[...] = x_ref[...] + mod_ref[5:6, :] * acc_scr[...]


def ffn_residual(x, g, mod, w_in, w_out, layer):
    nt = N_TOK // TM
    nf = D_FF // FF_TILE
    return pl.pallas_call(
        _ffn_kernel,
        grid=(nt, nf),
        in_specs=[pl.BlockSpec((TM, D), lambda i, f: (i, 0)), pl.BlockSpec((1, D), lambda i, f: (0, 0)),
                  _mod_spec(TM),
                  pl.BlockSpec((None, D, FF_TILE), lambda i, f: (layer, 0, f)),
                  pl.BlockSpec((None, D, FF_TILE), lambda i, f: (layer, 0, f + nf)),
                  pl.BlockSpec((None, FF_TILE, D), lambda i, f: (layer, f, 0))],
        out_specs=pl.BlockSpec((TM, D), lambda i, f: (i, 0)),
        out_shape=jax.ShapeDtypeStruct((N_TOK, D), F32),
        scratch_shapes=[pltpu.VMEM((TM, D), BF16), pltpu.VMEM((TM, D), F32)],
        compiler_params=_cparams("parallel", "arbitrary"),
        name="ffn",
    )(x, g, mod, w_in, w_in, w_out)


TS = 512
N_SLOTS = 2 * N_TOK + N_EXPERTS * TS
NT_S = N_SLOTS // TS
TG = 256
PIECE = 256
MAX_PIECES = (TM + 16 + PIECE - 1) // PIECE


def _moe_route_kernel(x_ref, g_ref, mod_ref, wr_ref, br_ref, h_ref, route_ref, rows_ref, run_ref, carry_scr):
    @pl.when(pl.program_id(0) == 0)
    def _():
        carry_scr[...] = jnp.zeros_like(carry_scr)

    h = _normmod(x_ref[...], g_ref[...], mod_ref[...], 3, 4).astype(BF16)
    h_ref[...] = h
    logits = _dot(h, wr_ref[...]) + br_ref[...]
    lane = lax.broadcasted_iota(jnp.int32, logits.shape, 1)
    neg = jnp.float32(-jnp.inf)
    lg = jnp.where(lane < N_EXPERTS, logits, neg)
    v1 = jnp.max(lg, axis=-1, keepdims=True)
    i1 = jnp.min(jnp.where(lg == v1, lane, LANES), axis=-1, keepdims=True)
    lg2 = jnp.where(lane == i1, neg, lg)
    v2 = jnp.max(lg2, axis=-1, keepdims=True)
    i2 = jnp.min(jnp.where(lg2 == v2, lane, LANES), axis=-1, keepdims=True)
    e2 = jnp.exp(v2 - v1)
    g1 = 1.0 / (1.0 + e2)
    g2 = e2 / (1.0 + e2)
    oh1 = lane == i1
    oh2 = lane == i2
    sel = jnp.where(oh1 | oh2, 1.0, 0.0)
    r = lax.broadcasted_iota(jnp.int32, (TM, TM), 0)
    c = lax.broadcasted_iota(jnp.int32, (TM, TM), 1)
    tri = jnp.where(c < r, 1.0, 0.0).astype(BF16)
    rank = _dot(tri, sel.astype(BF16)) + carry_scr[0:1, :]
    r1 = jnp.sum(jnp.where(oh1, rank, 0.0), axis=-1, keepdims=True)
    r2 = jnp.sum(jnp.where(oh2, rank, 0.0), axis=-1, keepdims=True)
    cols = (i1.astype(F32), i2.astype(F32), g1, g2, r1, r2)
    route = jnp.zeros(logits.shape, F32)
    for k, v in enumerate(cols):
        route = jnp.where(lane == k, v, route)
    route_ref[...] = route
    rows_ref[...] = route.T[0:8, :]
    carry_scr[...] = carry_scr[...] + jnp.sum(sel, axis=0, keepdims=True)
    run_ref[...] = carry_scr[...]


def moe_route(x, g, mod, w_router, b_router):
    nt = N_TOK // TM
    return pl.pallas_call(
        _moe_route_kernel,
        grid=(nt,),
        in_specs=[pl.BlockSpec((TM, D), lambda i: (i, 0)), pl.BlockSpec((1, D), lambda i: (0, 0)), _mod_spec(TM),
                  pl.BlockSpec((D, LANES), lambda i: (0, 0)), pl.BlockSpec((1, LANES), lambda i: (0, 0))],
        out_specs=[pl.BlockSpec((TM, D), lambda i: (i, 0)), pl.BlockSpec((TM, LANES), lambda i: (i, 0)),
                   pl.BlockSpec((8, TM), lambda i: (0, i)),
                   pl.BlockSpec((None, 8, LANES), lambda i: (i, 0, 0))],
        out_shape=[jax.ShapeDtypeStruct((N_TOK, D), BF16), jax.ShapeDtypeStruct((N_TOK, LANES), F32),
                   jax.ShapeDtypeStruct((8, N_TOK), F32),
                   jax.ShapeDtypeStruct((nt, 8, LANES), F32)],
        scratch_shapes=[pltpu.VMEM((8, LANES), F32)],
        compiler_params=_cparams("arbitrary"),
        name="moe_route",
    )(x, g, mod, w_router, b_router)


def _moe_gather_kernel(used_ref, clo_ref, chi_ref, pos_ref, gates_ref, h_ref, xs_ref, gate_ref, gat_scr, gsum_scr):
    i = pl.program_id(0)
    gat_scr[...] = jnp.zeros_like(gat_scr)
    gsum_scr[...] = jnp.zeros_like(gsum_scr)

    @pl.when(used_ref[i] > 0)
    def _():
        slot = i * TG + lax.broadcasted_iota(jnp.int32, (TG, TM), 0)

        def body(c, carry):
            base = pl.multiple_of(c * TM, TM)
            m1 = pos_ref[0:1, pl.ds(base, TM)] == slot
            m2 = pos_ref[1:2, pl.ds(base, TM)] == slot
            pick = jnp.where(m1 | m2, 1.0, 0.0).astype(BF16)
            gat_scr[...] += _dot(pick, h_ref[pl.ds(base, TM), :])
            g = jnp.where(m1, gates_ref[0:1, pl.ds(base, TM)], 0.0) + jnp.where(m2, gates_ref[1:2, pl.ds(base, TM)], 0.0)
            gsum_scr[...] += jnp.sum(g, axis=-1, keepdims=True)
            return carry

        lax.fori_loop(clo_ref[i], chi_ref[i] + 1, body, 0)

    xs_ref[...] = gat_scr[...].astype(BF16)
    gate_ref[...] = gsum_scr[...]


def moe_gather(used, clo, chi, pos_rows, gate_rows, h_b):
    whole = lambda shape: pl.BlockSpec(shape, lambda i, *_: (0, 0), pipeline_mode=pl.Buffered(1))
    grid_spec = pltpu.PrefetchScalarGridSpec(
        num_scalar_prefetch=3,
        grid=(N_SLOTS // TG,),
        in_specs=[whole((8, N_TOK)), whole((8, N_TOK)), whole((N_TOK, D))],
        out_specs=[pl.BlockSpec((TG, D), lambda i, *_: (i, 0)), pl.BlockSpec((TG, 1), lambda i, *_: (i, 0))],
        scratch_shapes=[pltpu.VMEM((TG, D), F32), pltpu.VMEM((TG, 1), F32)],
    )
    return pl.pallas_call(
        _moe_gather_kernel,
        grid_spec=grid_spec,
        out_shape=[jax.ShapeDtypeStruct((N_SLOTS, D), BF16), jax.ShapeDtypeStruct((N_SLOTS, 1), F32)],
        compiler_params=_cparams("arbitrary"),
        name="moe_gather",
    )(used, clo, chi, pos_rows, gate_rows, h_b)


def _moe_ffn_kernel(texp_ref, used_ref, xs_ref, gate_ref, wa_ref, wb_ref, wo_ref, ys_ref, acc_scr):
    i = pl.program_id(0)
    f = pl.program_id(1)
    live = used_ref[i] > 0

    @pl.when(f == 0)
    def _():
        acc_scr[...] = jnp.zeros_like(acc_scr)

    @pl.when(live)
    def _():
        acc_scr[...] += _swiglu_partial(xs_ref[...], wa_ref, wb_ref, wo_ref)

    @pl.when(f == pl.num_programs(1) - 1)
    def _():
        ys_ref[...] = (gate_ref[...] * acc_scr[...]).astype(BF16)


def moe_ffn(texp, used, xs, gate_col, w_in, w_out, layer):
    nf = D_FF // FF_TILE
    grid_spec = pltpu.PrefetchScalarGridSpec(
        num_scalar_prefetch=2,
        grid=(NT_S, nf),
        in_specs=[pl.BlockSpec((TS, D), lambda i, f, *_: (i, 0)),
                  pl.BlockSpec((TS, 1), lambda i, f, *_: (i, 0)),
                  pl.BlockSpec((None, None, D, FF_TILE), lambda i, f, texp, *_: (layer, texp[i], 0, f)),
                  pl.BlockSpec((None, None, D, FF_TILE), lambda i, f, texp, *_: (layer, texp[i], 0, f + nf)),
                  pl.BlockSpec((None, None, FF_TILE, D), lambda i, f, texp, *_: (layer, texp[i], f, 0))],
        out_specs=pl.BlockSpec((TS, D), lambda i, f, *_: (i, 0)),
        scratch_shapes=[pltpu.VMEM((TS, D), F32)],
    )
    return pl.pallas_call(
        _moe_ffn_kernel,
        grid_spec=grid_spec,
        out_shape=jax.ShapeDtypeStruct((N_SLOTS, D), BF16),
        compiler_params=_cparams("parallel", "arbitrary"),
        name="moe_ffn",
    )(texp, used, xs, gate_col, w_in, w_in, w_out)


def _moe_combine_kernel(start_ref, npc_ref, lo_ref, hi_ref, off_ref, x_ref, mod_ref, route_ref, ys_hbm, out_ref,
                        buf, acc_scr, sem):
    i = pl.program_id(0)
    n_tiles = pl.num_programs(0)

    def piece_copy(tile, e, k):
        half = tile % 2
        s = pl.multiple_of(start_ref[tile * N_EXPERTS + e] + k * PIECE, 16)
        return pltpu.make_async_copy(ys_hbm.at[pl.ds(s, PIECE), :], buf.at[half, e, pl.ds(k * PIECE, PIECE), :],
                                     sem.at[half, e, k])

    def start_tile(tile):
        for e in range(N_EXPERTS):
            for k in range(MAX_PIECES):
                @pl.when(k < npc_ref[tile * N_EXPERTS + e])
                def _():
                    piece_copy(tile, e, k).start()

    @pl.when(i == 0)
    def _():
        start_tile(i)

    @pl.when(i + 1 < n_tiles)
    def _():
        start_tile(i + 1)

    half = i % 2
    acc_scr[...] = jnp.zeros_like(acc_scr)
    route = route_ref[...]
    e1 = route[:, 0:1].astype(jnp.int32)
    e2 = route[:, 1:2].astype(jnp.int32)
    pos1 = route[:, 4:5].astype(jnp.int32)
    pos2 = route[:, 5:6].astype(jnp.int32)
    for e in range(N_EXPERTS):
        pos1 = pos1 + jnp.where(e1 == e, off_ref[e], 0)
        pos2 = pos2 + jnp.where(e2 == e, off_ref[e], 0)
    lane = lax.broadcasted_iota(jnp.int32, (TM, PIECE), 1)
    for e in range(N_EXPERTS):
        lo = lo_ref[i * N_EXPERTS + e]
        hi = hi_ref[i * N_EXPERTS + e]
        p1 = jnp.where((pos1 >= lo) & (pos1 < hi), pos1, -1)
        p2 = jnp.where((pos2 >= lo) & (pos2 < hi), pos2, -1)
        for k in range(MAX_PIECES):
            @pl.when(k < npc_ref[i * N_EXPERTS + e])
            def _():
                piece_copy(i, e, k).wait()
                base = start_ref[i * N_EXPERTS + e] + k * PIECE
                pick = jnp.where((p1 - base == lane) | (p2 - base == lane), 1.0, 0.0).astype(BF16)
                acc_scr[...] += _dot(pick, buf[half, e, pl.ds(k * PIECE, PIECE), :])

    out_ref[...] = x_ref[...] + mod_ref[5:6, :] * acc_scr[...]


def moe_combine(start, npc, lo, hi, off, x, mod, route, ys):
    nt = N_TOK // TM
    grid_spec = pltpu.PrefetchScalarGridSpec(
        num_scalar_prefetch=5,
        grid=(nt,),
        in_specs=[pl.BlockSpec((TM, D), lambda i, *_: (i, 0)),
                  pl.BlockSpec((None, 6, D), lambda i, *_: (_group_of_tile(i, TM), 0, 0)),
                  pl.BlockSpec((TM, LANES), lambda i, *_: (i, 0)),
                  pl.BlockSpec(memory_space=pl.ANY)],
        out_specs=pl.BlockSpec((TM, D), lambda i, *_: (i, 0)),
        scratch_shapes=[pltpu.VMEM((2, N_EXPERTS, MAX_PIECES * PIECE, D), BF16), pltpu.VMEM((TM, D), F32),
                        pltpu.SemaphoreType.DMA((2, N_EXPERTS, MAX_PIECES))],
    )
    return pl.pallas_call(
        _moe_combine_kernel,
        grid_spec=grid_spec,
        out_shape=jax.ShapeDtypeStruct((N_TOK, D), F32),
        compiler_params=_cparams("arbitrary"),
        name="moe_combine",
    )(start, npc, lo, hi, off, x, mod, route, ys)


def moe_residual(x, g, mod, w_router, b_router, w_in, w_out, layer):
    nt = N_TOK // TM
    i32 = jnp.int32
    h_b, route, rows, run = moe_route(x, g, mod, w_router, b_router)
    run = run[:, 0, :N_EXPERTS].astype(i32)
    run_prev = jnp.concatenate([jnp.zeros((1, N_EXPERTS), i32), run[:-1]], axis=0)
    total = run[-1]
    padded = (total + TS - 1) // TS * TS
    off_end = jnp.cumsum(padded)
    off = off_end - padded
    experts = jnp.arange(N_EXPERTS, dtype=i32)
    off_of = lambda e_row: jnp.sum(jnp.where(e_row[None, :].astype(i32) == experts[:, None], off[:, None], 0), axis=0)
    pos1 = rows[4].astype(i32) + off_of(rows[0])
    pos2 = rows[5].astype(i32) + off_of(rows[1])
    pos_rows = jnp.concatenate([pos1[None], pos2[None], jnp.full((6, N_TOK), -1, i32)], axis=0)
    gate_rows = jnp.concatenate([rows[2:4], jnp.zeros((6, N_TOK), F32)], axis=0)
    def tile_tables(tile):
        tile_start = jnp.arange(N_SLOTS // tile, dtype=i32) * tile
        texp = jnp.minimum(jnp.sum((tile_start[:, None] >= off_end[None, :]).astype(i32), axis=1), N_EXPERTS - 1)
        mine = texp[:, None] == experts[None, :]
        of_expert = lambda per_e: jnp.sum(jnp.where(mine, per_e[None, :], 0), axis=1)
        of_tile = lambda per_e: jnp.sum(jnp.where(mine[:, None, :], per_e[None], 0), axis=2)
        rank0 = tile_start - of_expert(off)
        rank1 = jnp.minimum(rank0 + tile, of_expert(total))
        used = rank0 < rank1
        clo = jnp.sum((of_tile(run) <= rank0[:, None]).astype(i32), axis=1)
        chi = jnp.sum((of_tile(run_prev) < rank1[:, None]).astype(i32), axis=1) - 1
        clo = jnp.where(used, jnp.minimum(clo, nt - 1), 0).astype(i32)
        chi = jnp.where(used, chi, -1).astype(i32)
        return texp, used.astype(i32), clo, chi

    _, used_g, clo, chi = tile_tables(TG)
    texp, used, _, _ = tile_tables(TS)
    xs, gate_col = moe_gather(used_g, clo, chi, pos_rows, gate_rows, h_b)
    ys = moe_ffn(texp, used, xs, gate_col, w_in, w_out, layer)
    lo = off[None, :] + run_prev
    hi = off[None, :] + run
    start = jnp.minimum(lo // 16 * 16, N_SLOTS - MAX_PIECES * PIECE)
    npc = jnp.where(hi > lo, (hi - start + PIECE - 1) // PIECE, 0)
    flat = lambda a: a.reshape(nt * N_EXPERTS).astype(i32)
    return moe_combine(flat(start), flat(npc), flat(lo), flat(hi), off.astype(i32), x, mod, route, ys)


def _final_norm_kernel(x_ref, g_ref, o_ref):
    o_ref[...] = _rms(x_ref[...], g_ref[...])


def final_norm(x, g, row0, n_rows):
    base = row0 // TM
    return pl.pallas_call(
        _final_norm_kernel,
        grid=(n_rows // TM,),
        in_specs=[pl.BlockSpec((TM, D), lambda i: (base + i, 0)), pl.BlockSpec((1, D), lambda i: (0, 0))],
        out_specs=pl.BlockSpec((TM, D), lambda i: (i, 0)),
        out_shape=jax.ShapeDtypeStruct((n_rows, D), F32),
        compiler_params=_cparams("parallel"),
        name="final_norm",
    )(x, g)


def _rope_tables():
    half = 16
    freqs = ROPE_THETA ** (-jnp.arange(half, dtype=F32) / half)
    t = jnp.arange(DEC_SEQ, dtype=jnp.int32)
    row = (t // GRID_W).astype(F32)[:, None] * freqs[None, :]
    col = (t % GRID_W).astype(F32)[:, None] * freqs[None, :]
    cos = jnp.concatenate([jnp.cos(row), jnp.cos(row), jnp.cos(col), jnp.cos(col)], axis=1)
    sin = jnp.concatenate([-jnp.sin(row), jnp.sin(row), -jnp.sin(col), jnp.sin(col)], axis=1)
    cos = jnp.concatenate([jnp.ones((TM, 64), F32), cos], axis=0)
    sin = jnp.concatenate([jnp.zeros((TM, 64), F32), sin], axis=0)
    return jnp.tile(cos, (1, 2)), jnp.tile(sin, (1, 2))


def _mla_weights(w_dq, w_uq, w_dkv, w_ukv, w_o):
    w1 = jnp.concatenate([w_dq, w_dkv, jnp.zeros((D, LANES - MLA_D_ROPE), F32)], axis=1).astype(BF16)
    uq = w_uq.reshape(MLA_Q_RANK, MLA_HEADS, MLA_DK)
    wuq = jnp.concatenate([uq[:, :, :MLA_D_NOPE].reshape(MLA_Q_RANK, -1),
                           uq[:, :, MLA_D_NOPE:].reshape(MLA_Q_RANK, -1)], axis=1).astype(BF16)
    ukv = w_ukv.reshape(MLA_KV_RANK, MLA_HEADS, MLA_D_NOPE + MLA_D_V)
    wukv = jnp.concatenate([ukv[:, :, :MLA_D_NOPE].reshape(MLA_KV_RANK, -1),
                            ukv[:, :, MLA_D_NOPE:].reshape(MLA_KV_RANK, -1)], axis=1).astype(BF16)
    return w1, wuq, wukv, w_o.astype(BF16)


def _s5_weights(a_re, a_im, log_dt, b_re, b_im, c_re, c_im, seg_len):
    dt = jnp.exp(log_dt)[..., None]
    mag = jnp.exp(a_re * dt)
    abar_re, abar_im = mag * jnp.cos(a_im * dt), mag * jnp.sin(a_im * dt)
    mag_n = jnp.exp(a_re * dt * seg_len)
    apow_re, apow_im = mag_n * jnp.cos(a_im * dt * seg_len), mag_n * jnp.sin(a_im * dt * seg_len)
    den = a_re * a_re + a_im * a_im
    coef_re = ((abar_re - 1.0) * a_re + abar_im * a_im) / den
    coef_im = (abar_im * a_re - (abar_re - 1.0) * a_im) / den
    bbar_re = coef_re[..., None] * b_re - coef_im[..., None] * b_im
    bbar_im = coef_re[..., None] * b_im + coef_im[..., None] * b_re
    eye = jnp.eye(S5_GB, dtype=F32)

    def in_block(m):
        m = m.reshape(2, S5_NGB, S5_GB, S5_STATE, S5_GROUP)
        return jnp.einsum('dbgpc,gh->dbgchp', m, eye).reshape(2, S5_NGB, LANES, S5_HALF)

    def out_block(m):
        m = m.reshape(2, S5_NGB, S5_GB, S5_GROUP, S5_STATE)
        return jnp.einsum('dbgcp,gh->dbgphc', m, eye).reshape(2, S5_NGB, S5_HALF, LANES)

    wb = jnp.concatenate([in_block(bbar_re), in_block(bbar_im)], axis=3).astype(BF16)
    wc = jnp.concatenate([out_block(c_re), out_block(-c_im)], axis=2).astype(BF16)
    lanes = lambda m: m.reshape(2, S5_NGB, 1, S5_HALF)
    a = jnp.concatenate([lanes(abar_re), lanes(abar_im)], axis=2)
    an = jnp.concatenate([lanes(apow_re), lanes(apow_im)], axis=2)
    return wb, wc, a, an


def kernel(x_prompt, x_sample, c, c_ctx, cache_mla_ckv, cache_mla_krope, state_s5_re, state_s5_im, cache_diff_k, cache_diff_v, ada_w, ada_b, norm_mix, norm_ffn, norm_final, mla_w_dq, mla_q_norm, mla_w_uq, mla_w_dkv, mla_kv_norm, mla_w_ukv, mla_w_o, s5_a_re, s5_a_im, s5_log_dt, s5_b_re, s5_b_im, s5_c_re, s5_c_im, s5_d, s5_w_glu, diff_w_qkv, diff_lq1, diff_lk1, diff_lq2, diff_lk2, diff_subln, diff_w_o, ffn_w_in, ffn_w_out, moe_w_router, moe_b_router, moe_w_in, moe_w_out):
    x = jnp.concatenate([x_prompt.reshape(N_P, D), x_sample.reshape(N_S, D)], axis=0)
    cond8 = jnp.concatenate([c_ctx[None], c, jnp.zeros((8 - N_GROUPS, D), F32)], axis=0)
    mods = ada_all(cond8, ada_w, ada_b).reshape(DEPTH, 8, 6, D)[:, :N_GROUPS]
    cos_t, sin_t = _rope_tables()

    ffn_w_in_b, ffn_w_out_b = ffn_w_in.astype(BF16), ffn_w_out.astype(BF16)
    moe_w_in_b, moe_w_out_b = moe_w_in.astype(BF16), moe_w_out.astype(BF16)
    new_ckv, new_kr, new_s5_re, new_s5_im, new_dk, new_dv = [], [], [], [], [], []
    for i in range(DEPTH):
        mod = mods[i]
        gmix = norm_mix[i].reshape(1, D)
        gffn = norm_ffn[i].reshape(1, D)
        j = i // 3
        kind = i % 3
        if kind == 0:
            w1, wuq, wukv, wo = _mla_weights(mla_w_dq[j], mla_w_uq[j], mla_w_dkv[j], mla_w_ukv[j], mla_w_o[j])
            q3, ckv, kr = mla_tokens(x, gmix, mod, w1, mla_q_norm[j].reshape(1, -1), wuq,
                                     mla_kv_norm[j].reshape(1, -1), cos_t, sin_t)
            ckv_p = ckv[:N_P].reshape(BATCH, SEQ, MLA_KV_RANK)
            kr_p = kr[:N_P].reshape(BATCH, SEQ, MLA_D_ROPE)
            new_ckv.append(ckv_p)
            new_kr.append(kr_p)
            ckv_s = jnp.concatenate([cache_mla_ckv[:, j], ckv[N_P:].reshape(DEC_BATCH, DEC_SEQ, -1)], axis=1)
            kr_s = jnp.concatenate([cache_mla_krope[:, j], kr[N_P:].reshape(DEC_BATCH, DEC_SEQ, -1)], axis=1)
            k3p, v3p = mla_kv(ckv_p, kr_p, wukv, SEQ)
            k3s, v3s = mla_kv(ckv_s, kr_s, wukv, 512)
            o_p = mla_attention(q3, k3p, v3p, 0, SEQ, SEQ, MLA_HEADS)
            o_s = mla_attention(q3, k3s, v3s, N_P, DEC_SEQ, 256, 2)
            x = proj_residual(o_p, o_s, wo, x, mod)
        elif kind == 1:
            h = normmod_time_major(x, gmix, mod).reshape(N_TOK, D)
            dsk = s5_d[j].reshape(1, D)
            seg = S5_SEG
            wb, wc, a, an = _s5_weights(s5_a_re[j], s5_a_im[j], s5_log_dt[j], s5_b_re[j], s5_b_im[j],
                                        s5_c_re[j], s5_c_im[j], seg)
            zero_h0 = jnp.zeros((BATCH // S5_SUB, 2, S5_NGB, 1, 2 * S5_HALF), F32)
            y_p, fin = s5_scan(h, wb, wc, a, an, dsk, zero_h0, 0, BATCH // S5_SUB, SEQ, False)
            fin = fin.reshape(BATCH // S5_SUB, 2, S5_NGB, S5_SUB, 2, S5_GB, S5_STATE)
            fin = jnp.transpose(fin, (0, 3, 1, 4, 2, 5, 6)).reshape(BATCH, 2, 2, S5_GROUPS, S5_STATE)
            new_s5_re.append(fin[:, :, 0])
            new_s5_im.append(fin[:, :, 1])
            h0 = jnp.stack([state_s5_re[:, j], state_s5_im[:, j]], axis=2)
            h0 = h0.reshape(DEC_BATCH, 2, 2, S5_NGB, S5_HALF)
            h0 = jnp.transpose(h0, (0, 1, 3, 2, 4)).reshape(DEC_BATCH, 2, S5_NGB, 1, 2 * S5_HALF)
            y_s, _ = s5_scan(h, wb, wc, a, an, dsk, h0, N_P, DEC_BATCH, seg, True)
            x = glu_residual(y_p, y_s, s5_w_glu[j].astype(BF16), x, mod)
        else:
            lam_init = 0.8 - 0.6 * math.exp(-0.3 * i)
            wqkv = diff_w_qkv[j].astype(BF16)
            q_p, k_p, v_p, k, v = diff_tokens(x, gmix, mod, wqkv, cos_t, sin_t, latent=False)
            q_s, k_s, v_s = diff_tokens(x, gmix, mod, wqkv, cos_t, sin_t, latent=True)
            new_dk.append(k.reshape(BATCH, SEQ, 2 * DIFF_HEADS, DIFF_DH))
            new_dv.append(v.reshape(BATCH, SEQ, DIFF_HEADS, 2 * DIFF_DH))
            lvecs = [a_.reshape(1, DIFF_DH) for a_ in (diff_lq1[j], diff_lk1[j], diff_lq2[j], diff_lk2[j])]
            subln = diff_subln[j].reshape(1, 2 * DIFF_DH)
            ctx_kv = [(k_p.reshape(BATCH, SEQ, D), v_p.reshape(BATCH, SEQ, D))]
            lat_kv = [(cache_diff_k[:, j].reshape(DEC_BATCH, PAST, D).astype(BF16),
                       cache_diff_v[:, j].reshape(DEC_BATCH, PAST, D).astype(BF16)),
                      (k_s.reshape(DEC_BATCH, DEC_SEQ, D), v_s.reshape(DEC_BATCH, DEC_SEQ, D))]
            o_p = diff_attention(lam_init, lvecs, subln, q_p, ctx_kv, SEQ, SEQ, DIFF_HEADS)
            o_s = diff_attention(lam_init, lvecs, subln, q_s, lat_kv, DEC_SEQ, 256, 1)
            x = proj_residual(o_p, o_s, diff_w_o[j].astype(BF16), x, mod)
        f = i // 2
        if i % 2 == 0:
            x = ffn_residual(x, gffn, mod, ffn_w_in_b, ffn_w_out_b, f)
        else:
            wr = jnp.concatenate([moe_w_router[f], jnp.zeros((D, LANES - N_EXPERTS), F32)], axis=1).astype(BF16)
            br = jnp.concatenate([moe_b_router[f], jnp.zeros((LANES - N_EXPERTS,), F32)]).reshape(1, LANES)
            x = moe_residual(x, gffn, mod, wr, br, moe_w_in_b, moe_w_out_b, f)
    y_p = final_norm(x, norm_final.reshape(1, D), 0, N_P)
    y_s = final_norm(x, norm_final.reshape(1, D), N_P, N_S)
    return (y_p.reshape(BATCH, SEQ, D), y_s.reshape(DEC_BATCH, DEC_SEQ, D),
            jnp.stack(new_ckv, axis=1), jnp.stack(new_kr, axis=1),
            jnp.stack(new_s5_re, axis=1), jnp.stack(new_s5_im, axis=1),
            jnp.stack(new_dk, axis=1), jnp.stack(new_dv, axis=1))
```

```python
import functools
import math

import jax
import jax.numpy as jnp
from jax import lax
from jax.experimental import pallas as pl
from jax.experimental.pallas import tpu as pltpu

D = 1024
BATCH = 16
SEQ = 256
DEPTH = 4
DEC_BATCH = 2
DEC_SEQ = 4096
PAST = 512
GRID_W = 64
N_P = BATCH * SEQ
N_S = DEC_BATCH * DEC_SEQ
N_TOK = N_P + N_S
N_GROUPS = 1 + DEC_BATCH

MLA_HEADS = 8
MLA_Q_RANK = 384
MLA_KV_RANK = 256
MLA_D_NOPE = 128
MLA_D_ROPE = 64
MLA_D_V = 128
MLA_DK = MLA_D_NOPE + MLA_D_ROPE

S5_GROUP = 16
S5_GROUPS = D // S5_GROUP
S5_STATE = 64
S5_GB = 8
S5_NGB = S5_GROUPS // S5_GB
S5_HALF = S5_GB * S5_STATE
S5_NCH = S5_HALF // 128
S5_SUB = 8

DIFF_HEADS = 8
DIFF_DH = D // (2 * DIFF_HEADS)

D_FF = 2816
N_EXPERTS = 8
ROPE_THETA = 10000.0
EPS = 1e-6

LOG2E = math.log2(math.e)
TM = 512
FF_TILE = D_FF
LANES = 128
VMEM_LIMIT = 56 * 1024 * 1024

F32 = jnp.float32
BF16 = jnp.bfloat16


def _cparams(*sem):
    return pltpu.CompilerParams(dimension_semantics=sem, vmem_limit_bytes=VMEM_LIMIT)


def _group_of_tile(i, tm):
    n_p = N_P // tm
    per = DEC_SEQ // tm
    return jnp.where(i < n_p, 0, 1 + (i - n_p) // per)


def _rope_tile(i, tm):
    n_p = N_P // tm
    per = DEC_SEQ // tm
    return jnp.where(i < n_p, 0, 1 + (i - n_p) % per)


def _rms(x, g):
    return x * lax.rsqrt(jnp.mean(x * x, axis=-1, keepdims=True) + EPS) * g


def _normmod(x, g, mod, k_shift, k_scale):
    return _rms(x, g) * (1.0 + mod[k_scale:k_scale + 1, :]) + mod[k_shift:k_shift + 1, :]


def _sigmoid(x):
    return 1.0 / (1.0 + jnp.exp(-x))


def _dot(a, b):
    return jnp.dot(a, b, preferred_element_type=F32)


def _dot_nt(a, b):
    return lax.dot_general(a, b, (((1,), (1,)), ((), ())), preferred_element_type=F32)


def _rope(x, cos, sin):
    lane = lax.broadcasted_iota(jnp.int32, x.shape, 1)
    nxt = pltpu.roll(x, LANES - 16, 1)
    prv = pltpu.roll(x, 16, 1)
    swapped = jnp.where((lane % 32) < 16, nxt, prv)
    return x * cos + swapped * sin


def _ada_kernel(c_ref, w_ref, b_ref, o_ref):
    c = c_ref[...]
    s = (c * _sigmoid(c)).astype(BF16)
    o_ref[...] = _dot(s, w_ref[...].astype(BF16)) + b_ref[...]


def ada_all(cond8, ada_w, ada_b):
    tn = 1536
    return pl.pallas_call(
        _ada_kernel,
        grid=(DEPTH, 6 * D // tn),
        in_specs=[pl.BlockSpec((8, D), lambda l, n: (0, 0)),
                  pl.BlockSpec((None, D, tn), lambda l, n: (l, 0, n)),
                  pl.BlockSpec((None, 1, tn), lambda l, n: (l, 0, n))],
        out_specs=pl.BlockSpec((None, 8, tn), lambda l, n: (l, 0, n)),
        out_shape=jax.ShapeDtypeStruct((DEPTH, 8, 6 * D), F32),
        compiler_params=_cparams("parallel", "parallel"),
        name="ada",
    )(cond8, ada_w, ada_b.reshape(DEPTH, 1, 6 * D))


def _mod_spec(tm):
    return pl.BlockSpec((None, 6, D), lambda i, *_: (_group_of_tile(i, tm), 0, 0))


def _mla_tok_kernel(x_ref, g_ref, mod_ref, w1_ref, qn_ref, wuq_ref, kvn_ref, cos_ref, sin_ref,
                    q_ref, ckv_ref, kr_ref):
    h = _normmod(x_ref[...], g_ref[...], mod_ref[...], 0, 1).astype(BF16)
    t1 = _dot(h, w1_ref[...])
    ql = _rms(t1[:, :MLA_Q_RANK], qn_ref[...]).astype(BF16)
    q = _dot(ql, wuq_ref[...]) * (MLA_DK ** -0.5 * LOG2E)
    c0 = MLA_Q_RANK
    ckv_ref[...] = _rms(t1[:, c0:c0 + MLA_KV_RANK], kvn_ref[...])
    cos = cos_ref[...]
    sin = sin_ref[...]
    kr = _rope(t1[:, c0 + MLA_KV_RANK:c0 + MLA_KV_RANK + LANES], cos, sin)
    kr_ref[...] = kr[:, :MLA_D_ROPE]
    n_nope = MLA_HEADS * MLA_D_NOPE
    for pair in range(MLA_HEADS // 2):
        qr = _rope(q[:, n_nope + pair * LANES:n_nope + (pair + 1) * LANES], cos, sin).astype(BF16)
        for sub in range(2):
            hd = 2 * pair + sub
            q_ref[hd, :, 0:MLA_D_NOPE] = q[:, hd * MLA_D_NOPE:(hd + 1) * MLA_D_NOPE].astype(BF16)
            q_ref[hd, :, MLA_D_NOPE:MLA_DK] = qr[:, sub * MLA_D_ROPE:(sub + 1) * MLA_D_ROPE]


def mla_tokens(x, g, mod, w1, qn, wuq, kvn, cos_t, sin_t):
    nt = N_TOK // TM
    const = lambda shape: pl.BlockSpec(shape, lambda i: (0,) * len(shape))
    return pl.pallas_call(
        _mla_tok_kernel,
        grid=(nt,),
        in_specs=[pl.BlockSpec((TM, D), lambda i: (i, 0)), const((1, D)), _mod_spec(TM),
                  const(w1.shape), const((1, MLA_Q_RANK)), const(wuq.shape), const((1, MLA_KV_RANK)),
                  pl.BlockSpec((TM, LANES), lambda i: (_rope_tile(i, TM), 0)),
                  pl.BlockSpec((TM, LANES), lambda i: (_rope_tile(i, TM), 0))],
        out_specs=[pl.BlockSpec((MLA_HEADS, TM, MLA_DK), lambda i: (0, i, 0)),
                   pl.BlockSpec((TM, MLA_KV_RANK), lambda i: (i, 0)),
                   pl.BlockSpec((TM, MLA_D_ROPE), lambda i: (i, 0))],
        out_shape=[jax.ShapeDtypeStruct((MLA_HEADS, N_TOK, MLA_DK), BF16),
                   jax.ShapeDtypeStruct((N_TOK, MLA_KV_RANK), F32),
                   jax.ShapeDtypeStruct((N_TOK, MLA_D_ROPE), F32)],
        compiler_params=_cparams("parallel"),
        name="mla_tokens",
    )(x, g, mod, w1, qn, wuq, kvn, cos_t, sin_t)


def _mla_kv_kernel(ckv_ref, kr_ref, w_ref, k_ref, v_ref):
    kv = _dot(ckv_ref[...].astype(BF16), w_ref[...])
    kr = kr_ref[...].astype(BF16)
    n_nope = MLA_HEADS * MLA_D_NOPE
    for hd in range(MLA_HEADS):
        k_ref[hd, :, 0:MLA_D_NOPE] = kv[:, hd * MLA_D_NOPE:(hd + 1) * MLA_D_NOPE].astype(BF16)
        k_ref[hd, :, MLA_D_NOPE:MLA_DK] = kr
        v_ref[hd] = kv[:, n_nope + hd * MLA_D_V:n_nope + (hd + 1) * MLA_D_V].astype(BF16)


def mla_kv(ckv, kr, wukv, ts):
    nb, s, _ = ckv.shape
    return pl.pallas_call(
        _mla_kv_kernel,
        grid=(nb, s // ts),
        in_specs=[pl.BlockSpec((None, ts, MLA_KV_RANK), lambda b, t: (b, t, 0)),
                  pl.BlockSpec((None, ts, MLA_D_ROPE), lambda b, t: (b, t, 0)),
                  pl.BlockSpec(wukv.shape, lambda b, t: (0, 0))],
        out_specs=[pl.BlockSpec((None, MLA_HEADS, ts, MLA_DK), lambda b, t: (b, 0, t, 0)),
                   pl.BlockSpec((None, MLA_HEADS, ts, MLA_D_V), lambda b, t: (b, 0, t, 0))],
        out_shape=[jax.ShapeDtypeStruct((nb, MLA_HEADS, s, MLA_DK), BF16),
                   jax.ShapeDtypeStruct((nb, MLA_HEADS, s, MLA_D_V), BF16)],
        compiler_params=_cparams("parallel", "parallel"),
        name="mla_kv",
    )(ckv, kr, wukv)


def _mla_attn_kernel(hps, q_ref, k_ref, v_ref, o_ref):
    for hd in range(hps):
        s = _dot_nt(q_ref[hd], k_ref[hd])
        p = jnp.exp2(s - jnp.max(s, axis=-1, keepdims=True))
        l = jnp.sum(p, axis=-1, keepdims=True)
        o = _dot(p.astype(BF16), v_ref[hd]) / l
        o_ref[:, hd * MLA_D_V:(hd + 1) * MLA_D_V] = o.astype(BF16)


def mla_attention(q3, k3, v3, row0, seq, tq, hps):
    nb, _, s, _ = k3.shape
    nq = seq // tq
    base = row0 // tq
    return pl.pallas_call(
        functools.partial(_mla_attn_kernel, hps),
        grid=(nb, MLA_HEADS // hps, nq),
        in_specs=[pl.BlockSpec((hps, tq, MLA_DK), lambda b, h, i: (h, base + b * nq + i, 0)),
                  pl.BlockSpec((None, hps, s, MLA_DK), lambda b, h, i: (b, h, 0, 0)),
                  pl.BlockSpec((None, hps, s, MLA_D_V), lambda b, h, i: (b, h, 0, 0))],
        out_specs=pl.BlockSpec((tq, hps * MLA_D_V), lambda b, h, i: (b * nq + i, h)),
        out_shape=jax.ShapeDtypeStruct((nb * seq, MLA_HEADS * MLA_D_V), BF16),
        compiler_params=_cparams("parallel", "parallel", "parallel"),
        name="mla_attn_%d" % s,
    )(q3, k3, v3)


def _proj_res_kernel(op_ref, os_ref, w_ref, x_ref, mod_ref, out_ref):
    o = jnp.where(pl.program_id(0) < N_P // TM, op_ref[...], os_ref[...])
    out_ref[...] = x_ref[...] + mod_ref[2:3, :] * _dot(o, w_ref[...])


def proj_residual(o_p, o_s, w, x, mod):
    nt = N_TOK // TM
    n_p = N_P // TM
    return pl.pallas_call(
        _proj_res_kernel,
        grid=(nt,),
        in_specs=[pl.BlockSpec((TM, D), lambda i: (jnp.minimum(i, n_p - 1), 0)),
                  pl.BlockSpec((TM, D), lambda i: (jnp.maximum(i - n_p, 0), 0)),
                  pl.BlockSpec((D, D), lambda i: (0, 0)),
                  pl.BlockSpec((TM, D), lambda i: (i, 0)), _mod_spec(TM)],
        out_specs=pl.BlockSpec((TM, D), lambda i: (i, 0)),
        out_shape=jax.ShapeDtypeStruct((N_TOK, D), F32),
        compiler_params=_cparams("parallel"),
        name="proj_residual",
    )(o_p, o_s, w, x, mod)


def _diff_tok_kernel(latent, x_ref, g_ref, mod_ref, w_ref, *refs):
    if latent:
        cos_ref, sin_ref, q_ref, kc_ref, vc_ref = refs
        cos = cos_ref[...]
        sin = sin_ref[...]
        rot = lambda t: _rope(t, cos, sin)
    else:
        q_ref, kc_ref, vc_ref, k_ref, v_ref = refs
        rot = lambda t: t
    h = _normmod(x_ref[...], g_ref[...], mod_ref[...], 0, 1).astype(BF16)
    wide = 2 * LANES
    rot2 = lambda t: jnp.concatenate([rot(t[:, :LANES]), rot(t[:, LANES:])], axis=1)
    for c in range(D // wide):
        sl = slice(c * wide, (c + 1) * wide)
        q = _dot(h, w_ref[:, c * wide:(c + 1) * wide]) * (DIFF_DH ** -0.5 * LOG2E)
        k = _dot(h, w_ref[:, D + c * wide:D + (c + 1) * wide])
        v = _dot(h, w_ref[:, 2 * D + c * wide:2 * D + (c + 1) * wide])
        q_ref[:, sl] = rot2(q).astype(BF16)
        kc_ref[:, sl] = rot2(k).astype(BF16)
        vc_ref[:, sl] = v.astype(BF16)
        if not latent:
            k_ref[:, sl] = k
            v_ref[:, sl] = v


def diff_tokens(x, g, mod, wqkv, cos_t, sin_t, latent):
    n_rows = N_S if latent else N_P
    base = (N_P if latent else 0) // TM
    row = lambda i: (i, 0)
    in_specs = [pl.BlockSpec((TM, D), lambda i: (base + i, 0)), pl.BlockSpec((1, D), lambda i: (0, 0)),
                pl.BlockSpec((None, 6, D), lambda i: (_group_of_tile(base + i, TM), 0, 0)),
                pl.BlockSpec((D, 3 * D), lambda i: (0, 0))]
    args = [x, g, mod, wqkv]
    out_shape = [jax.ShapeDtypeStruct((n_rows, D), BF16)] * 3
    if latent:
        rope_spec = pl.BlockSpec((TM, LANES), lambda i: (_rope_tile(base + i, TM), 0))
        in_specs += [rope_spec, rope_spec]
        args += [cos_t, sin_t]
    else:
        out_shape += [jax.ShapeDtypeStruct((n_rows, D), F32)] * 2
    return pl.pallas_call(
        functools.partial(_diff_tok_kernel, latent),
        grid=(n_rows // TM,),
        in_specs=in_specs,
        out_specs=[pl.BlockSpec((TM, D), row)] * len(out_shape),
        out_shape=out_shape,
        compiler_params=_cparams("parallel"),
        name="diff_tokens_lat" if latent else "diff_tokens_ctx",
    )(*args)


def _diff_attn_kernel(lam_init, n_kv, pairs, lq1_ref, lk1_ref, lq2_ref, lk2_ref, sub_ref, q_ref, *refs):
    kv_refs = refs[:2 * n_kv]
    o_ref = refs[2 * n_kv]
    lam = (jnp.exp(jnp.sum(lq1_ref[...] * lk1_ref[...], axis=-1, keepdims=True))
           - jnp.exp(jnp.sum(lq2_ref[...] * lk2_ref[...], axis=-1, keepdims=True)) + lam_init)
    for pr in range(pairs):
        lanes = slice(pr * LANES, (pr + 1) * LANES)
        q = q_ref[:, lanes]
        lane = lax.broadcasted_iota(jnp.int32, q.shape, 1)
        zero = jnp.zeros_like(q)

        def probs(qh):
            s = jnp.concatenate([_dot_nt(qh, kv_refs[2 * i][:, lanes]) for i in range(n_kv)], axis=1)
            p = jnp.exp2(s - jnp.max(s, axis=-1, keepdims=True))
            return p, jnp.sum(p, axis=-1, keepdims=True)

        p1, l1 = probs(jnp.where(lane < DIFF_DH, q, zero))
        p2, l2 = probs(jnp.where(lane >= DIFF_DH, q, zero))
        att = (p1 + (-lam * l1 / l2) * p2).astype(BF16)
        acc = jnp.zeros(q.shape, F32)
        col = 0
        for i in range(n_kv):
            n = kv_refs[2 * i + 1].shape[0]
            acc = acc + _dot(att[:, col:col + n], kv_refs[2 * i + 1][:, lanes])
            col += n
        o = acc / l1
        o_ref[:, lanes] = (_rms(o, sub_ref[...]) * (1.0 - lam_init)).astype(BF16)


def diff_attention(lam_init, lvecs, subln, q, kvs, seq, tq, pairs):
    nb = kvs[0][0].shape[0]
    nq = seq // tq
    n_keys = sum(k.shape[1] for k, _ in kvs)
    width = pairs * LANES
    vec = pl.BlockSpec((1, DIFF_DH), lambda b, h, i: (0, 0))
    kv_specs, kv_args = [], []
    for k, v in kvs:
        spec = pl.BlockSpec((None, k.shape[1], width), lambda b, h, i: (b, 0, h))
        kv_specs += [spec, spec]
        kv_args += [k, v]
    return pl.pallas_call(
        functools.partial(_diff_attn_kernel, lam_init, len(kvs), pairs),
        grid=(nb, DIFF_HEADS // pairs, nq),
        in_specs=[vec, vec, vec, vec, pl.BlockSpec((1, 2 * DIFF_DH), lambda b, h, i: (0, 0)),
                  pl.BlockSpec((tq, width), lambda b, h, i: (b * nq + i, h))] + kv_specs,
        out_specs=pl.BlockSpec((tq, width), lambda b, h, i: (b * nq + i, h)),
        out_shape=jax.ShapeDtypeStruct((nb * seq, D), BF16),
        compiler_params=_cparams("parallel", "parallel", "parallel"),
        name="diff_attn_%d" % n_keys,
    )(*lvecs, subln, q, *kv_args)


def _normmod_kernel(x_ref, g_ref, mod_ref, h_ref):
    h_ref[...] = _normmod(x_ref[...], g_ref[...], mod_ref[...], 0, 1)


S5_T = SEQ
S5_SEG = DEC_SEQ // S5_SUB
S5_TILES_P = N_P // S5_T
S5_PER_SEG = S5_SEG // S5_T


def _s5_tile_pos(i):
    k = i - S5_TILES_P
    per_batch = S5_SUB * S5_PER_SEG
    lat_row = S5_TILES_P // S5_SUB + (k // per_batch) * S5_PER_SEG + k % S5_PER_SEG
    lat_col = (k % per_batch) // S5_PER_SEG
    is_p = i < S5_TILES_P
    return jnp.where(is_p, i // S5_SUB, lat_row), jnp.where(is_p, i % S5_SUB, lat_col)


def normmod_time_major(x, g, mod):
    nt = N_TOK // S5_T
    return pl.pallas_call(
        _normmod_kernel,
        grid=(nt,),
        in_specs=[pl.BlockSpec((S5_T, D), lambda i: (i, 0)), pl.BlockSpec((1, D), lambda i: (0, 0)),
                  _mod_spec(S5_T)],
        out_specs=pl.BlockSpec((S5_T, D), lambda i: _s5_tile_pos(i)),
        out_shape=jax.ShapeDtypeStruct((N_TOK // S5_SUB, S5_SUB * D), F32),
        compiler_params=_cparams("parallel"),
        name="normmod",
    )(x, g, mod)


def _s5_kernel(chained, n, u_ref, wb_ref, wc_ref, a_ref, an_ref, dsk_ref, h0_ref, y_ref, fin_ref,
               bu_ref, ini_ref):
    d = pl.program_id(2)
    rows = S5_SUB * n
    chunk = 512
    nch = S5_NCH
    for r in range(rows // chunk):
        rs = slice(r * chunk, (r + 1) * chunk)
        bu = _dot(u_ref[rs, :].astype(BF16), wb_ref[...])
        for c in range(2 * nch):
            bu_ref[c, rs, :] = bu[:, c * LANES:(c + 1) * LANES]
    ar = [jnp.broadcast_to(a_ref[0:1, c * LANES:(c + 1) * LANES], (S5_SUB, LANES)) for c in range(nch)]
    ai = [jnp.broadcast_to(a_ref[1:2, c * LANES:(c + 1) * LANES], (S5_SUB, LANES)) for c in range(nch)]

    def step_index(s):
        return jnp.where(d == 0, s, n - 1 - s)

    def step_rows(t):
        return pl.ds(pl.multiple_of(t * S5_SUB, S5_SUB), S5_SUB)

    def advance(h, t):
        out = [None] * (2 * nch)
        for c in range(nch):
            br = bu_ref[c, step_rows(t), :]
            bi = bu_ref[nch + c, step_rows(t), :]
            out[c] = ar[c] * h[c] - ai[c] * h[nch + c] + br
            out[nch + c] = ar[c] * h[nch + c] + ai[c] * h[c] + bi
        return out

    unroll = 4
    zeros = [jnp.zeros((S5_SUB, LANES), F32) for _ in range(2 * nch)]

    if chained:
        def local_body(s, h):
            h = list(h)
            for k in range(unroll):
                h = advance(h, step_index(s * unroll + k))
            return tuple(h)

        ends = lax.fori_loop(0, n // unroll, local_body, tuple(zeros))
        sub_row = lax.broadcasted_iota(jnp.int32, (S5_SUB, LANES), 0)
        cur = [h0_ref[:, c * LANES:(c + 1) * LANES] for c in range(2 * nch)]
        anr = [an_ref[0:1, c * LANES:(c + 1) * LANES] for c in range(nch)]
        ani = [an_ref[1:2, c * LANES:(c + 1) * LANES] for c in range(nch)]
        for kk in range(S5_SUB):
            j = jnp.where(d == 0, kk, S5_SUB - 1 - kk)
            nxt = [None] * (2 * nch)
            for c in range(2 * nch):
                ini_ref[c, pl.ds(j, 1), :] = cur[c]
            for c in range(nch):
                er = jnp.sum(jnp.where(sub_row == j, ends[c], 0.0), axis=0, keepdims=True)
                ei = jnp.sum(jnp.where(sub_row == j, ends[nch + c], 0.0), axis=0, keepdims=True)
                nxt[c] = er + anr[c] * cur[c] - ani[c] * cur[nch + c]
                nxt[nch + c] = ei + anr[c] * cur[nch + c] + ani[c] * cur[c]
            cur = nxt
        h_init = [ini_ref[c] for c in range(2 * nch)]
    else:
        h_init = zeros

    def body(s, h):
        h = list(h)
        for k in range(unroll):
            t = step_index(s * unroll + k)
            h = advance(h, t)
            for c in range(2 * nch):
                bu_ref[c, step_rows(t), :] = h[c]
        return tuple(h)

    fin = lax.fori_loop(0, n // unroll, body, tuple(h_init))
    for c in range(2 * nch):
        fin_ref[:, c * LANES:(c + 1) * LANES] = fin[c]

    @pl.when(d == 0)
    def _():
        y_ref[...] = dsk_ref[...] * u_ref[...]

    for r in range(rows // chunk):
        rs = slice(r * chunk, (r + 1) * chunk)
        hs = jnp.concatenate([bu_ref[c, rs, :].astype(BF16) for c in range(2 * nch)], axis=1)
        y_ref[rs, :] += _dot(hs, wc_ref[...])


def s5_scan(h, wb, wc, a, an, dskip, h0, row0, n_blocks, n, chained):
    rows = S5_SUB * n
    base = row0 // rows
    kern = functools.partial(_s5_kernel, chained, n)
    return pl.pallas_call(
        kern,
        grid=(n_blocks, S5_NGB, 2),
        in_specs=[pl.BlockSpec((rows, LANES), lambda r, c, d: (base + r, c)),
                  pl.BlockSpec((None, None, LANES, 2 * S5_HALF), lambda r, c, d: (d, c, 0, 0)),
                  pl.BlockSpec((None, None, 2 * S5_HALF, LANES), lambda r, c, d: (d, c, 0, 0)),
                  pl.BlockSpec((None, None, 2, S5_HALF), lambda r, c, d: (d, c, 0, 0)),
                  pl.BlockSpec((None, None, 2, S5_HALF), lambda r, c, d: (d, c, 0, 0)),
                  pl.BlockSpec((1, LANES), lambda r, c, d: (0, c)),
                  pl.BlockSpec((None, None, None, 1, 2 * S5_HALF), lambda r, c, d: (r, d, c, 0, 0))],
        out_specs=[pl.BlockSpec((rows, LANES), lambda r, c, d: (r, c)),
                   pl.BlockSpec((None, None, None, S5_SUB, 2 * S5_HALF), lambda r, c, d: (r, d, c, 0, 0))],
        out_shape=[jax.ShapeDtypeStruct((n_blocks * rows, D), F32),
                   jax.ShapeDtypeStruct((n_blocks, 2, S5_NGB, S5_SUB, 2 * S5_HALF), F32)],
        scratch_shapes=[pltpu.VMEM((2 * S5_NCH, rows, LANES), F32),
                        pltpu.VMEM((2 * S5_NCH, S5_SUB, LANES), F32)],
        compiler_params=_cparams("parallel", "parallel", "arbitrary"),
        name="s5_scan_%d" % n,
    )(h, wb, wc, a, an, dskip, h0)


def _glu_res_kernel(yp_ref, ys_ref, w_ref, x_ref, mod_ref, out_ref):
    y = jnp.where(pl.program_id(0) < S5_TILES_P, yp_ref[...], ys_ref[...])
    g = 0.5 * y * (1.0 + jnp.tanh(math.sqrt(2.0 / math.pi) * (y + 0.044715 * (y * y * y))))
    t = _dot(g.astype(BF16), w_ref[...])
    out_ref[...] = x_ref[...] + mod_ref[2:3, :] * (t[:, :D] * _sigmoid(t[:, D:]))


def glu_residual(y_p, y_s, w, x, mod):
    nt = N_TOK // S5_T
    n_blk_p = S5_TILES_P // S5_SUB

    def yp_map(i):
        r, c = _s5_tile_pos(jnp.minimum(i, S5_TILES_P - 1))
        return r, c

    def ys_map(i):
        r, c = _s5_tile_pos(jnp.maximum(i, S5_TILES_P))
        return r - n_blk_p, c

    return pl.pallas_call(
        _glu_res_kernel,
        grid=(nt,),
        in_specs=[pl.BlockSpec((S5_T, D), yp_map), pl.BlockSpec((S5_T, D), ys_map),
                  pl.BlockSpec((D, 2 * D), lambda i: (0, 0)),
                  pl.BlockSpec((S5_T, D), lambda i: (i, 0)), _mod_spec(S5_T)],
        out_specs=pl.BlockSpec((S5_T, D), lambda i: (i, 0)),
        out_shape=jax.ShapeDtypeStruct((N_TOK, D), F32),
        compiler_params=_cparams("parallel"),
        name="glu_residual",
    )(y_p.reshape(N_P // S5_SUB, S5_SUB * D), y_s.reshape(N_S // S5_SUB, S5_SUB * D), w, x, mod)


FF_CHUNKS = ((0, 768), (768, 1536), (1536, 2304), (2304, FF_TILE))


def _swiglu_partial(h, wa_ref, wb_ref, wo_ref):
    out = None
    for lo, hi in FF_CHUNKS:
        a = _dot(h, wa_ref[:, lo:hi])
        b = _dot(h, wb_ref[:, lo:hi])
        act = (a * _sigmoid(a) * b).astype(BF16)
        part = _dot(act, wo_ref[lo:hi, :])
        out = part if out is None else out + part
    return out


def _ffn_kernel(x_ref, g_ref, mod_ref, wa_ref, wb_ref, wo_ref, out_ref, h_scr, acc_scr):
    f = pl.program_id(1)

    @pl.when(f == 0)
    def _():
        h_scr[...] = _normmod(x_ref[...], g_ref[...], mod_ref[...], 3, 4).astype(BF16)
        acc_scr[...] = jnp.zeros_like(acc_scr)

    acc_scr[...] += _swiglu_partial(h_scr[...], wa_ref, wb_ref, wo_ref)

    @pl.when(f == pl.num_programs(1) - 1)
    def _():
        out_ref[...] = x_ref[...] + mod_ref[5:6, :] * acc_scr[...]


def ffn_residual(x, g, mod, w_in, w_out, layer):
    nt = N_TOK // TM
    nf = D_FF // FF_TILE
    return pl.pallas_call(
        _ffn_kernel,
        grid=(nt, nf),
        in_specs=[pl.BlockSpec((TM, D), lambda i, f: (i, 0)), pl.BlockSpec((1, D), lambda i, f: (0, 0)),
                  _mod_spec(TM),
                  pl.BlockSpec((None, D, FF_TILE), lambda i, f: (layer, 0, f)),
                  pl.BlockSpec((None, D, FF_TILE), lambda i, f: (layer, 0, f + nf)),
                  pl.BlockSpec((None, FF_TILE, D), lambda i, f: (layer, f, 0))],
        out_specs=pl.BlockSpec((TM, D), lambda i, f: (i, 0)),
        out_shape=jax.ShapeDtypeStruct((N_TOK, D), F32),
        scratch_shapes=[pltpu.VMEM((TM, D), BF16), pltpu.VMEM((TM, D), F32)],
        compiler_params=_cparams("parallel", "arbitrary"),
        name="ffn",
    )(x, g, mod, w_in, w_in, w_out)


TS = 512
N_SLOTS = 2 * N_TOK + N_EXPERTS * TS
NT_S = N_SLOTS // TS
TG = 256
PIECE = 256
MAX_PIECES = (TM + 16 + PIECE - 1) // PIECE


def _moe_route_kernel(x_ref, g_ref, mod_ref, wr_ref, br_ref, h_ref, route_ref, rows_ref, run_ref, carry_scr):
    @pl.when(pl.program_id(0) == 0)
    def _():
        carry_scr[...] = jnp.zeros_like(carry_scr)

    h = _normmod(x_ref[...], g_ref[...], mod_ref[...], 3, 4).astype(BF16)
    h_ref[...] = h
    logits = _dot(h, wr_ref[...]) + br_ref[...]
    lane = lax.broadcasted_iota(jnp.int32, logits.shape, 1)
    neg = jnp.float32(-jnp.inf)
    lg = jnp.where(lane < N_EXPERTS, logits, neg)
    v1 = jnp.max(lg, axis=-1, keepdims=True)
    i1 = jnp.min(jnp.where(lg == v1, lane, LANES), axis=-1, keepdims=True)
    lg2 = jnp.where(lane == i1, neg, lg)
    v2 = jnp.max(lg2, axis=-1, keepdims=True)
    i2 = jnp.min(jnp.where(lg2 == v2, lane, LANES), axis=-1, keepdims=True)
    e2 = jnp.exp(v2 - v1)
    g1 = 1.0 / (1.0 + e2)
    g2 = e2 / (1.0 + e2)
    oh1 = lane == i1
    oh2 = lane == i2
    sel = jnp.where(oh1 | oh2, 1.0, 0.0)
    r = lax.broadcasted_iota(jnp.int32, (TM, TM), 0)
    c = lax.broadcasted_iota(jnp.int32, (TM, TM), 1)
    tri = jnp.where(c < r, 1.0, 0.0).astype(BF16)
    rank = _dot(tri, sel.astype(BF16)) + carry_scr[0:1, :]
    r1 = jnp.sum(jnp.where(oh1, rank, 0.0), axis=-1, keepdims=True)
    r2 = jnp.sum(jnp.where(oh2, rank, 0.0), axis=-1, keepdims=True)
    cols = (i1.astype(F32), i2.astype(F32), g1, g2, r1, r2)
    route = jnp.zeros(logits.shape, F32)
    for k, v in enumerate(cols):
        route = jnp.where(lane == k, v, route)
    route_ref[...] = route
    rows_ref[...] = route.T[0:8, :]
    carry_scr[...] = carry_scr[...] + jnp.sum(sel, axis=0, keepdims=True)
    run_ref[...] = carry_scr[...]


def moe_route(x, g, mod, w_router, b_router):
    nt = N_TOK // TM
    return pl.pallas_call(
        _moe_route_kernel,
        grid=(nt,),
        in_specs=[pl.BlockSpec((TM, D), lambda i: (i, 0)), pl.BlockSpec((1, D), lambda i: (0, 0)), _mod_spec(TM),
                  pl.BlockSpec((D, LANES), lambda i: (0, 0)), pl.BlockSpec((1, LANES), lambda i: (0, 0))],
        out_specs=[pl.BlockSpec((TM, D), lambda i: (i, 0)), pl.BlockSpec((TM, LANES), lambda i: (i, 0)),
                   pl.BlockSpec((8, TM), lambda i: (0, i)),
                   pl.BlockSpec((None, 8, LANES), lambda i: (i, 0, 0))],
        out_shape=[jax.ShapeDtypeStruct((N_TOK, D), BF16), jax.ShapeDtypeStruct((N_TOK, LANES), F32),
                   jax.ShapeDtypeStruct((8, N_TOK), F32),
                   jax.ShapeDtypeStruct((nt, 8, LANES), F32)],
        scratch_shapes=[pltpu.VMEM((8, LANES), F32)],
        compiler_params=_cparams("arbitrary"),
        name="moe_route",
    )(x, g, mod, w_router, b_router)


def _moe_gather_kernel(used_ref, clo_ref, chi_ref, pos_ref, gates_ref, h_ref, xs_ref, gate_ref, gat_scr, gsum_scr):
    i = pl.program_id(0)
    gat_scr[...] = jnp.zeros_like(gat_scr)
    gsum_scr[...] = jnp.zeros_like(gsum_scr)

    @pl.when(used_ref[i] > 0)
    def _():
        slot = i * TG + lax.broadcasted_iota(jnp.int32, (TG, TM), 0)

        def body(c, carry):
            base = pl.multiple_of(c * TM, TM)
            m1 = pos_ref[0:1, pl.ds(base, TM)] == slot
            m2 = pos_ref[1:2, pl.ds(base, TM)] == slot
            pick = jnp.where(m1 | m2, 1.0, 0.0).astype(BF16)
            gat_scr[...] += _dot(pick, h_ref[pl.ds(base, TM), :])
            g = jnp.where(m1, gates_ref[0:1, pl.ds(base, TM)], 0.0) + jnp.where(m2, gates_ref[1:2, pl.ds(base, TM)], 0.0)
            gsum_scr[...] += jnp.sum(g, axis=-1, keepdims=True)
            return carry

        lax.fori_loop(clo_ref[i], chi_ref[i] + 1, body, 0)

    xs_ref[...] = gat_scr[...].astype(BF16)
    gate_ref[...] = gsum_scr[...]


def moe_gather(used, clo, chi, pos_rows, gate_rows, h_b):
    whole = lambda shape: pl.BlockSpec(shape, lambda i, *_: (0, 0), pipeline_mode=pl.Buffered(1))
    grid_spec = pltpu.PrefetchScalarGridSpec(
        num_scalar_prefetch=3,
        grid=(N_SLOTS // TG,),
        in_specs=[whole((8, N_TOK)), whole((8, N_TOK)), whole((N_TOK, D))],
        out_specs=[pl.BlockSpec((TG, D), lambda i, *_: (i, 0)), pl.BlockSpec((TG, 1), lambda i, *_: (i, 0))],
        scratch_shapes=[pltpu.VMEM((TG, D), F32), pltpu.VMEM((TG, 1), F32)],
    )
    return pl.pallas_call(
        _moe_gather_kernel,
        grid_spec=grid_spec,
        out_shape=[jax.ShapeDtypeStruct((N_SLOTS, D), BF16), jax.ShapeDtypeStruct((N_SLOTS, 1), F32)],
        compiler_params=_cparams("arbitrary"),
        name="moe_gather",
    )(used, clo, chi, pos_rows, gate_rows, h_b)


def _moe_ffn_kernel(texp_ref, used_ref, xs_ref, gate_ref, wa_ref, wb_ref, wo_ref, ys_ref, acc_scr):
    i = pl.program_id(0)
    f = pl.program_id(1)
    live = used_ref[i] > 0

    @pl.when(f == 0)
    def _():
        acc_scr[...] = jnp.zeros_like(acc_scr)

    @pl.when(live)
    def _():
        acc_scr[...] += _swiglu_partial(xs_ref[...], wa_ref, wb_ref, wo_ref)

    @pl.when(f == pl.num_programs(1) - 1)
    def _():
        ys_ref[...] = (gate_ref[...] * acc_scr[...]).astype(BF16)


def moe_ffn(texp, used, xs, gate_col, w_in, w_out, layer):
    nf = D_FF // FF_TILE
    grid_spec = pltpu.PrefetchScalarGridSpec(
        num_scalar_prefetch=2,
        grid=(NT_S, nf),
        in_specs=[pl.BlockSpec((TS, D), lambda i, f, *_: (i, 0)),
                  pl.BlockSpec((TS, 1), lambda i, f, *_: (i, 0)),
                  pl.BlockSpec((None, None, D, FF_TILE), lambda i, f, texp, *_: (layer, texp[i], 0, f)),
                  pl.BlockSpec((None, None, D, FF_TILE), lambda i, f, texp, *_: (layer, texp[i], 0, f + nf)),
                  pl.BlockSpec((None, None, FF_TILE, D), lambda i, f, texp, *_: (layer, texp[i], f, 0))],
        out_specs=pl.BlockSpec((TS, D), lambda i, f, *_: (i, 0)),
        scratch_shapes=[pltpu.VMEM((TS, D), F32)],
    )
    return pl.pallas_call(
        _moe_ffn_kernel,
        grid_spec=grid_spec,
        out_shape=jax.ShapeDtypeStruct((N_SLOTS, D), BF16),
        compiler_params=_cparams("parallel", "arbitrary"),
        name="moe_ffn",
    )(texp, used, xs, gate_col, w_in, w_in, w_out)


def _moe_combine_kernel(start_ref, npc_ref, lo_ref, hi_ref, off_ref, x_ref, mod_ref, route_ref, ys_hbm, out_ref,
                        buf, acc_scr, sem):
    i = pl.program_id(0)
    n_tiles = pl.num_programs(0)

    def piece_copy(tile, e, k):
        half = tile % 2
        s = pl.multiple_of(start_ref[tile * N_EXPERTS + e] + k * PIECE, 16)
        return pltpu.make_async_copy(ys_hbm.at[pl.ds(s, PIECE), :], buf.at[half, e, pl.ds(k * PIECE, PIECE), :],
                                     sem.at[half, e, k])

    def start_tile(tile):
        for e in range(N_EXPERTS):
            for k in range(MAX_PIECES):
                @pl.when(k < npc_ref[tile * N_EXPERTS + e])
                def _():
                    piece_copy(tile, e, k).start()

    @pl.when(i == 0)
    def _():
        start_tile(i)

    @pl.when(i + 1 < n_tiles)
    def _():
        start_tile(i + 1)

    half = i % 2
    acc_scr[...] = jnp.zeros_like(acc_scr)
    route = route_ref[...]
    e1 = route[:, 0:1].astype(jnp.int32)
    e2 = route[:, 1:2].astype(jnp.int32)
    pos1 = route[:, 4:5].astype(jnp.int32)
    pos2 = route[:, 5:6].astype(jnp.int32)
    for e in range(N_EXPERTS):
        pos1 = pos1 + jnp.where(e1 == e, off_ref[e], 0)
        pos2 = pos2 + jnp.where(e2 == e, off_ref[e], 0)
    lane = lax.broadcasted_iota(jnp.int32, (TM, PIECE), 1)
    for e in range(N_EXPERTS):
        lo = lo_ref[i * N_EXPERTS + e]
        hi = hi_ref[i * N_EXPERTS + e]
        p1 = jnp.where((pos1 >= lo) & (pos1 < hi), pos1, -1)
        p2 = jnp.where((pos2 >= lo) & (pos2 < hi), pos2, -1)
        for k in range(MAX_PIECES):
            @pl.when(k < npc_ref[i * N_EXPERTS + e])
            def _():
                piece_copy(i, e, k).wait()
                base = start_ref[i * N_EXPERTS + e] + k * PIECE
                pick = jnp.where((p1 - base == lane) | (p2 - base == lane), 1.0, 0.0).astype(BF16)
                acc_scr[...] += _dot(pick, buf[half, e, pl.ds(k * PIECE, PIECE), :])

    out_ref[...] = x_ref[...] + mod_ref[5:6, :] * acc_scr[...]


def moe_combine(start, npc, lo, hi, off, x, mod, route, ys):
    nt = N_TOK // TM
    grid_spec = pltpu.PrefetchScalarGridSpec(
        num_scalar_prefetch=5,
        grid=(nt,),
        in_specs=[pl.BlockSpec((TM, D), lambda i, *_: (i, 0)),
                  pl.BlockSpec((None, 6, D), lambda i, *_: (_group_of_tile(i, TM), 0, 0)),
                  pl.BlockSpec((TM, LANES), lambda i, *_: (i, 0)),
                  pl.BlockSpec(memory_space=pl.ANY)],
        out_specs=pl.BlockSpec((TM, D), lambda i, *_: (i, 0)),
        scratch_shapes=[pltpu.VMEM((2, N_EXPERTS, MAX_PIECES * PIECE, D), BF16), pltpu.VMEM((TM, D), F32),
                        pltpu.SemaphoreType.DMA((2, N_EXPERTS, MAX_PIECES))],
    )
    return pl.pallas_call(
        _moe_combine_kernel,
        grid_spec=grid_spec,
        out_shape=jax.ShapeDtypeStruct((N_TOK, D), F32),
        compiler_params=_cparams("arbitrary"),
        name="moe_combine",
    )(start, npc, lo, hi, off, x, mod, route, ys)


def moe_residual(x, g, mod, w_router, b_router, w_in, w_out, layer):
    nt = N_TOK // TM
    i32 = jnp.int32
    h_b, route, rows, run = moe_route(x, g, mod, w_router, b_router)
    run = run[:, 0, :N_EXPERTS].astype(i32)
    run_prev = jnp.concatenate([jnp.zeros((1, N_EXPERTS), i32), run[:-1]], axis=0)
    total = run[-1]
    padded = (total + TS - 1) // TS * TS
    off_end = jnp.cumsum(padded)
    off = off_end - padded
    experts = jnp.arange(N_EXPERTS, dtype=i32)
    off_of = lambda e_row: jnp.sum(jnp.where(e_row[None, :].astype(i32) == experts[:, None], off[:, None], 0), axis=0)
    pos1 = rows[4].astype(i32) + off_of(rows[0])
    pos2 = rows[5].astype(i32) + off_of(rows[1])
    pos_rows = jnp.concatenate([pos1[None], pos2[None], jnp.full((6, N_TOK), -1, i32)], axis=0)
    gate_rows = jnp.concatenate([rows[2:4], jnp.zeros((6, N_TOK), F32)], axis=0)
    def tile_tables(tile):
        tile_start = jnp.arange(N_SLOTS // tile, dtype=i32) * tile
        texp = jnp.minimum(jnp.sum((tile_start[:, None] >= off_end[None, :]).astype(i32), axis=1), N_EXPERTS - 1)
        mine = texp[:, None] == experts[None, :]
        of_expert = lambda per_e: jnp.sum(jnp.where(mine, per_e[None, :], 0), axis=1)
        of_tile = lambda per_e: jnp.sum(jnp.where(mine[:, None, :], per_e[None], 0), axis=2)
        rank0 = tile_start - of_expert(off)
        rank1 = jnp.minimum(rank0 + tile, of_expert(total))
        used = rank0 < rank1
        clo = jnp.sum((of_tile(run) <= rank0[:, None]).astype(i32), axis=1)
        chi = jnp.sum((of_tile(run_prev) < rank1[:, None]).astype(i32), axis=1) - 1
        clo = jnp.where(used, jnp.minimum(clo, nt - 1), 0).astype(i32)
        chi = jnp.where(used, chi, -1).astype(i32)
        return texp, used.astype(i32), clo, chi

    _, used_g, clo, chi = tile_tables(TG)
    texp, used, _, _ = tile_tables(TS)
    xs, gate_col = moe_gather(used_g, clo, chi, pos_rows, gate_rows, h_b)
    ys = moe_ffn(texp, used, xs, gate_col, w_in, w_out, layer)
    lo = off[None, :] + run_prev
    hi = off[None, :] + run
    start = jnp.minimum(lo // 16 * 16, N_SLOTS - MAX_PIECES * PIECE)
    npc = jnp.where(hi > lo, (hi - start + PIECE - 1) // PIECE, 0)
    flat = lambda a: a.reshape(nt * N_EXPERTS).astype(i32)
    return moe_combine(flat(start), flat(npc), flat(lo), flat(hi), off.astype(i32), x, mod, route, ys)


def _final_norm_kernel(x_ref, g_ref, o_ref):
    o_ref[...] = _rms(x_ref[...], g_ref[...])


def final_norm(x, g, row0, n_rows):
    base = row0 // TM
    return pl.pallas_call(
        _final_norm_kernel,
        grid=(n_rows // TM,),
        in_specs=[pl.BlockSpec((TM, D), lambda i: (base + i, 0)), pl.BlockSpec((1, D), lambda i: (0, 0))],
        out_specs=pl.BlockSpec((TM, D), lambda i: (i, 0)),
        out_shape=jax.ShapeDtypeStruct((n_rows, D), F32),
        compiler_params=_cparams("parallel"),
        name="final_norm",
    )(x, g)


def _rope_tables():
    half = 16
    freqs = ROPE_THETA ** (-jnp.arange(half, dtype=F32) / half)
    t = jnp.arange(DEC_SEQ, dtype=jnp.int32)
    row = (t // GRID_W).astype(F32)[:, None] * freqs[None, :]
    col = (t % GRID_W).astype(F32)[:, None] * freqs[None, :]
    cos = jnp.concatenate([jnp.cos(row), jnp.cos(row), jnp.cos(col), jnp.cos(col)], axis=1)
    sin = jnp.concatenate([-jnp.sin(row), jnp.sin(row), -jnp.sin(col), jnp.sin(col)], axis=1)
    cos = jnp.concatenate([jnp.ones((TM, 64), F32), cos], axis=0)
    sin = jnp.concatenate([jnp.zeros((TM, 64), F32), sin], axis=0)
    return jnp.tile(cos, (1, 2)), jnp.tile(sin, (1, 2))


def _mla_weights(w_dq, w_uq, w_dkv, w_ukv, w_o):
    w1 = jnp.concatenate([w_dq, w_dkv, jnp.zeros((D, LANES - MLA_D_ROPE), F32)], axis=1).astype(BF16)
    uq = w_uq.reshape(MLA_Q_RANK, MLA_HEADS, MLA_DK)
    wuq = jnp.concatenate([uq[:, :, :MLA_D_NOPE].reshape(MLA_Q_RANK, -1),
                           uq[:, :, MLA_D_NOPE:].reshape(MLA_Q_RANK, -1)], axis=1).astype(BF16)
    ukv = w_ukv.reshape(MLA_KV_RANK, MLA_HEADS, MLA_D_NOPE + MLA_D_V)
    wukv = jnp.concatenate([ukv[:, :, :MLA_D_NOPE].reshape(MLA_KV_RANK, -1),
                            ukv[:, :, MLA_D_NOPE:].reshape(MLA_KV_RANK, -1)], axis=1).astype(BF16)
    return w1, wuq, wukv, w_o.astype(BF16)


def _s5_weights(a_re, a_im, log_dt, b_re, b_im, c_re, c_im, seg_len):
    dt = jnp.exp(log_dt)[..., None]
    mag = jnp.exp(a_re * dt)
    abar_re, abar_im = mag * jnp.cos(a_im * dt), mag * jnp.sin(a_im * dt)
    mag_n = jnp.exp(a_re * dt * seg_len)
    apow_re, apow_im = mag_n * jnp.cos(a_im * dt * seg_len), mag_n * jnp.sin(a_im * dt * seg_len)
    den = a_re * a_re + a_im * a_im
    coef_re = ((abar_re - 1.0) * a_re + abar_im * a_im) / den
    coef_im = (abar_im * a_re - (abar_re - 1.0) * a_im) / den
    bbar_re = coef_re[..., None] * b_re - coef_im[..., None] * b_im
    bbar_im = coef_re[..., None] * b_im + coef_im[..., None] * b_re
    eye = jnp.eye(S5_GB, dtype=F32)

    def in_block(m):
        m = m.reshape(2, S5_NGB, S5_GB, S5_STATE, S5_GROUP)
        return jnp.einsum('dbgpc,gh->dbgchp', m, eye).reshape(2, S5_NGB, LANES, S5_HALF)

    def out_block(m):
        m = m.reshape(2, S5_NGB, S5_GB, S5_GROUP, S5_STATE)
        return jnp.einsum('dbgcp,gh->dbgphc', m, eye).reshape(2, S5_NGB, S5_HALF, LANES)

    wb = jnp.concatenate([in_block(bbar_re), in_block(bbar_im)], axis=3).astype(BF16)
    wc = jnp.concatenate([out_block(c_re), out_block(-c_im)], axis=2).astype(BF16)
    lanes = lambda m: m.reshape(2, S5_NGB, 1, S5_HALF)
    a = jnp.concatenate([lanes(abar_re), lanes(abar_im)], axis=2)
    an = jnp.concatenate([lanes(apow_re), lanes(apow_im)], axis=2)
    return wb, wc, a, an


def kernel(x_prompt, x_sample, c, c_ctx, cache_mla_ckv, cache_mla_krope, state_s5_re, state_s5_im, cache_diff_k, cache_diff_v, ada_w, ada_b, norm_mix, norm_ffn, norm_final, mla_w_dq, mla_q_norm, mla_w_uq, mla_w_dkv, mla_kv_norm, mla_w_ukv, mla_w_o, s5_a_re, s5_a_im, s5_log_dt, s5_b_re, s5_b_im, s5_c_re, s5_c_im, s5_d, s5_w_glu, diff_w_qkv, diff_lq1, diff_lk1, diff_lq2, diff_lk2, diff_subln, diff_w_o, ffn_w_in, ffn_w_out, moe_w_router, moe_b_router, moe_w_in, moe_w_out):
    x = jnp.concatenate([x_prompt.reshape(N_P, D), x_sample.reshape(N_S, D)], axis=0)
    cond8 = jnp.concatenate([c_ctx[None], c, jnp.zeros((8 - N_GROUPS, D), F32)], axis=0)
    mods = ada_all(cond8, ada_w, ada_b).reshape(DEPTH, 8, 6, D)[:, :N_GROUPS]
    cos_t, sin_t = _rope_tables()

    ffn_w_in_b, ffn_w_out_b = ffn_w_in.astype(BF16), ffn_w_out.astype(BF16)
    moe_w_in_b, moe_w_out_b = moe_w_in.astype(BF16), moe_w_out.astype(BF16)
    new_ckv, new_kr, new_s5_re, new_s5_im, new_dk, new_dv = [], [], [], [], [], []
    for i in range(DEPTH):
        mod = mods[i]
        gmix = norm_mix[i].reshape(1, D)
        gffn = norm_ffn[i].reshape(1, D)
        j = i // 3
        kind = i % 3
        if kind == 0:
            w1, wuq, wukv, wo = _mla_weights(mla_w_dq[j], mla_w_uq[j], mla_w_dkv[j], mla_w_ukv[j], mla_w_o[j])
            q3, ckv, kr = mla_tokens(x, gmix, mod, w1, mla_q_norm[j].reshape(1, -1), wuq,
                                     mla_kv_norm[j].reshape(1, -1), cos_t, sin_t)
            ckv_p = ckv[:N_P].reshape(BATCH, SEQ, MLA_KV_RANK)
            kr_p = kr[:N_P].reshape(BATCH, SEQ, MLA_D_ROPE)
            new_ckv.append(ckv_p)
            new_kr.append(kr_p)
            ckv_s = jnp.concatenate([cache_mla_ckv[:, j], ckv[N_P:].reshape(DEC_BATCH, DEC_SEQ, -1)], axis=1)
            kr_s = jnp.concatenate([cache_mla_krope[:, j], kr[N_P:].reshape(DEC_BATCH, DEC_SEQ, -1)], axis=1)
            k3p, v3p = mla_kv(ckv_p, kr_p, wukv, SEQ)
            k3s, v3s = mla_kv(ckv_s, kr_s, wukv, 512)
            o_p = mla_attention(q3, k3p, v3p, 0, SEQ, SEQ, MLA_HEADS)
            o_s = mla_attention(q3, k3s, v3s, N_P, DEC_SEQ, 256, 2)
            x = proj_residual(o_p, o_s, wo, x, mod)
        elif kind == 1:
            h = normmod_time_major(x, gmix, mod).reshape(N_TOK, D)
            dsk = s5_d[j].reshape(1, D)
            seg = S5_SEG
            wb, wc, a, an = _s5_weights(s5_a_re[j], s5_a_im[j], s5_log_dt[j], s5_b_re[j], s5_b_im[j],
                                        s5_c_re[j], s5_c_im[j], seg)
            zero_h0 = jnp.zeros((BATCH // S5_SUB, 2, S5_NGB, 1, 2 * S5_HALF), F32)
            y_p, fin = s5_scan(h, wb, wc, a, an, dsk, zero_h0, 0, BATCH // S5_SUB, SEQ, False)
            fin = fin.reshape(BATCH // S5_SUB, 2, S5_NGB, S5_SUB, 2, S5_GB, S5_STATE)
            fin = jnp.transpose(fin, (0, 3, 1, 4, 2, 5, 6)).reshape(BATCH, 2, 2, S5_GROUPS, S5_STATE)
            new_s5_re.append(fin[:, :, 0])
            new_s5_im.append(fin[:, :, 1])
            h0 = jnp.stack([state_s5_re[:, j], state_s5_im[:, j]], axis=2)
            h0 = h0.reshape(DEC_BATCH, 2, 2, S5_NGB, S5_HALF)
            h0 = jnp.transpose(h0, (0, 1, 3, 2, 4)).reshape(DEC_BATCH, 2, S5_NGB, 1, 2 * S5_HALF)
            y_s, _ = s5_scan(h, wb, wc, a, an, dsk, h0, N_P, DEC_BATCH, seg, True)
            x = glu_residual(y_p, y_s, s5_w_glu[j].astype(BF16), x, mod)
        else:
            lam_init = 0.8 - 0.6 * math.exp(-0.3 * i)
            wqkv = diff_w_qkv[j].astype(BF16)
            q_p, k_p, v_p, k, v = diff_tokens(x, gmix, mod, wqkv, cos_t, sin_t, latent=False)
            q_s, k_s, v_s = diff_tokens(x, gmix, mod, wqkv, cos_t, sin_t, latent=True)
            new_dk.append(k.reshape(BATCH, SEQ, 2 * DIFF_HEADS, DIFF_DH))
            new_dv.append(v.reshape(BATCH, SEQ, DIFF_HEADS, 2 * DIFF_DH))
            lvecs = [a_.reshape(1, DIFF_DH) for a_ in (diff_lq1[j], diff_lk1[j], diff_lq2[j], diff_lk2[j])]
            subln = diff_subln[j].reshape(1, 2 * DIFF_DH)
            ctx_kv = [(k_p.reshape(BATCH, SEQ, D), v_p.reshape(BATCH, SEQ, D))]
            lat_kv = [(cache_diff_k[:, j].reshape(DEC_BATCH, PAST, D).astype(BF16),
                       cache_diff_v[:, j].reshape(DEC_BATCH, PAST, D).astype(BF16)),
                      (k_s.reshape(DEC_BATCH, DEC_SEQ, D), v_s.reshape(DEC_BATCH, DEC_SEQ, D))]
            o_p = diff_attention(lam_init, lvecs, subln, q_p, ctx_kv, SEQ, SEQ, DIFF_HEADS)
            o_s = diff_attention(lam_init, lvecs, subln, q_s, lat_kv, DEC_SEQ, 256, 1)
            x = proj_residual(o_p, o_s, diff_w_o[j].astype(BF16), x, mod)
        f = i // 2
        if i % 2 == 0:
            x = ffn_residual(x, gffn, mod, ffn_w_in_b, ffn_w_out_b, f)
        else:
            wr = jnp.concatenate([moe_w_router[f], jnp.zeros((D, LANES - N_EXPERTS), F32)], axis=1).astype(BF16)
            br = jnp.concatenate([moe_b_router[f], jnp.zeros((LANES - N_EXPERTS,), F32)]).reshape(1, LANES)
            x = moe_residual(x, gffn, mod, wr, br, moe_w_in_b, moe_w_out_b, f)
    y_p = final_norm(x, norm_final.reshape(1, D), 0, N_P)
    y_s = final_norm(x, norm_final.reshape(1, D), N_P, N_S)
    return (y_p.reshape(BATCH, SEQ, D), y_s.reshape(DEC_BATCH, DEC_SEQ, D),
            jnp.stack(new_ckv, axis=1), jnp.stack(new_kr, axis=1),
            jnp.stack(new_s5_re, axis=1), jnp.stack(new_s5_im, axis=1),
            jnp.stack(new_dk, axis=1), jnp.stack(new_dv, axis=1))
```

```python
import functools
import math

import jax
import jax.numpy as jnp
from jax import lax
from jax.experimental import pallas as pl
from jax.experimental.pallas import tpu as pltpu

D = 1024
BATCH = 16
SEQ = 256
DEPTH = 4
DEC_BATCH = 2
DEC_SEQ = 4096
PAST = 512
GRID_W = 64
N_P = BATCH * SEQ
N_S = DEC_BATCH * DEC_SEQ
N_TOK = N_P + N_S
N_GROUPS = 1 + DEC_BATCH

MLA_HEADS = 8
MLA_Q_RANK = 384
MLA_KV_RANK = 256
MLA_D_NOPE = 128
MLA_D_ROPE = 64
MLA_D_V = 128
MLA_DK = MLA_D_NOPE + MLA_D_ROPE

S5_GROUP = 16
S5_GROUPS = D // S5_GROUP
S5_STATE = 64
S5_GB = 8
S5_NGB = S5_GROUPS // S5_GB
S5_HALF = S5_GB * S5_STATE
S5_NCH = S5_HALF // 128
S5_SUB = 8

DIFF_HEADS = 8
DIFF_DH = D // (2 * DIFF_HEADS)

D_FF = 2816
N_EXPERTS = 8
ROPE_THETA = 10000.0
EPS = 1e-6

LOG2E = math.log2(math.e)
TM = 512
FF_TILE = D_FF
LANES = 128
VMEM_LIMIT = 56 * 1024 * 1024

F32 = jnp.float32
BF16 = jnp.bfloat16


def _cparams(*sem):
    return pltpu.CompilerParams(dimension_semantics=sem, vmem_limit_bytes=VMEM_LIMIT)


def _group_of_tile(i, tm):
    n_p = N_P // tm
    per = DEC_SEQ // tm
    return jnp.where(i < n_p, 0, 1 + (i - n_p) // per)


def _rope_tile(i, tm):
    n_p = N_P // tm
    per = DEC_SEQ // tm
    return jnp.where(i < n_p, 0, 1 + (i - n_p) % per)


def _rms(x, g):
    return x * lax.rsqrt(jnp.mean(x * x, axis=-1, keepdims=True) + EPS) * g


def _normmod(x, g, mod, k_shift, k_scale):
    return _rms(x, g) * (1.0 + mod[k_scale:k_scale + 1, :]) + mod[k_shift:k_shift + 1, :]


def _sigmoid(x):
    return 1.0 / (1.0 + jnp.exp(-x))


def _dot(a, b):
    return jnp.dot(a, b, preferred_element_type=F32)


def _dot_nt(a, b):
    return lax.dot_general(a, b, (((1,), (1,)), ((), ())), preferred_element_type=F32)


def _after(x, row_stat):
    always = (row_stat == row_stat) | (row_stat != row_stat)
    return jnp.where(always, x, jnp.zeros_like(x))


def _rope(x, cos, sin):
    lane = lax.broadcasted_iota(jnp.int32, x.shape, 1)
    nxt = pltpu.roll(x, LANES - 16, 1)
    prv = pltpu.roll(x, 16, 1)
    swapped = jnp.where((lane % 32) < 16, nxt, prv)
    return x * cos + swapped * sin


def _ada_kernel(c_ref, w_ref, b_ref, o_ref):
    c = c_ref[...]
    s = (c * _sigmoid(c)).astype(BF16)
    o_ref[...] = _dot(s, w_ref[...].astype(BF16)) + b_ref[...]


def ada_all(cond8, ada_w, ada_b):
    tn = 1536
    return pl.pallas_call(
        _ada_kernel,
        grid=(DEPTH, 6 * D // tn),
        in_specs=[pl.BlockSpec((8, D), lambda l, n: (0, 0)),
                  pl.BlockSpec((None, D, tn), lambda l, n: (l, 0, n)),
                  pl.BlockSpec((None, 1, tn), lambda l, n: (l, 0, n))],
        out_specs=pl.BlockSpec((None, 8, tn), lambda l, n: (l, 0, n)),
        out_shape=jax.ShapeDtypeStruct((DEPTH, 8, 6 * D), F32),
        compiler_params=_cparams("parallel", "parallel"),
        name="ada",
    )(cond8, ada_w, ada_b.reshape(DEPTH, 1, 6 * D))


def _mod_spec(tm):
    return pl.BlockSpec((None, 6, D), lambda i, *_: (_group_of_tile(i, tm), 0, 0))


def _mla_tok_kernel(x_ref, g_ref, mod_ref, w1_ref, qn_ref, wuq_ref, kvn_ref, cos_ref, sin_ref,
                    q_ref, ckv_ref, kr_ref):
    h = _normmod(x_ref[...], g_ref[...], mod_ref[...], 0, 1).astype(BF16)
    t1 = _dot(h, w1_ref[...])
    ql = _rms(t1[:, :MLA_Q_RANK], qn_ref[...]).astype(BF16)
    q = _dot(ql, wuq_ref[...]) * (MLA_DK ** -0.5 * LOG2E)
    c0 = MLA_Q_RANK
    ckv_ref[...] = _rms(t1[:, c0:c0 + MLA_KV_RANK], kvn_ref[...])
    cos = cos_ref[...]
    sin = sin_ref[...]
    kr = _rope(t1[:, c0 + MLA_KV_RANK:c0 + MLA_KV_RANK + LANES], cos, sin)
    kr_ref[...] = kr[:, :MLA_D_ROPE]
    n_nope = MLA_HEADS * MLA_D_NOPE
    for pair in range(MLA_HEADS // 2):
        qr = _rope(q[:, n_nope + pair * LANES:n_nope + (pair + 1) * LANES], cos, sin).astype(BF16)
        for sub in range(2):
            hd = 2 * pair + sub
            q_ref[hd, :, 0:MLA_D_NOPE] = q[:, hd * MLA_D_NOPE:(hd + 1) * MLA_D_NOPE].astype(BF16)
            q_ref[hd, :, MLA_D_NOPE:MLA_DK] = qr[:, sub * MLA_D_ROPE:(sub + 1) * MLA_D_ROPE]


def mla_tokens(x, g, mod, w1, qn, wuq, kvn, cos_t, sin_t):
    nt = N_TOK // TM
    const = lambda shape: pl.BlockSpec(shape, lambda i: (0,) * len(shape))
    return pl.pallas_call(
        _mla_tok_kernel,
        grid=(nt,),
        in_specs=[pl.BlockSpec((TM, D), lambda i: (i, 0)), const((1, D)), _mod_spec(TM),
                  const(w1.shape), const((1, MLA_Q_RANK)), const(wuq.shape), const((1, MLA_KV_RANK)),
                  pl.BlockSpec((TM, LANES), lambda i: (_rope_tile(i, TM), 0)),
                  pl.BlockSpec((TM, LANES), lambda i: (_rope_tile(i, TM), 0))],
        out_specs=[pl.BlockSpec((MLA_HEADS, TM, MLA_DK), lambda i: (0, i, 0)),
                   pl.BlockSpec((TM, MLA_KV_RANK), lambda i: (i, 0)),
                   pl.BlockSpec((TM, MLA_D_ROPE), lambda i: (i, 0))],
        out_shape=[jax.ShapeDtypeStruct((MLA_HEADS, N_TOK, MLA_DK), BF16),
                   jax.ShapeDtypeStruct((N_TOK, MLA_KV_RANK), F32),
                   jax.ShapeDtypeStruct((N_TOK, MLA_D_ROPE), F32)],
        compiler_params=_cparams("parallel"),
        name="mla_tokens",
    )(x, g, mod, w1, qn, wuq, kvn, cos_t, sin_t)


def _mla_kv_kernel(ckv_ref, kr_ref, w_ref, k_ref, v_ref):
    kv = _dot(ckv_ref[...].astype(BF16), w_ref[...])
    kr = kr_ref[...].astype(BF16)
    n_nope = MLA_HEADS * MLA_D_NOPE
    for hd in range(MLA_HEADS):
        k_ref[hd, :, 0:MLA_D_NOPE] = kv[:, hd * MLA_D_NOPE:(hd + 1) * MLA_D_NOPE].astype(BF16)
        k_ref[hd, :, MLA_D_NOPE:MLA_DK] = kr
        v_ref[hd] = kv[:, n_nope + hd * MLA_D_V:n_nope + (hd + 1) * MLA_D_V].astype(BF16)


def mla_kv(ckv, kr, wukv, ts):
    nb, s, _ = ckv.shape
    return pl.pallas_call(
        _mla_kv_kernel,
        grid=(nb, s // ts),
        in_specs=[pl.BlockSpec((None, ts, MLA_KV_RANK), lambda b, t: (b, t, 0)),
                  pl.BlockSpec((None, ts, MLA_D_ROPE), lambda b, t: (b, t, 0)),
                  pl.BlockSpec(wukv.shape, lambda b, t: (0, 0))],
        out_specs=[pl.BlockSpec((None, MLA_HEADS, ts, MLA_DK), lambda b, t: (b, 0, t, 0)),
                   pl.BlockSpec((None, MLA_HEADS, ts, MLA_D_V), lambda b, t: (b, 0, t, 0))],
        out_shape=[jax.ShapeDtypeStruct((nb, MLA_HEADS, s, MLA_DK), BF16),
                   jax.ShapeDtypeStruct((nb, MLA_HEADS, s, MLA_D_V), BF16)],
        compiler_params=_cparams("parallel", "parallel"),
        name="mla_kv",
    )(ckv, kr, wukv)


def _mla_attn_kernel(hps, q_ref, k_ref, v_ref, o_ref):
    for hd in range(hps):
        s = _dot_nt(q_ref[hd], k_ref[hd])
        p = jnp.exp2(s - jnp.max(s, axis=-1, keepdims=True))
        l = jnp.sum(p, axis=-1, keepdims=True)
        o = _dot(p.astype(BF16), v_ref[hd]) / l
        o_ref[:, hd * MLA_D_V:(hd + 1) * MLA_D_V] = o.astype(BF16)


def mla_attention(q3, k3, v3, row0, seq, tq, hps):
    nb, _, s, _ = k3.shape
    nq = seq // tq
    base = row0 // tq
    return pl.pallas_call(
        functools.partial(_mla_attn_kernel, hps),
        grid=(nb, MLA_HEADS // hps, nq),
        in_specs=[pl.BlockSpec((hps, tq, MLA_DK), lambda b, h, i: (h, base + b * nq + i, 0)),
                  pl.BlockSpec((None, hps, s, MLA_DK), lambda b, h, i: (b, h, 0, 0)),
                  pl.BlockSpec((None, hps, s, MLA_D_V), lambda b, h, i: (b, h, 0, 0))],
        out_specs=pl.BlockSpec((tq, hps * MLA_D_V), lambda b, h, i: (b * nq + i, h)),
        out_shape=jax.ShapeDtypeStruct((nb * seq, MLA_HEADS * MLA_D_V), BF16),
        compiler_params=_cparams("parallel", "parallel", "parallel"),
        name="mla_attn_%d" % s,
    )(q3, k3, v3)


def _proj_res_kernel(op_ref, os_ref, w_ref, x_ref, mod_ref, out_ref):
    o = jnp.where(pl.program_id(0) < N_P // TM, op_ref[...], os_ref[...])
    out_ref[...] = x_ref[...] + mod_ref[2:3, :] * _dot(o, w_ref[...])


def proj_residual(o_p, o_s, w, x, mod):
    nt = N_TOK // TM
    n_p = N_P // TM
    return pl.pallas_call(
        _proj_res_kernel,
        grid=(nt,),
        in_specs=[pl.BlockSpec((TM, D), lambda i: (jnp.minimum(i, n_p - 1), 0)),
                  pl.BlockSpec((TM, D), lambda i: (jnp.maximum(i - n_p, 0), 0)),
                  pl.BlockSpec((D, D), lambda i: (0, 0)),
                  pl.BlockSpec((TM, D), lambda i: (i, 0)), _mod_spec(TM)],
        out_specs=pl.BlockSpec((TM, D), lambda i: (i, 0)),
        out_shape=jax.ShapeDtypeStruct((N_TOK, D), F32),
        compiler_params=_cparams("parallel"),
        name="proj_residual",
    )(o_p, o_s, w, x, mod)


def _diff_tok_kernel(latent, x_ref, g_ref, mod_ref, w_ref, *refs):
    if latent:
        cos_ref, sin_ref, q_ref, kc_ref, vc_ref = refs
        cos = cos_ref[...]
        sin = sin_ref[...]
        rot = lambda t: _rope(t, cos, sin)
    else:
        q_ref, kc_ref, vc_ref, k_ref, v_ref = refs
        rot = lambda t: t
    h = _normmod(x_ref[...], g_ref[...], mod_ref[...], 0, 1).astype(BF16)
    wide = 2 * LANES
    rot2 = lambda t: jnp.concatenate([rot(t[:, :LANES]), rot(t[:, LANES:])], axis=1)
    for c in range(D // wide):
        sl = slice(c * wide, (c + 1) * wide)
        q = _dot(h, w_ref[:, c * wide:(c + 1) * wide]) * (DIFF_DH ** -0.5 * LOG2E)
        k = _dot(h, w_ref[:, D + c * wide:D + (c + 1) * wide])
        v = _dot(h, w_ref[:, 2 * D + c * wide:2 * D + (c + 1) * wide])
        q_ref[:, sl] = rot2(q).astype(BF16)
        kc_ref[:, sl] = rot2(k).astype(BF16)
        vc_ref[:, sl] = v.astype(BF16)
        if not latent:
            k_ref[:, sl] = k
            v_ref[:, sl] = v


def diff_tokens(x, g, mod, wqkv, cos_t, sin_t, latent):
    n_rows = N_S if latent else N_P
    base = (N_P if latent else 0) // TM
    row = lambda i: (i, 0)
    in_specs = [pl.BlockSpec((TM, D), lambda i: (base + i, 0)), pl.BlockSpec((1, D), lambda i: (0, 0)),
                pl.BlockSpec((None, 6, D), lambda i: (_group_of_tile(base + i, TM), 0, 0)),
                pl.BlockSpec((D, 3 * D), lambda i: (0, 0))]
    args = [x, g, mod, wqkv]
    out_shape = [jax.ShapeDtypeStruct((n_rows, D), BF16)] * 3
    if latent:
        rope_spec = pl.BlockSpec((TM, LANES), lambda i: (_rope_tile(base + i, TM), 0))
        in_specs += [rope_spec, rope_spec]
        args += [cos_t, sin_t]
    else:
        out_shape += [jax.ShapeDtypeStruct((n_rows, D), F32)] * 2
    return pl.pallas_call(
        functools.partial(_diff_tok_kernel, latent),
        grid=(n_rows // TM,),
        in_specs=in_specs,
        out_specs=[pl.BlockSpec((TM, D), row)] * len(out_shape),
        out_shape=out_shape,
        compiler_params=_cparams("parallel"),
        name="diff_tokens_lat" if latent else "diff_tokens_ctx",
    )(*args)


def _diff_attn_kernel(lam_init, n_kv, pairs, stagger, lq1_ref, lk1_ref, lq2_ref, lk2_ref, sub_ref, q_ref, *refs):
    kv_refs = refs[:2 * n_kv]
    o_ref = refs[2 * n_kv]
    lam = (jnp.exp(jnp.sum(lq1_ref[...] * lk1_ref[...], axis=-1, keepdims=True))
           - jnp.exp(jnp.sum(lq2_ref[...] * lk2_ref[...], axis=-1, keepdims=True)) + lam_init)
    for pr in range(pairs):
        lanes = slice(pr * LANES, (pr + 1) * LANES)
        q = q_ref[:, lanes]
        lane = lax.broadcasted_iota(jnp.int32, q.shape, 1)
        zero = jnp.zeros_like(q)

        def probs(qh):
            s = jnp.concatenate([_dot_nt(qh, kv_refs[2 * i][:, lanes]) for i in range(n_kv)], axis=1)
            m = jnp.max(s, axis=-1, keepdims=True)
            p = jnp.exp2(s - m)
            return p, jnp.sum(p, axis=-1, keepdims=True), m

        p1, l1, m1 = probs(jnp.where(lane < DIFF_DH, q, zero))
        q2 = jnp.where(lane >= DIFF_DH, q, zero)
        p2, l2, _ = probs(_after(q2, m1) if stagger else q2)
        att = (p1 + (-lam * l1 / l2) * p2).astype(BF16)
        acc = jnp.zeros(q.shape, F32)
        col = 0
        for i in range(n_kv):
            n = kv_refs[2 * i + 1].shape[0]
            acc = acc + _dot(att[:, col:col + n], kv_refs[2 * i + 1][:, lanes])
            col += n
        o = acc / l1
        o_ref[:, lanes] = (_rms(o, sub_ref[...]) * (1.0 - lam_init)).astype(BF16)


def diff_attention(lam_init, lvecs, subln, q, kvs, seq, tq, pairs):
    nb = kvs[0][0].shape[0]
    nq = seq // tq
    n_keys = sum(k.shape[1] for k, _ in kvs)
    width = pairs * LANES
    vec = pl.BlockSpec((1, DIFF_DH), lambda b, h, i: (0, 0))
    kv_specs, kv_args = [], []
    for k, v in kvs:
        spec = pl.BlockSpec((None, k.shape[1], width), lambda b, h, i: (b, 0, h))
        kv_specs += [spec, spec]
        kv_args += [k, v]
    return pl.pallas_call(
        functools.partial(_diff_attn_kernel, lam_init, len(kvs), pairs, n_keys > SEQ),
        grid=(nb, DIFF_HEADS // pairs, nq),
        in_specs=[vec, vec, vec, vec, pl.BlockSpec((1, 2 * DIFF_DH), lambda b, h, i: (0, 0)),
                  pl.BlockSpec((tq, width), lambda b, h, i: (b * nq + i, h))] + kv_specs,
        out_specs=pl.BlockSpec((tq, width), lambda b, h, i: (b * nq + i, h)),
        out_shape=jax.ShapeDtypeStruct((nb * seq, D), BF16),
        compiler_params=_cparams("parallel", "parallel", "parallel"),
        name="diff_attn_%d" % n_keys,
    )(*lvecs, subln, q, *kv_args)


def _normmod_kernel(x_ref, g_ref, mod_ref, h_ref):
    h_ref[...] = _normmod(x_ref[...], g_ref[...], mod_ref[...], 0, 1)


S5_T = SEQ
S5_SEG = DEC_SEQ // S5_SUB
S5_TILES_P = N_P // S5_T
S5_PER_SEG = S5_SEG // S5_T


def _s5_tile_pos(i):
    k = i - S5_TILES_P
    per_batch = S5_SUB * S5_PER_SEG
    lat_row = S5_TILES_P // S5_SUB + (k // per_batch) * S5_PER_SEG + k % S5_PER_SEG
    lat_col = (k % per_batch) // S5_PER_SEG
    is_p = i < S5_TILES_P
    return jnp.where(is_p, i // S5_SUB, lat_row), jnp.where(is_p, i % S5_SUB, lat_col)


def normmod_time_major(x, g, mod):
    nt = N_TOK // S5_T
    return pl.pallas_call(
        _normmod_kernel,
        grid=(nt,),
        in_specs=[pl.BlockSpec((S5_T, D), lambda i: (i, 0)), pl.BlockSpec((1, D), lambda i: (0, 0)),
                  _mod_spec(S5_T)],
        out_specs=pl.BlockSpec((S5_T, D), lambda i: _s5_tile_pos(i)),
        out_shape=jax.ShapeDtypeStruct((N_TOK // S5_SUB, S5_SUB * D), F32),
        compiler_params=_cparams("parallel"),
        name="normmod",
    )(x, g, mod)


def _s5_kernel(chained, n, u_ref, wb_ref, wc_ref, a_ref, an_ref, dsk_ref, h0_ref, y_ref, fin_ref,
               bu_ref, ini_ref):
    d = pl.program_id(2)
    rows = S5_SUB * n
    chunk = 512
    nch = S5_NCH
    for r in range(rows // chunk):
        rs = slice(r * chunk, (r + 1) * chunk)
        bu = _dot(u_ref[rs, :].astype(BF16), wb_ref[...])
        for c in range(2 * nch):
            bu_ref[c, rs, :] = bu[:, c * LANES:(c + 1) * LANES]
    ar = [jnp.broadcast_to(a_ref[0:1, c * LANES:(c + 1) * LANES], (S5_SUB, LANES)) for c in range(nch)]
    ai = [jnp.broadcast_to(a_ref[1:2, c * LANES:(c + 1) * LANES], (S5_SUB, LANES)) for c in range(nch)]

    def step_index(s):
        return jnp.where(d == 0, s, n - 1 - s)

    def step_rows(t):
        return pl.ds(pl.multiple_of(t * S5_SUB, S5_SUB), S5_SUB)

    def advance(h, t):
        out = [None] * (2 * nch)
        for c in range(nch):
            br = bu_ref[c, step_rows(t), :]
            bi = bu_ref[nch + c, step_rows(t), :]
            out[c] = ar[c] * h[c] - ai[c] * h[nch + c] + br
            out[nch + c] = ar[c] * h[nch + c] + ai[c] * h[c] + bi
        return out

    unroll = 4
    zeros = [jnp.zeros((S5_SUB, LANES), F32) for _ in range(2 * nch)]

    if chained:
        def local_body(s, h):
            h = list(h)
            for k in range(unroll):
                h = advance(h, step_index(s * unroll + k))
            return tuple(h)

        ends = lax.fori_loop(0, n // unroll, local_body, tuple(zeros))
        sub_row = lax.broadcasted_iota(jnp.int32, (S5_SUB, LANES), 0)
        cur = [h0_ref[:, c * LANES:(c + 1) * LANES] for c in range(2 * nch)]
        anr = [an_ref[0:1, c * LANES:(c + 1) * LANES] for c in range(nch)]
        ani = [an_ref[1:2, c * LANES:(c + 1) * LANES] for c in range(nch)]
        for kk in range(S5_SUB):
            j = jnp.where(d == 0, kk, S5_SUB - 1 - kk)
            nxt = [None] * (2 * nch)
            for c in range(2 * nch):
                ini_ref[c, pl.ds(j, 1), :] = cur[c]
            for c in range(nch):
                er = jnp.sum(jnp.where(sub_row == j, ends[c], 0.0), axis=0, keepdims=True)
                ei = jnp.sum(jnp.where(sub_row == j, ends[nch + c], 0.0), axis=0, keepdims=True)
                nxt[c] = er + anr[c] * cur[c] - ani[c] * cur[nch + c]
                nxt[nch + c] = ei + anr[c] * cur[nch + c] + ani[c] * cur[c]
            cur = nxt
        h_init = [ini_ref[c] for c in range(2 * nch)]
    else:
        h_init = zeros

    def body(s, h):
        h = list(h)
        for k in range(unroll):
            t = step_index(s * unroll + k)
            h = advance(h, t)
            for c in range(2 * nch):
                bu_ref[c, step_rows(t), :] = h[c]
        return tuple(h)

    fin = lax.fori_loop(0, n // unroll, body, tuple(h_init))
    for c in range(2 * nch):
        fin_ref[:, c * LANES:(c + 1) * LANES] = fin[c]

    @pl.when(d == 0)
    def _():
        y_ref[...] = dsk_ref[...] * u_ref[...]

    for r in range(rows // chunk):
        rs = slice(r * chunk, (r + 1) * chunk)
        hs = jnp.concatenate([bu_ref[c, rs, :].astype(BF16) for c in range(2 * nch)], axis=1)
        y_ref[rs, :] += _dot(hs, wc_ref[...])


def s5_scan(h, wb, wc, a, an, dskip, h0, row0, n_blocks, n, chained):
    rows = S5_SUB * n
    base = row0 // rows
    kern = functools.partial(_s5_kernel, chained, n)
    return pl.pallas_call(
        kern,
        grid=(n_blocks, S5_NGB, 2),
        in_specs=[pl.BlockSpec((rows, LANES), lambda r, c, d: (base + r, c)),
                  pl.BlockSpec((None, None, LANES, 2 * S5_HALF), lambda r, c, d: (d, c, 0, 0)),
                  pl.BlockSpec((None, None, 2 * S5_HALF, LANES), lambda r, c, d: (d, c, 0, 0)),
                  pl.BlockSpec((None, None, 2, S5_HALF), lambda r, c, d: (d, c, 0, 0)),
                  pl.BlockSpec((None, None, 2, S5_HALF), lambda r, c, d: (d, c, 0, 0)),
                  pl.BlockSpec((1, LANES), lambda r, c, d: (0, c)),
                  pl.BlockSpec((None, None, None, 1, 2 * S5_HALF), lambda r, c, d: (r, d, c, 0, 0))],
        out_specs=[pl.BlockSpec((rows, LANES), lambda r, c, d: (r, c)),
                   pl.BlockSpec((None, None, None, S5_SUB, 2 * S5_HALF), lambda r, c, d: (r, d, c, 0, 0))],
        out_shape=[jax.ShapeDtypeStruct((n_blocks * rows, D), F32),
                   jax.ShapeDtypeStruct((n_blocks, 2, S5_NGB, S5_SUB, 2 * S5_HALF), F32)],
        scratch_shapes=[pltpu.VMEM((2 * S5_NCH, rows, LANES), F32),
                        pltpu.VMEM((2 * S5_NCH, S5_SUB, LANES), F32)],
        compiler_params=_cparams("parallel", "parallel", "arbitrary"),
        name="s5_scan_%d" % n,
    )(h, wb, wc, a, an, dskip, h0)


def _glu_res_kernel(yp_ref, ys_ref, w_ref, x_ref, mod_ref, out_ref):
    y = jnp.where(pl.program_id(0) < S5_TILES_P, yp_ref[...], ys_ref[...])
    g = 0.5 * y * (1.0 + jnp.tanh(math.sqrt(2.0 / math.pi) * (y + 0.044715 * (y * y * y))))
    t = _dot(g.astype(BF16), w_ref[...])
    out_ref[...] = x_ref[...] + mod_ref[2:3, :] * (t[:, :D] * _sigmoid(t[:, D:]))


def glu_residual(y_p, y_s, w, x, mod):
    nt = N_TOK // S5_T
    n_blk_p = S5_TILES_P // S5_SUB

    def yp_map(i):
        r, c = _s5_tile_pos(jnp.minimum(i, S5_TILES_P - 1))
        return r, c

    def ys_map(i):
        r, c = _s5_tile_pos(jnp.maximum(i, S5_TILES_P))
        return r - n_blk_p, c

    return pl.pallas_call(
        _glu_res_kernel,
        grid=(nt,),
        in_specs=[pl.BlockSpec((S5_T, D), yp_map), pl.BlockSpec((S5_T, D), ys_map),
                  pl.BlockSpec((D, 2 * D), lambda i: (0, 0)),
                  pl.BlockSpec((S5_T, D), lambda i: (i, 0)), _mod_spec(S5_T)],
        out_specs=pl.BlockSpec((S5_T, D), lambda i: (i, 0)),
        out_shape=jax.ShapeDtypeStruct((N_TOK, D), F32),
        compiler_params=_cparams("parallel"),
        name="glu_residual",
    )(y_p.reshape(N_P // S5_SUB, S5_SUB * D), y_s.reshape(N_S // S5_SUB, S5_SUB * D), w, x, mod)


FF_CHUNKS = ((0, 768), (768, 1536), (1536, 2304), (2304, FF_TILE))


def _swiglu_partial(h, wa_ref, wb_ref, wo_ref):
    out = None
    for lo, hi in FF_CHUNKS:
        a = _dot(h, wa_ref[:, lo:hi])
        b = _dot(h, wb_ref[:, lo:hi])
        act = (a * _sigmoid(a) * b).astype(BF16)
        part = _dot(act, wo_ref[lo:hi, :])
        out = part if out is None else out + part
    return out


def _ffn_kernel(x_ref, g_ref, mod_ref, wa_ref, wb_ref, wo_ref, out_ref, h_scr, acc_scr):
    f = pl.program_id(1)

    @pl.when(f == 0)
    def _():
        h_scr[...] = _normmod(x_ref[...], g_ref[...], mod_ref[...], 3, 4).astype(BF16)
        acc_scr[...] = jnp.zeros_like(acc_scr)

    acc_scr[...] += _swiglu_partial(h_scr[...], wa_ref, wb_ref, wo_ref)

    @pl.when(f == pl.num_programs(1) - 1)
    def _():
        out_ref[...] = x_ref[...] + mod_ref[5:6, :] * acc_scr[...]


def ffn_residual(x, g, mod, w_in, w_out, layer):
    nt = N_TOK // TM
    nf = D_FF // FF_TILE
    return pl.pallas_call(
        _ffn_kernel,
        grid=(nt, nf),
        in_specs=[pl.BlockSpec((TM, D), lambda i, f: (i, 0)), pl.BlockSpec((1, D), lambda i, f: (0, 0)),
                  _mod_spec(TM),
                  pl.BlockSpec((None, D, FF_TILE), lambda i, f: (layer, 0, f)),
                  pl.BlockSpec((None, D, FF_TILE), lambda i, f: (layer, 0, f + nf)),
                  pl.BlockSpec((None, FF_TILE, D), lambda i, f: (layer, f, 0))],
        out_specs=pl.BlockSpec((TM, D), lambda i, f: (i, 0)),
        out_shape=jax.ShapeDtypeStruct((N_TOK, D), F32),
        scratch_shapes=[pltpu.VMEM((TM, D), BF16), pltpu.VMEM((TM, D), F32)],
        compiler_params=_cparams("parallel", "arbitrary"),
        name="ffn",
    )(x, g, mod, w_in, w_in, w_out)


TS = 512
N_SLOTS = 2 * N_TOK + N_EXPERTS * TS
NT_S = N_SLOTS // TS
TG = 256
PIECE = 256
MAX_PIECES = (TM + 16 + PIECE - 1) // PIECE


def _moe_route_kernel(x_ref, g_ref, mod_ref, wr_ref, br_ref, h_ref, route_ref, rows_ref, run_ref, carry_scr):
    @pl.when(pl.program_id(0) == 0)
    def _():
        carry_scr[...] = jnp.zeros_like(carry_scr)

    h = _normmod(x_ref[...], g_ref[...], mod_ref[...], 3, 4).astype(BF16)
    h_ref[...] = h
    logits = _dot(h, wr_ref[...]) + br_ref[...]
    lane = lax.broadcasted_iota(jnp.int32, logits.shape, 1)
    neg = jnp.float32(-jnp.inf)
    lg = jnp.where(lane < N_EXPERTS, logits, neg)
    v1 = jnp.max(lg, axis=-1, keepdims=True)
    i1 = jnp.min(jnp.where(lg == v1, lane, LANES), axis=-1, keepdims=True)
    lg2 = jnp.where(lane == i1, neg, lg)
    v2 = jnp.max(lg2, axis=-1, keepdims=True)
    i2 = jnp.min(jnp.where(lg2 == v2, lane, LANES), axis=-1, keepdims=True)
    e2 = jnp.exp(v2 - v1)
    g1 = 1.0 / (1.0 + e2)
    g2 = e2 / (1.0 + e2)
    oh1 = lane == i1
    oh2 = lane == i2
    sel = jnp.where(oh1 | oh2, 1.0, 0.0)
    r = lax.broadcasted_iota(jnp.int32, (TM, TM), 0)
    c = lax.broadcasted_iota(jnp.int32, (TM, TM), 1)
    tri = jnp.where(c < r, 1.0, 0.0).astype(BF16)
    rank = _dot(tri, sel.astype(BF16)) + carry_scr[0:1, :]
    r1 = jnp.sum(jnp.where(oh1, rank, 0.0), axis=-1, keepdims=True)
    r2 = jnp.sum(jnp.where(oh2, rank, 0.0), axis=-1, keepdims=True)
    cols = (i1.astype(F32), i2.astype(F32), g1, g2, r1, r2)
    route = jnp.zeros(logits.shape, F32)
    for k, v in enumerate(cols):
        route = jnp.where(lane == k, v, route)
    route_ref[...] = route
    rows_ref[...] = route.T[0:8, :]
    carry_scr[...] = carry_scr[...] + jnp.sum(sel, axis=0, keepdims=True)
    run_ref[...] = carry_scr[...]


def moe_route(x, g, mod, w_router, b_router):
    nt = N_TOK // TM
    return pl.pallas_call(
        _moe_route_kernel,
        grid=(nt,),
        in_specs=[pl.BlockSpec((TM, D), lambda i: (i, 0)), pl.BlockSpec((1, D), lambda i: (0, 0)), _mod_spec(TM),
                  pl.BlockSpec((D, LANES), lambda i: (0, 0)), pl.BlockSpec((1, LANES), lambda i: (0, 0))],
        out_specs=[pl.BlockSpec((TM, D), lambda i: (i, 0)), pl.BlockSpec((TM, LANES), lambda i: (i, 0)),
                   pl.BlockSpec((8, TM), lambda i: (0, i)),
                   pl.BlockSpec((None, 8, LANES), lambda i: (i, 0, 0))],
        out_shape=[jax.ShapeDtypeStruct((N_TOK, D), BF16), jax.ShapeDtypeStruct((N_TOK, LANES), F32),
                   jax.ShapeDtypeStruct((8, N_TOK), F32),
                   jax.ShapeDtypeStruct((nt, 8, LANES), F32)],
        scratch_shapes=[pltpu.VMEM((8, LANES), F32)],
        compiler_params=_cparams("arbitrary"),
        name="moe_route",
    )(x, g, mod, w_router, b_router)


def _moe_gather_kernel(used_ref, clo_ref, chi_ref, pos_ref, gates_ref, h_ref, xs_ref, gate_ref, gat_scr, gsum_scr):
    i = pl.program_id(0)
    gat_scr[...] = jnp.zeros_like(gat_scr)
    gsum_scr[...] = jnp.zeros_like(gsum_scr)

    @pl.when(used_ref[i] > 0)
    def _():
        slot = i * TG + lax.broadcasted_iota(jnp.int32, (TG, TM), 0)

        def body(c, carry):
            base = pl.multiple_of(c * TM, TM)
            m1 = pos_ref[0:1, pl.ds(base, TM)] == slot
            m2 = pos_ref[1:2, pl.ds(base, TM)] == slot
            pick = jnp.where(m1 | m2, 1.0, 0.0).astype(BF16)
            gat_scr[...] += _dot(pick, h_ref[pl.ds(base, TM), :])
            g = jnp.where(m1, gates_ref[0:1, pl.ds(base, TM)], 0.0) + jnp.where(m2, gates_ref[1:2, pl.ds(base, TM)], 0.0)
            gsum_scr[...] += jnp.sum(g, axis=-1, keepdims=True)
            return carry

        lax.fori_loop(clo_ref[i], chi_ref[i] + 1, body, 0)

    xs_ref[...] = gat_scr[...].astype(BF16)
    gate_ref[...] = gsum_scr[...]


def moe_gather(used, clo, chi, pos_rows, gate_rows, h_b):
    whole = lambda shape: pl.BlockSpec(shape, lambda i, *_: (0, 0), pipeline_mode=pl.Buffered(1))
    grid_spec = pltpu.PrefetchScalarGridSpec(
        num_scalar_prefetch=3,
        grid=(N_SLOTS // TG,),
        in_specs=[whole((8, N_TOK)), whole((8, N_TOK)), whole((N_TOK, D))],
        out_specs=[pl.BlockSpec((TG, D), lambda i, *_: (i, 0)), pl.BlockSpec((TG, 1), lambda i, *_: (i, 0))],
        scratch_shapes=[pltpu.VMEM((TG, D), F32), pltpu.VMEM((TG, 1), F32)],
    )
    return pl.pallas_call(
        _moe_gather_kernel,
        grid_spec=grid_spec,
        out_shape=[jax.ShapeDtypeStruct((N_SLOTS, D), BF16), jax.ShapeDtypeStruct((N_SLOTS, 1), F32)],
        compiler_params=_cparams("arbitrary"),
        name="moe_gather",
    )(used, clo, chi, pos_rows, gate_rows, h_b)


def _moe_ffn_kernel(texp_ref, used_ref, xs_ref, gate_ref, wa_ref, wb_ref, wo_ref, ys_ref, acc_scr):
    i = pl.program_id(0)
    f = pl.program_id(1)
    live = used_ref[i] > 0

    @pl.when(f == 0)
    def _():
        acc_scr[...] = jnp.zeros_like(acc_scr)

    @pl.when(live)
    def _():
        acc_scr[...] += _swiglu_partial(xs_ref[...], wa_ref, wb_ref, wo_ref)

    @pl.when(f == pl.num_programs(1) - 1)
    def _():
        ys_ref[...] = (gate_ref[...] * acc_scr[...]).astype(BF16)


def moe_ffn(texp, used, xs, gate_col, w_in, w_out, layer):
    nf = D_FF // FF_TILE
    grid_spec = pltpu.PrefetchScalarGridSpec(
        num_scalar_prefetch=2,
        grid=(NT_S, nf),
        in_specs=[pl.BlockSpec((TS, D), lambda i, f, *_: (i, 0)),
                  pl.BlockSpec((TS, 1), lambda i, f, *_: (i, 0)),
                  pl.BlockSpec((None, None, D, FF_TILE), lambda i, f, texp, *_: (layer, texp[i], 0, f)),
                  pl.BlockSpec((None, None, D, FF_TILE), lambda i, f, texp, *_: (layer, texp[i], 0, f + nf)),
                  pl.BlockSpec((None, None, FF_TILE, D), lambda i, f, texp, *_: (layer, texp[i], f, 0))],
        out_specs=pl.BlockSpec((TS, D), lambda i, f, *_: (i, 0)),
        scratch_shapes=[pltpu.VMEM((TS, D), F32)],
    )
    return pl.pallas_call(
        _moe_ffn_kernel,
        grid_spec=grid_spec,
        out_shape=jax.ShapeDtypeStruct((N_SLOTS, D), BF16),
        compiler_params=_cparams("parallel", "arbitrary"),
        name="moe_ffn",
    )(texp, used, xs, gate_col, w_in, w_in, w_out)


def _moe_combine_kernel(start_ref, npc_ref, lo_ref, hi_ref, off_ref, x_ref, mod_ref, route_ref, ys_hbm, out_ref,
                        buf, acc_scr, sem):
    i = pl.program_id(0)
    n_tiles = pl.num_programs(0)

    def piece_copy(tile, e, k):
        half = tile % 2
        s = pl.multiple_of(start_ref[tile * N_EXPERTS + e] + k * PIECE, 16)
        return pltpu.make_async_copy(ys_hbm.at[pl.ds(s, PIECE), :], buf.at[half, e, pl.ds(k * PIECE, PIECE), :],
                                     sem.at[half, e, k])

    def start_tile(tile):
        for e in range(N_EXPERTS):
            for k in range(MAX_PIECES):
                @pl.when(k < npc_ref[tile * N_EXPERTS + e])
                def _():
                    piece_copy(tile, e, k).start()

    @pl.when(i == 0)
    def _():
        start_tile(i)

    @pl.when(i + 1 < n_tiles)
    def _():
        start_tile(i + 1)

    half = i % 2
    acc_scr[...] = jnp.zeros_like(acc_scr)
    route = route_ref[...]
    e1 = route[:, 0:1].astype(jnp.int32)
    e2 = route[:, 1:2].astype(jnp.int32)
    pos1 = route[:, 4:5].astype(jnp.int32)
    pos2 = route[:, 5:6].astype(jnp.int32)
    for e in range(N_EXPERTS):
        pos1 = pos1 + jnp.where(e1 == e, off_ref[e], 0)
        pos2 = pos2 + jnp.where(e2 == e, off_ref[e], 0)
    lane = lax.broadcasted_iota(jnp.int32, (TM, PIECE), 1)
    for e in range(N_EXPERTS):
        lo = lo_ref[i * N_EXPERTS + e]
        hi = hi_ref[i * N_EXPERTS + e]
        p1 = jnp.where((pos1 >= lo) & (pos1 < hi), pos1, -1)
        p2 = jnp.where((pos2 >= lo) & (pos2 < hi), pos2, -1)
        for k in range(MAX_PIECES):
            @pl.when(k < npc_ref[i * N_EXPERTS + e])
            def _():
                piece_copy(i, e, k).wait()
                base = start_ref[i * N_EXPERTS + e] + k * PIECE
                pick = jnp.where((p1 - base == lane) | (p2 - base == lane), 1.0, 0.0).astype(BF16)
                acc_scr[...] += _dot(pick, buf[half, e, pl.ds(k * PIECE, PIECE), :])

    out_ref[...] = x_ref[...] + mod_ref[5:6, :] * acc_scr[...]


def moe_combine(start, npc, lo, hi, off, x, mod, route, ys):
    nt = N_TOK // TM
    grid_spec = pltpu.PrefetchScalarGridSpec(
        num_scalar_prefetch=5,
        grid=(nt,),
        in_specs=[pl.BlockSpec((TM, D), lambda i, *_: (i, 0)),
                  pl.BlockSpec((None, 6, D), lambda i, *_: (_group_of_tile(i, TM), 0, 0)),
                  pl.BlockSpec((TM, LANES), lambda i, *_: (i, 0)),
                  pl.BlockSpec(memory_space=pl.ANY)],
        out_specs=pl.BlockSpec((TM, D), lambda i, *_: (i, 0)),
        scratch_shapes=[pltpu.VMEM((2, N_EXPERTS, MAX_PIECES * PIECE, D), BF16), pltpu.VMEM((TM, D), F32),
                        pltpu.SemaphoreType.DMA((2, N_EXPERTS, MAX_PIECES))],
    )
    return pl.pallas_call(
        _moe_combine_kernel,
        grid_spec=grid_spec,
        out_shape=jax.ShapeDtypeStruct((N_TOK, D), F32),
        compiler_params=_cparams("arbitrary"),
        name="moe_combine",
    )(start, npc, lo, hi, off, x, mod, route, ys)


def moe_residual(x, g, mod, w_router, b_router, w_in, w_out, layer):
    nt = N_TOK // TM
    i32 = jnp.int32
    h_b, route, rows, run = moe_route(x, g, mod, w_router, b_router)
    run = run[:, 0, :N_EXPERTS].astype(i32)
    run_prev = jnp.concatenate([jnp.zeros((1, N_EXPERTS), i32), run[:-1]], axis=0)
    total = run[-1]
    padded = (total + TS - 1) // TS * TS
    off_end = jnp.cumsum(padded)
    off = off_end - padded
    experts = jnp.arange(N_EXPERTS, dtype=i32)
    off_of = lambda e_row: jnp.sum(jnp.where(e_row[None, :].astype(i32) == experts[:, None], off[:, None], 0), axis=0)
    pos1 = rows[4].astype(i32) + off_of(rows[0])
    pos2 = rows[5].astype(i32) + off_of(rows[1])
    pos_rows = jnp.concatenate([pos1[None], pos2[None], jnp.full((6, N_TOK), -1, i32)], axis=0)
    gate_rows = jnp.concatenate([rows[2:4], jnp.zeros((6, N_TOK), F32)], axis=0)
    def tile_tables(tile):
        tile_start = jnp.arange(N_SLOTS // tile, dtype=i32) * tile
        texp = jnp.minimum(jnp.sum((tile_start[:, None] >= off_end[None, :]).astype(i32), axis=1), N_EXPERTS - 1)
        mine = texp[:, None] == experts[None, :]
        of_expert = lambda per_e: jnp.sum(jnp.where(mine, per_e[None, :], 0), axis=1)
        of_tile = lambda per_e: jnp.sum(jnp.where(mine[:, None, :], per_e[None], 0), axis=2)
        rank0 = tile_start - of_expert(off)
        rank1 = jnp.minimum(rank0 + tile, of_expert(total))
        used = rank0 < rank1
        clo = jnp.sum((of_tile(run) <= rank0[:, None]).astype(i32), axis=1)
        chi = jnp.sum((of_tile(run_prev) < rank1[:, None]).astype(i32), axis=1) - 1
        clo = jnp.where(used, jnp.minimum(clo, nt - 1), 0).astype(i32)
        chi = jnp.where(used, chi, -1).astype(i32)
        return texp, used.astype(i32), clo, chi

    _, used_g, clo, chi = tile_tables(TG)
    texp, used, _, _ = tile_tables(TS)
    xs, gate_col = moe_gather(used_g, clo, chi, pos_rows, gate_rows, h_b)
    ys = moe_ffn(texp, used, xs, gate_col, w_in, w_out, layer)
    lo = off[None, :] + run_prev
    hi = off[None, :] + run
    start = jnp.minimum(lo // 16 * 16, N_SLOTS - MAX_PIECES * PIECE)
    npc = jnp.where(hi > lo, (hi - start + PIECE - 1) // PIECE, 0)
    flat = lambda a: a.reshape(nt * N_EXPERTS).astype(i32)
    return moe_combine(flat(start), flat(npc), flat(lo), flat(hi), off.astype(i32), x, mod, route, ys)


def _final_norm_kernel(x_ref, g_ref, o_ref):
    o_ref[...] = _rms(x_ref[...], g_ref[...])


def final_norm(x, g, row0, n_rows):
    base = row0 // TM
    return pl.pallas_call(
        _final_norm_kernel,
        grid=(n_rows // TM,),
        in_specs=[pl.BlockSpec((TM, D), lambda i: (base + i, 0)), pl.BlockSpec((1, D), lambda i: (0, 0))],
        out_specs=pl.BlockSpec((TM, D), lambda i: (i, 0)),
        out_shape=jax.ShapeDtypeStruct((n_rows, D), F32),
        compiler_params=_cparams("parallel"),
        name="final_norm",
    )(x, g)


def _rope_tables():
    half = 16
    freqs = ROPE_THETA ** (-jnp.arange(half, dtype=F32) / half)
    t = jnp.arange(DEC_SEQ, dtype=jnp.int32)
    row = (t // GRID_W).astype(F32)[:, None] * freqs[None, :]
    col = (t % GRID_W).astype(F32)[:, None] * freqs[None, :]
    cos = jnp.concatenate([jnp.cos(row), jnp.cos(row), jnp.cos(col), jnp.cos(col)], axis=1)
    sin = jnp.concatenate([-jnp.sin(row), jnp.sin(row), -jnp.sin(col), jnp.sin(col)], axis=1)
    cos = jnp.concatenate([jnp.ones((TM, 64), F32), cos], axis=0)
    sin = jnp.concatenate([jnp.zeros((TM, 64), F32), sin], axis=0)
    return jnp.tile(cos, (1, 2)), jnp.tile(sin, (1, 2))


def _mla_weights(w_dq, w_uq, w_dkv, w_ukv, w_o):
    w1 = jnp.concatenate([w_dq, w_dkv, jnp.zeros((D, LANES - MLA_D_ROPE), F32)], axis=1).astype(BF16)
    uq = w_uq.reshape(MLA_Q_RANK, MLA_HEADS, MLA_DK)
    wuq = jnp.concatenate([uq[:, :, :MLA_D_NOPE].reshape(MLA_Q_RANK, -1),
                           uq[:, :, MLA_D_NOPE:].reshape(MLA_Q_RANK, -1)], axis=1).astype(BF16)
    ukv = w_ukv.reshape(MLA_KV_RANK, MLA_HEADS, MLA_D_NOPE + MLA_D_V)
    wukv = jnp.concatenate([ukv[:, :, :MLA_D_NOPE].reshape(MLA_KV_RANK, -1),
                            ukv[:, :, MLA_D_NOPE:].reshape(MLA_KV_RANK, -1)], axis=1).astype(BF16)
    return w1, wuq, wukv, w_o.astype(BF16)


def _s5_weights(a_re, a_im, log_dt, b_re, b_im, c_re, c_im, seg_len):
    dt = jnp.exp(log_dt)[..., None]
    mag = jnp.exp(a_re * dt)
    abar_re, abar_im = mag * jnp.cos(a_im * dt), mag * jnp.sin(a_im * dt)
    mag_n = jnp.exp(a_re * dt * seg_len)
    apow_re, apow_im = mag_n * jnp.cos(a_im * dt * seg_len), mag_n * jnp.sin(a_im * dt * seg_len)
    den = a_re * a_re + a_im * a_im
    coef_re = ((abar_re - 1.0) * a_re + abar_im * a_im) / den
    coef_im = (abar_im * a_re - (abar_re - 1.0) * a_im) / den
    bbar_re = coef_re[..., None] * b_re - coef_im[..., None] * b_im
    bbar_im = coef_re[..., None] * b_im + coef_im[..., None] * b_re
    eye = jnp.eye(S5_GB, dtype=F32)

    def in_block(m):
        m = m.reshape(2, S5_NGB, S5_GB, S5_STATE, S5_GROUP)
        return jnp.einsum('dbgpc,gh->dbgchp', m, eye).reshape(2, S5_NGB, LANES, S5_HALF)

    def out_block(m):
        m = m.reshape(2, S5_NGB, S5_GB, S5_GROUP, S5_STATE)
        return jnp.einsum('dbgcp,gh->dbgphc', m, eye).reshape(2, S5_NGB, S5_HALF, LANES)

    wb = jnp.concatenate([in_block(bbar_re), in_block(bbar_im)], axis=3).astype(BF16)
    wc = jnp.concatenate([out_block(c_re), out_block(-c_im)], axis=2).astype(BF16)
    lanes = lambda m: m.reshape(2, S5_NGB, 1, S5_HALF)
    a = jnp.concatenate([lanes(abar_re), lanes(abar_im)], axis=2)
    an = jnp.concatenate([lanes(apow_re), lanes(apow_im)], axis=2)
    return wb, wc, a, an


def kernel(x_prompt, x_sample, c, c_ctx, cache_mla_ckv, cache_mla_krope, state_s5_re, state_s5_im, cache_diff_k, cache_diff_v, ada_w, ada_b, norm_mix, norm_ffn, norm_final, mla_w_dq, mla_q_norm, mla_w_uq, mla_w_dkv, mla_kv_norm, mla_w_ukv, mla_w_o, s5_a_re, s5_a_im, s5_log_dt, s5_b_re, s5_b_im, s5_c_re, s5_c_im, s5_d, s5_w_glu, diff_w_qkv, diff_lq1, diff_lk1, diff_lq2, diff_lk2, diff_subln, diff_w_o, ffn_w_in, ffn_w_out, moe_w_router, moe_b_router, moe_w_in, moe_w_out):
    x = jnp.concatenate([x_prompt.reshape(N_P, D), x_sample.reshape(N_S, D)], axis=0)
    cond8 = jnp.concatenate([c_ctx[None], c, jnp.zeros((8 - N_GROUPS, D), F32)], axis=0)
    mods = ada_all(cond8, ada_w, ada_b).reshape(DEPTH, 8, 6, D)[:, :N_GROUPS]
    cos_t, sin_t = _rope_tables()

    ffn_w_in_b, ffn_w_out_b = ffn_w_in.astype(BF16), ffn_w_out.astype(BF16)
    moe_w_in_b, moe_w_out_b = moe_w_in.astype(BF16), moe_w_out.astype(BF16)
    new_ckv, new_kr, new_s5_re, new_s5_im, new_dk, new_dv = [], [], [], [], [], []
    for i in range(DEPTH):
        mod = mods[i]
        gmix = norm_mix[i].reshape(1, D)
        gffn = norm_ffn[i].reshape(1, D)
        j = i // 3
        kind = i % 3
        if kind == 0:
            w1, wuq, wukv, wo = _mla_weights(mla_w_dq[j], mla_w_uq[j], mla_w_dkv[j], mla_w_ukv[j], mla_w_o[j])
            q3, ckv, kr = mla_tokens(x, gmix, mod, w1, mla_q_norm[j].reshape(1, -1), wuq,
                                     mla_kv_norm[j].reshape(1, -1), cos_t, sin_t)
            ckv_p = ckv[:N_P].reshape(BATCH, SEQ, MLA_KV_RANK)
            kr_p = kr[:N_P].reshape(BATCH, SEQ, MLA_D_ROPE)
            new_ckv.append(ckv_p)
            new_kr.append(kr_p)
            ckv_s = jnp.concatenate([cache_mla_ckv[:, j], ckv[N_P:].reshape(DEC_BATCH, DEC_SEQ, -1)], axis=1)
            kr_s = jnp.concatenate([cache_mla_krope[:, j], kr[N_P:].reshape(DEC_BATCH, DEC_SEQ, -1)], axis=1)
            k3p, v3p = mla_kv(ckv_p, kr_p, wukv, SEQ)
            k3s, v3s = mla_kv(ckv_s, kr_s, wukv, 512)
            o_p = mla_attention(q3, k3p, v3p, 0, SEQ, SEQ, MLA_HEADS)
            o_s = mla_attention(q3, k3s, v3s, N_P, DEC_SEQ, 256, 2)
            x = proj_residual(o_p, o_s, wo, x, mod)
        elif kind == 1:
            h = normmod_time_major(x, gmix, mod).reshape(N_TOK, D)
            dsk = s5_d[j].reshape(1, D)
            seg = S5_SEG
            wb, wc, a, an = _s5_weights(s5_a_re[j], s5_a_im[j], s5_log_dt[j], s5_b_re[j], s5_b_im[j],
                                        s5_c_re[j], s5_c_im[j], seg)
            zero_h0 = jnp.zeros((BATCH // S5_SUB, 2, S5_NGB, 1, 2 * S5_HALF), F32)
            y_p, fin = s5_scan(h, wb, wc, a, an, dsk, zero_h0, 0, BATCH // S5_SUB, SEQ, False)
            fin = fin.reshape(BATCH // S5_SUB, 2, S5_NGB, S5_SUB, 2, S5_GB, S5_STATE)
            fin = jnp.transpose(fin, (0, 3, 1, 4, 2, 5, 6)).reshape(BATCH, 2, 2, S5_GROUPS, S5_STATE)
            new_s5_re.append(fin[:, :, 0])
            new_s5_im.append(fin[:, :, 1])
            h0 = jnp.stack([state_s5_re[:, j], state_s5_im[:, j]], axis=2)
            h0 = h0.reshape(DEC_BATCH, 2, 2, S5_NGB, S5_HALF)
            h0 = jnp.transpose(h0, (0, 1, 3, 2, 4)).reshape(DEC_BATCH, 2, S5_NGB, 1, 2 * S5_HALF)
            y_s, _ = s5_scan(h, wb, wc, a, an, dsk, h0, N_P, DEC_BATCH, seg, True)
            x = glu_residual(y_p, y_s, s5_w_glu[j].astype(BF16), x, mod)
        else:
            lam_init = 0.8 - 0.6 * math.exp(-0.3 * i)
            wqkv = diff_w_qkv[j].astype(BF16)
            q_p, k_p, v_p, k, v = diff_tokens(x, gmix, mod, wqkv, cos_t, sin_t, latent=False)
            q_s, k_s, v_s = diff_tokens(x, gmix, mod, wqkv, cos_t, sin_t, latent=True)
            new_dk.append(k.reshape(BATCH, SEQ, 2 * DIFF_HEADS, DIFF_DH))
            new_dv.append(v.reshape(BATCH, SEQ, DIFF_HEADS, 2 * DIFF_DH))
            lvecs = [a_.reshape(1, DIFF_DH) for a_ in (diff_lq1[j], diff_lk1[j], diff_lq2[j], diff_lk2[j])]
            subln = diff_subln[j].reshape(1, 2 * DIFF_DH)
            ctx_kv = [(k_p.reshape(BATCH, SEQ, D), v_p.reshape(BATCH, SEQ, D))]
            lat_kv = [(cache_diff_k[:, j].reshape(DEC_BATCH, PAST, D).astype(BF16),
                       cache_diff_v[:, j].reshape(DEC_BATCH, PAST, D).astype(BF16)),
                      (k_s.reshape(DEC_BATCH, DEC_SEQ, D), v_s.reshape(DEC_BATCH, DEC_SEQ, D))]
            o_p = diff_attention(lam_init, lvecs, subln, q_p, ctx_kv, SEQ, SEQ, DIFF_HEADS)
            o_s = diff_attention(lam_init, lvecs, subln, q_s, lat_kv, DEC_SEQ, 256, 1)
            x = proj_residual(o_p, o_s, diff_w_o[j].astype(BF16), x, mod)
        f = i // 2
        if i % 2 == 0:
            x = ffn_residual(x, gffn, mod, ffn_w_in_b, ffn_w_out_b, f)
        else:
            wr = jnp.concatenate([moe_w_router[f], jnp.zeros((D, LANES - N_EXPERTS), F32)], axis=1).astype(BF16)
            br = jnp.concatenate([moe_b_router[f], jnp.zeros((LANES - N_EXPERTS,), F32)]).reshape(1, LANES)
            x = moe_residual(x, gffn, mod, wr, br, moe_w_in_b, moe_w_out_b, f)
    y_p = final_norm(x, norm_final.reshape(1, D), 0, N_P)
    y_s = final_norm(x, norm_final.reshape(1, D), N_P, N_S)
    return (y_p.reshape(BATCH, SEQ, D), y_s.reshape(DEC_BATCH, DEC_SEQ, D),
            jnp.stack(new_ckv, axis=1), jnp.stack(new_kr, axis=1),
            jnp.stack(new_s5_re, axis=1), jnp.stack(new_s5_im, axis=1),
            jnp.stack(new_dk, axis=1), jnp.stack(new_dv, axis=1))
```
